```python
import math
import jax
import jax.numpy as jnp
from jax import lax
import numpy as np

D_MODEL = 1024
BATCH = 8
SEQ = 2048
DEPTH = 4
DEC_BATCH = 32
DEC_SEQ = 4
PAST_LEN = 8192
PAGE_SIZE = 128

N_MIXERS = 2
N_A_LAYERS = (DEPTH + 1) // N_MIXERS
N_B_LAYERS = DEPTH // N_MIXERS
HEAD_DIM = 64
N_HEADS = D_MODEL // HEAD_DIM
GROUP_HEADS = (N_HEADS - 2 * (N_HEADS // 3), N_HEADS // 3, N_HEADS // 3)
GROUP_WINDOWS = (128, 512, 2048)
GROUP_DILATIONS = (1, 4, 16)
Q_BLOCK = 128
ROPE_THETA = 10000.0
NEG_INF = -1e30
CHUNK = 128
SGU_WIDTH = D_MODEL
SGU_GROUPS = 8
SGU_GROUP_WIDTH = SGU_WIDTH // SGU_GROUPS
N_EXPERTS = 32
TOP_K = 4
EXPERT_FF = D_MODEL
SWIGLU_LIMIT = 7.0
SWIGLU_ALPHA = 1.702
MOE_BLOCK = 128
DEEPNORM_ALPHA = (2 * DEPTH) ** 0.25
DEEPNORM_BETA = (8 * DEPTH) ** -0.25
LN_EPS = 1e-5

kernel_name = "hybrid_dilated_attn_sgu_moe_step"


def layer_norm(x, g, b):
    x32 = x.astype(jnp.float32)
    mu = jnp.mean(x32, axis=-1, keepdims=True)
    var = jnp.mean(jnp.square(x32 - mu), axis=-1, keepdims=True)
    y = (x32 - mu) * lax.rsqrt(var + LN_EPS) * g.astype(jnp.float32) + b.astype(jnp.float32)
    return y.astype(x.dtype)


def rope(t, pos):
    half = HEAD_DIM // 2
    inv_freq = ROPE_THETA ** (-jnp.arange(half, dtype=jnp.float32) / half)
    ang = pos.astype(jnp.float32)[:, None] * inv_freq[None, :]
    cos = jnp.cos(ang)[:, None, :]
    sin = jnp.sin(ang)[:, None, :]
    t32 = t.astype(jnp.float32)
    t1, t2 = t32[..., :half], t32[..., half:]
    return jnp.concatenate([t1 * cos - t2 * sin, t2 * cos + t1 * sin], axis=-1).astype(t.dtype)


def softmax_with_lse(s, valid):
    s = jnp.where(valid, s, NEG_INF)
    m = jnp.max(s, axis=-1, keepdims=True)
    p = jnp.exp(s - m)
    l = jnp.sum(p, axis=-1, keepdims=True)
    return p / l, (m + jnp.log(l))[..., 0]


def group_specs():
    specs, h0 = [], 0
    for hg, win, dil in zip(GROUP_HEADS, GROUP_WINDOWS, GROUP_DILATIONS):
        specs.append((h0, h0 + hg, win, dil))
        h0 += hg
    return specs


def dilated_group_prompt(q, k, v, dil, n_back):
    b, s, h, e = q.shape
    span = dil * Q_BLOCK
    s_pad = -(-s // span) * span
    m_len = s_pad // dil
    nb = m_len // Q_BLOCK

    def to_blocks(t):
        t = jnp.pad(t.astype(jnp.float32), ((0, 0), (0, s_pad - s), (0, 0), (0, 0)))
        t = t.reshape(b, m_len, dil, h, e).transpose(0, 2, 3, 1, 4)
        return t.reshape(b, dil, h, nb, Q_BLOCK, e)

    def with_prev(t):
        tp = jnp.pad(t, ((0, 0), (0, 0), (0, 0), (1, 0), (0, 0), (0, 0)))
        return jnp.concatenate([tp[:, :, :, :-1], tp[:, :, :, 1:]], axis=4)

    qb = to_blocks(q)
    kb = with_prev(to_blocks(k))
    vb = with_prev(to_blocks(v))
    scores = jnp.einsum("brhnqe,brhnke->brhnqk", qb, kb) * (e ** -0.5)
    qi = jnp.arange(Q_BLOCK)[:, None]
    kj = jnp.arange(2 * Q_BLOCK)[None, :]
    delta = Q_BLOCK + qi - kj
    band = (delta >= 0) & (delta <= n_back)
    has_prev = (jnp.arange(nb)[:, None, None] > 0) | (kj[None] >= Q_BLOCK)
    p, lse = softmax_with_lse(scores, band[None] & has_prev)
    o = jnp.einsum("brhnqk,brhnke->brhnqe", p, vb)
    o = o.reshape(b, dil, h, m_len, e).transpose(0, 3, 1, 2, 4).reshape(b, s_pad, h, e)[:, :s]
    lse = lse.reshape(b, dil, h, m_len).transpose(0, 3, 1, 2).reshape(b, s_pad, h)[:, :s]
    return o, lse


def dilated_group_sample(q, k_new, v_new, buf, dil, n_back):
    b, s, h, e = q.shape
    lb = buf.shape[1]
    ext_k = jnp.concatenate([buf[:, :, 0], k_new], axis=1)
    ext_v = jnp.concatenate([buf[:, :, 1], v_new], axis=1)
    idx = lb + jnp.arange(s)[:, None] - dil * jnp.arange(n_back + 1)[None, :]
    valid = idx >= 0
    idx = jnp.maximum(idx, 0)
    kg = ext_k[:, idx].astype(jnp.float32)
    vg = ext_v[:, idx].astype(jnp.float32)
    scores = jnp.einsum("bjhe,bjkhe->bhjk", q.astype(jnp.float32), kg) * (e ** -0.5)
    p, lse = softmax_with_lse(scores, valid)
    o = jnp.einsum("bhjk,bjkhe->bjhe", p, vg)
    new_buf = jnp.stack([ext_k[:, s:], ext_v[:, s:]], axis=2)
    return o, lse.transpose(0, 2, 1), new_buf


def combine_groups(outs, lses, dtype):
    n_groups = len(outs)
    group_lse = jnp.stack([jax.nn.logsumexp(l, axis=-1) - math.log(l.shape[-1]) for l in lses], axis=-1)
    w = jax.nn.softmax(group_lse, axis=-1) * n_groups
    y = jnp.concatenate([o * w[..., g, None, None] for g, o in enumerate(outs)], axis=2)
    return y.reshape(y.shape[0], y.shape[1], N_HEADS * HEAD_DIM).astype(dtype)


def split_qkv(x, w_qkv, pos):
    b, s, _ = x.shape
    qkv = (x @ w_qkv).reshape(b, s, 3, N_HEADS, HEAD_DIM)
    return rope(qkv[:, :, 0], pos), rope(qkv[:, :, 1], pos), qkv[:, :, 2]


def attn_prompt(x, w_qkv, w_o):
    s = x.shape[1]
    q, k, v = split_qkv(x, w_qkv, jnp.arange(s, dtype=jnp.int32))
    outs, lses, rows = [], [], []
    for h0, h1, win, dil in group_specs():
        o, lse = dilated_group_prompt(q[:, :, h0:h1], k[:, :, h0:h1], v[:, :, h0:h1], dil, win // dil)
        outs.append(o)
        lses.append(lse)
        keep = min(win, s)
        rows.append(jnp.stack([k[:, s - keep:, h0:h1], v[:, s - keep:, h0:h1]], axis=2))
    return combine_groups(outs, lses, x.dtype) @ w_o, rows


def attn_sample(x, bufs, w_qkv, w_o):
    s = x.shape[1]
    q, k, v = split_qkv(x, w_qkv, PAST_LEN + jnp.arange(s, dtype=jnp.int32))
    outs, lses, new_bufs = [], [], []
    for (h0, h1, win, dil), buf in zip(group_specs(), bufs):
        o, lse, nbuf = dilated_group_sample(q[:, :, h0:h1], k[:, :, h0:h1], v[:, :, h0:h1], buf, dil, win // dil)
        outs.append(o)
        lses.append(lse)
        new_bufs.append(nbuf)
    return combine_groups(outs, lses, x.dtype) @ w_o, new_bufs


def spatial_gating_mlp(x, w_in, b_in, ln_g, ln_b, w_s, b_s, w_out):
    b, s, _ = x.shape
    c = min(CHUNK, s)
    z = jax.nn.gelu(x @ w_in + b_in, approximate=False)
    u, v = z[..., :SGU_WIDTH], z[..., SGU_WIDTH:]
    v = layer_norm(v, ln_g, ln_b)
    vc = v.reshape(b, s // c, c, SGU_GROUPS, SGU_GROUP_WIDTH)
    ws = w_s[:, :c, :c] * jnp.tril(jnp.ones((c, c), w_s.dtype))
    gate = jnp.einsum("gij,bnjge->bnige", ws, vc) + b_s[:, :c].T[:, :, None]
    y = (u * gate.reshape(b, s, SGU_WIDTH)) @ w_out
    return y, v


def clamped_swiglu(gate, up):
    gate = jnp.minimum(gate, SWIGLU_LIMIT)
    up = jnp.clip(up, -SWIGLU_LIMIT, SWIGLU_LIMIT)
    return (up + 1.0) * gate * jax.nn.sigmoid(SWIGLU_ALPHA * gate)


def moe_ffn(x, w_r, b_r, w_gu, b_gu, w_dn, b_dn):
    b, s, d = x.shape
    t = b * s
    xt = x.reshape(t, d)
    logits = xt.astype(jnp.float32) @ w_r.astype(jnp.float32) + b_r.astype(jnp.float32)
    top_val, top_idx = lax.top_k(logits, TOP_K)
    gates = jax.nn.softmax(top_val, axis=-1).astype(x.dtype)
    n_rows = t * TOP_K
    e_flat = top_idx.reshape(n_rows).astype(jnp.int32)
    tok_flat = jnp.arange(n_rows, dtype=jnp.int32) // TOP_K
    order = jnp.argsort(e_flat)
    e_sorted = e_flat[order]
    counts = jnp.zeros((N_EXPERTS,), jnp.int32).at[e_flat].add(1)
    padded = (counts + MOE_BLOCK - 1) // MOE_BLOCK * MOE_BLOCK
    pad_end = jnp.cumsum(padded)
    pad_start = pad_end - padded
    start = jnp.cumsum(counts) - counts
    dest = pad_start[e_sorted] + jnp.arange(n_rows, dtype=jnp.int32) - start[e_sorted]
    n_blocks = -(-n_rows // MOE_BLOCK) + N_EXPERTS
    n_slots = n_blocks * MOE_BLOCK
    slot_tok = jnp.full((n_slots,), t, jnp.int32).at[dest].set(tok_flat[order])
    slot_gate = jnp.zeros((n_slots,), x.dtype).at[dest].set(gates.reshape(n_rows)[order])
    block_start = jnp.arange(n_blocks, dtype=jnp.int32) * MOE_BLOCK
    block_expert = jnp.minimum(jnp.searchsorted(pad_end, block_start, side="right"), N_EXPERTS - 1)
    x_pad = jnp.concatenate([xt, jnp.zeros((1, d), x.dtype)], axis=0)

    def expert_block(args):
        e, toks, g = args
        h = x_pad[toks] @ w_gu[e] + b_gu[e]
        y = clamped_swiglu(h[:, :EXPERT_FF], h[:, EXPERT_FF:]) @ w_dn[e] + b_dn[e]
        return y * g[:, None]

    ys = lax.map(expert_block, (block_expert, slot_tok.reshape(n_blocks, MOE_BLOCK), slot_gate.reshape(n_blocks, MOE_BLOCK)))
    out = jax.ops.segment_sum(ys.reshape(n_slots, d), slot_tok, num_segments=t + 1)[:t]
    return out.reshape(b, s, d)


def setup_inputs(seed: int = 0) -> dict:
    key = jax.random.key(seed)
    ks = iter(list(jax.random.split(key, 32)))

    def nrm(shape, scale):
        return jax.random.normal(next(ks), shape, jnp.float32) * scale

    inp = {}
    inp["x_prompt"] = nrm((BATCH, SEQ, D_MODEL), 1.0)
    inp["x_sample"] = nrm((DEC_BATCH, DEC_SEQ, D_MODEL), 1.0)
    inp["cache_kv_w128"] = nrm((N_A_LAYERS, DEC_BATCH, min(GROUP_WINDOWS[0], PAST_LEN), 2, GROUP_HEADS[0], HEAD_DIM), 1.0)
    inp["cache_kv_w512"] = nrm((N_A_LAYERS, DEC_BATCH, min(GROUP_WINDOWS[1], PAST_LEN), 2, GROUP_HEADS[1], HEAD_DIM), 1.0)
    inp["cache_kv_w2048"] = nrm((N_A_LAYERS, DEC_BATCH, min(GROUP_WINDOWS[2], PAST_LEN), 2, GROUP_HEADS[2], HEAD_DIM), 1.0)
    inp["attn_w_qkv"] = nrm((N_A_LAYERS, D_MODEL, 3 * N_HEADS * HEAD_DIM), D_MODEL ** -0.5)
    inp["attn_w_o"] = nrm((N_A_LAYERS, N_HEADS * HEAD_DIM, D_MODEL), (N_HEADS * HEAD_DIM) ** -0.5 * DEEPNORM_BETA)
    inp["sgu_w_in"] = nrm((N_B_LAYERS, D_MODEL, 2 * SGU_WIDTH), D_MODEL ** -0.5)
    inp["sgu_b_in"] = nrm((N_B_LAYERS, 2 * SGU_WIDTH), 0.02)
    inp["sgu_ln_g"] = 1.0 + nrm((N_B_LAYERS, SGU_WIDTH), 0.1)
    inp["sgu_ln_b"] = nrm((N_B_LAYERS, SGU_WIDTH), 0.02)
    inp["sgu_w_s"] = nrm((N_B_LAYERS, SGU_GROUPS, CHUNK, CHUNK), CHUNK ** -0.5)
    inp["sgu_b_s"] = 1.0 + nrm((N_B_LAYERS, SGU_GROUPS, CHUNK), 0.1)
    inp["sgu_w_out"] = nrm((N_B_LAYERS, SGU_WIDTH, D_MODEL), SGU_WIDTH ** -0.5 * DEEPNORM_BETA)
    inp["moe_w_router"] = nrm((DEPTH, D_MODEL, N_EXPERTS), D_MODEL ** -0.5)
    inp["moe_b_router"] = nrm((DEPTH, N_EXPERTS), 0.01)
    inp["moe_w_gu"] = nrm((DEPTH, N_EXPERTS, D_MODEL, 2 * EXPERT_FF), D_MODEL ** -0.5)
    inp["moe_b_gu"] = nrm((DEPTH, N_EXPERTS, 2 * EXPERT_FF), 0.02)
    inp["moe_w_down"] = nrm((DEPTH, N_EXPERTS, EXPERT_FF, D_MODEL), EXPERT_FF ** -0.5 * DEEPNORM_BETA)
    inp["moe_b_down"] = nrm((DEPTH, N_EXPERTS, D_MODEL), 0.02)
    inp["ln1_g"] = 1.0 + nrm((DEPTH, D_MODEL), 0.1)
    inp["ln1_b"] = nrm((DEPTH, D_MODEL), 0.02)
    inp["ln2_g"] = 1.0 + nrm((DEPTH, D_MODEL), 0.1)
    inp["ln2_b"] = nrm((DEPTH, D_MODEL), 0.02)
    return inp


def reference(x_prompt, x_sample, cache_kv_w128, cache_kv_w512, cache_kv_w2048,
              attn_w_qkv, attn_w_o, sgu_w_in, sgu_b_in, sgu_ln_g, sgu_ln_b, sgu_w_s, sgu_b_s, sgu_w_out,
              moe_w_router, moe_b_router, moe_w_gu, moe_b_gu, moe_w_down, moe_b_down,
              ln1_g, ln1_b, ln2_g, ln2_b):
    caches = (cache_kv_w128, cache_kv_w512, cache_kv_w2048)
    xp, xs = x_prompt, x_sample
    kv_prompt = [[] for _ in caches]
    kv_sample = [[] for _ in caches]
    v_rows = []
    for i in range(DEPTH):
        j = i // N_MIXERS
        if i % N_MIXERS == 0:
            yp, rows_p = attn_prompt(xp, attn_w_qkv[j], attn_w_o[j])
            ys, rows_s = attn_sample(xs, [c[j] for c in caches], attn_w_qkv[j], attn_w_o[j])
            for g in range(len(caches)):
                kv_prompt[g].append(rows_p[g])
                kv_sample[g].append(rows_s[g])
        else:
            sgu_args = (sgu_w_in[j], sgu_b_in[j], sgu_ln_g[j], sgu_ln_b[j], sgu_w_s[j], sgu_b_s[j], sgu_w_out[j])
            yp = spatial_gating_mlp(xp, *sgu_args)[0]
            ys, v_new = spatial_gating_mlp(xs, *sgu_args)
            v_rows.append(v_new)
        xp = layer_norm(DEEPNORM_ALPHA * xp + yp, ln1_g[i], ln1_b[i])
        xs = layer_norm(DEEPNORM_ALPHA * xs + ys, ln1_g[i], ln1_b[i])
        moe_args = (moe_w_router[i], moe_b_router[i], moe_w_gu[i], moe_b_gu[i], moe_w_down[i], moe_b_down[i])
        xp = layer_norm(DEEPNORM_ALPHA * xp + moe_ffn(xp, *moe_args), ln2_g[i], ln2_b[i])
        xs = layer_norm(DEEPNORM_ALPHA * xs + moe_ffn(xs, *moe_args), ln2_g[i], ln2_b[i])
    kv_w128_prompt = jnp.stack(kv_prompt[0])
    kv_w512_prompt = jnp.stack(kv_prompt[1])
    kv_w2048_prompt = jnp.stack(kv_prompt[2])
    kv_w128_sample = jnp.stack(kv_sample[0])
    kv_w512_sample = jnp.stack(kv_sample[1])
    kv_w2048_sample = jnp.stack(kv_sample[2])
    sgu_v_sample = jnp.stack(v_rows)
    return (xp, xs, kv_w128_prompt, kv_w512_prompt, kv_w2048_prompt, kv_w128_sample, kv_w512_sample, kv_w2048_sample, sgu_v_sample)
```

```python
import functools
import math

import jax
import jax.numpy as jnp
from jax import lax
from jax.experimental import pallas as pl
from jax.experimental.pallas import tpu as pltpu

F32 = jnp.float32
BF16 = jnp.bfloat16

D_MODEL = 1024
HEAD_DIM = 64
N_HEADS = D_MODEL // HEAD_DIM
GROUPS = ((0, 6, 128, 1), (6, 11, 512, 4), (11, 16, 2048, 16))
N_GROUPS = len(GROUPS)
Q_BLOCK = 128
SEC = 384
ROPE_THETA = 10000.0
NEG_INF = -1e30
PAST_LEN = 8192
SGU_GROUPS = 8
SGU_CHUNK = 128
N_EXPERTS = 32
TOP_K = 4
SWIGLU_LIMIT = 7.0
SWIGLU_ALPHA = 1.702
MOE_BLOCK = 256
DEPTH = 4
DEEPNORM_ALPHA = (2 * DEPTH) ** 0.25
LN_EPS = 1e-5
LANES = 128
MIB = 1024 * 1024


def _params(vmem_mib, *semantics):
    return pltpu.CompilerParams(dimension_semantics=semantics, vmem_limit_bytes=vmem_mib * MIB)


def _ln(x, g, b):
    mu = jnp.mean(x, axis=-1, keepdims=True)
    xc = x - mu
    var = jnp.mean(xc * xc, axis=-1, keepdims=True)
    return xc * lax.rsqrt(var + LN_EPS) * g + b


def _qkv_prompt_kernel(x_ref, w_ref, cos_ref, sin_ref, o1_ref, o2_ref, o3_ref, kv_ref, y_s, *, bm):
    y = jnp.dot(x_ref[...].astype(BF16), w_ref[...], preferred_element_type=F32)
    cos = cos_ref[...]
    sin = sin_ref[...]
    lane = lax.broadcasted_iota(jnp.int32, (bm, LANES), 1)
    first_half = (lane & (HEAD_DIM // 2)) == 0
    blocks_per_sec = SEC // LANES
    n_rot = 2 * N_GROUPS * blocks_per_sec
    for c in range(3 * N_GROUPS * blocks_per_sec):
        blk = y[:, c * LANES:(c + 1) * LANES]
        if c < n_rot:
            swapped = jnp.where(first_half, pltpu.roll(blk, LANES - HEAD_DIM // 2, 1), pltpu.roll(blk, HEAD_DIM // 2, 1))
            blk = blk * cos + swapped * sin
        if c < n_rot // 2:
            blk = blk * (HEAD_DIM ** -0.5)
        else:
            kv_ref[:, (c - n_rot // 2) * LANES:(c - n_rot // 2 + 1) * LANES] = blk
        y_s[c] = blk
    for g, o_ref in enumerate((o1_ref, o2_ref, o3_ref)):
        d = GROUPS[g][3]
        n = bm // d
        for r in range(d):
            for part in range(3):
                for p in range(blocks_per_sec):
                    c = (part * N_GROUPS + g) * blocks_per_sec + p
                    rows = y_s[c] if d == 1 else y_s[c, pl.ds(r, n, stride=d), :]
                    o_ref[0, r, :, part * SEC + p * LANES:part * SEC + (p + 1) * LANES] = rows.astype(BF16)


def _qkv_prompt(x_all, w_perm, cos, sin, batch, seq):
    bm = 256
    tiles = seq // bm
    out_shape = [jax.ShapeDtypeStruct((batch, d, seq // d, 3 * SEC), BF16) for (_, _, _, d) in GROUPS]
    out_shape.append(jax.ShapeDtypeStruct((batch * seq, 2 * N_GROUPS * SEC), F32))
    out_specs = [pl.BlockSpec((1, d, bm // d, 3 * SEC), lambda b, i: (b, 0, i, 0)) for (_, _, _, d) in GROUPS]
    out_specs.append(pl.BlockSpec((bm, 2 * N_GROUPS * SEC), lambda b, i: (b * tiles + i, 0)))
    n_cols = 3 * N_GROUPS * SEC
    return pl.pallas_call(
        functools.partial(_qkv_prompt_kernel, bm=bm),
        grid=(batch, tiles),
        in_specs=[
            pl.BlockSpec((bm, D_MODEL), lambda b, i: (b * tiles + i, 0)),
            pl.BlockSpec((D_MODEL, n_cols), lambda b, i: (0, 0)),
            pl.BlockSpec((bm, LANES), lambda b, i: (i, 0)),
            pl.BlockSpec((bm, LANES), lambda b, i: (i, 0)),
        ],
        out_specs=out_specs,
        out_shape=out_shape,
        scratch_shapes=[pltpu.VMEM((n_cols // LANES, bm, LANES), F32)],
        compiler_params=_params(48, "arbitrary", "arbitrary"),
        name="qkv_prompt",
    )(x_all, w_perm, cos, sin)


def _attn_group_kernel(qkv_ref, o_ref, lse_ref, *, d, m_len, n_heads):
    nb = m_len // Q_BLOCK
    kw = min(2 * Q_BLOCK, m_len)
    lane = lax.broadcasted_iota(jnp.int32, (1, LANES), 1)
    lo = lane < HEAD_DIM
    qi = lax.broadcasted_iota(jnp.int32, (Q_BLOCK, kw), 0)
    kj = lax.broadcasted_iota(jnp.int32, (Q_BLOCK, kw), 1)

    def block(i, carry):
        r = i // nb
        n = i % nb
        ks = pl.multiple_of(jnp.maximum(n - 1, 0) * Q_BLOCK, Q_BLOCK)
        qs = pl.multiple_of(n * Q_BLOCK, Q_BLOCK)
        delta = qs - ks + qi - kj
        valid = (delta >= 0) & (delta <= Q_BLOCK)
        if d == 1:
            dst = pl.ds(qs, Q_BLOCK)
        else:
            dst = pl.ds(r + qs * d, Q_BLOCK, stride=d)
        lses = []
        for p in range(SEC // LANES):
            cols = slice(p * LANES, (p + 1) * LANES)
            q2 = qkv_ref[0, r, pl.ds(qs, Q_BLOCK), cols]
            k2 = qkv_ref[0, r, pl.ds(ks, kw), SEC + p * LANES:SEC + (p + 1) * LANES]
            v2 = qkv_ref[0, r, pl.ds(ks, kw), 2 * SEC + p * LANES:2 * SEC + (p + 1) * LANES]
            o_pair = jnp.zeros((Q_BLOCK, LANES), F32)
            for half in range(2):
                if 2 * p + half >= n_heads:
                    continue
                msk = lo if half == 0 else jnp.logical_not(lo)
                qm = jnp.where(msk, q2, jnp.zeros_like(q2))
                vm = jnp.where(msk, v2, jnp.zeros_like(v2))
                s = lax.dot_general(qm, k2, (((1,), (1,)), ((), ())), preferred_element_type=F32)
                s = jnp.where(valid, s, NEG_INF)
                mx = jnp.max(s, axis=1, keepdims=True)
                pe = jnp.exp(s - mx)
                l = jnp.sum(pe, axis=1, keepdims=True)
                o_pair = o_pair + jnp.dot(pe.astype(BF16), vm, preferred_element_type=F32) / l
                lses.append(mx + jnp.log(l))
            o_ref[0, p, dst, :] = o_pair
        lmax = functools.reduce(jnp.maximum, lses)
        lsum = functools.reduce(lambda a, b: a + b, [jnp.exp(l - lmax) for l in lses])
        glse = lmax + jnp.log(lsum) - math.log(n_heads)
        lse_ref[0, dst, :] = jnp.broadcast_to(glse, (Q_BLOCK, LANES))
        return carry

    lax.fori_loop(0, d * nb, block, 0)


def _attn_group(qkv_g, g, batch, seq):
    h0, h1, _, d = GROUPS[g]
    m_len = seq // d
    return pl.pallas_call(
        functools.partial(_attn_group_kernel, d=d, m_len=m_len, n_heads=h1 - h0),
        grid=(batch,),
        in_specs=[pl.BlockSpec((1, d, m_len, 3 * SEC), lambda b: (b, 0, 0, 0))],
        out_specs=[
            pl.BlockSpec((1, SEC // LANES, seq, LANES), lambda b: (b, 0, 0, 0)),
            pl.BlockSpec((1, seq, LANES), lambda b: (b, 0, 0)),
        ],
        out_shape=[
            jax.ShapeDtypeStruct((batch, SEC // LANES, seq, LANES), F32),
            jax.ShapeDtypeStruct((batch, seq, LANES), F32),
        ],
        compiler_params=_params(40, "arbitrary"),
        name=f"attn_group{g}",
    )(qkv_g)


def _attn_out_kernel(o1_ref, o2_ref, o3_ref, l1_ref, l2_ref, l3_ref, wo_ref, x_ref, g_ref, b_ref, out_ref):
    ls = [l1_ref[0], l2_ref[0], l3_ref[0]]
    mx = jnp.maximum(jnp.maximum(ls[0], ls[1]), ls[2])
    es = [jnp.exp(l - mx) for l in ls]
    inv = float(N_GROUPS) / (es[0] + es[1] + es[2])
    acc = None
    for g, o_ref in enumerate((o1_ref, o2_ref, o3_ref)):
        w = es[g] * inv
        o_g = jnp.concatenate([o_ref[0, p] * w for p in range(SEC // LANES)], axis=1)
        part = jnp.dot(o_g.astype(BF16), wo_ref[g], preferred_element_type=F32)
        acc = part if acc is None else acc + part
    out_ref[...] = _ln(DEEPNORM_ALPHA * x_ref[...] + acc, g_ref[...], b_ref[...])


def _attn_out_prompt(os_, ls_, wo_perm, x_all, ln_g, ln_b, batch, seq):
    bm = 512
    tiles = seq // bm
    t_all = x_all.shape[0]
    row = lambda b, i: (b * tiles + i, 0)
    const2 = lambda b, i: (0, 0)
    return pl.pallas_call(
        _attn_out_kernel,
        grid=(batch, tiles),
        in_specs=[pl.BlockSpec((1, SEC // LANES, bm, LANES), lambda b, i: (b, 0, i, 0))] * 3
        + [pl.BlockSpec((1, bm, LANES), lambda b, i: (b, i, 0))] * 3 + [
            pl.BlockSpec((N_GROUPS, SEC, D_MODEL), lambda b, i: (0, 0, 0)),
            pl.BlockSpec((bm, D_MODEL), row),
            pl.BlockSpec((1, D_MODEL), const2),
            pl.BlockSpec((1, D_MODEL), const2),
        ],
        out_specs=pl.BlockSpec((bm, D_MODEL), row),
        out_shape=jax.ShapeDtypeStruct((t_all, D_MODEL), F32),
        compiler_params=_params(40, "arbitrary", "arbitrary"),
        name="attn_out_prompt",
    )(*os_, *ls_, wo_perm, x_all, ln_g, ln_b)


def _mm_kernel(x_ref, w_ref, o_ref):
    o_ref[...] = jnp.dot(x_ref[...].astype(BF16), w_ref[...], preferred_element_type=F32)


def _mm_rows(x_all, w, row0, rows):
    n = w.shape[1]
    blk0 = row0 // rows
    return pl.pallas_call(
        _mm_kernel,
        grid=(1,),
        in_specs=[
            pl.BlockSpec((rows, D_MODEL), lambda i: (blk0, 0)),
            pl.BlockSpec((D_MODEL, n), lambda i: (0, 0)),
        ],
        out_specs=pl.BlockSpec((rows, n), lambda i: (0, 0)),
        out_shape=jax.ShapeDtypeStruct((rows, n), F32),
        compiler_params=_params(40, "arbitrary"),
        name="mm_rows",
    )(x_all, w)


def _proj_ln_kernel(a_ref, w_ref, x_ref, g_ref, b_ref, prev_ref, out_ref):
    del prev_ref
    acc = jnp.dot(a_ref[...].astype(BF16), w_ref[...], preferred_element_type=F32)
    out_ref[...] = _ln(DEEPNORM_ALPHA * x_ref[...] + acc, g_ref[...], b_ref[...])


def _proj_ln_rows(a, w, x_all, ln_g, ln_b, x_next, row0):
    rows = a.shape[0]
    blk0 = row0 // rows
    const2 = lambda i: (0, 0)
    return pl.pallas_call(
        _proj_ln_kernel,
        grid=(1,),
        in_specs=[
            pl.BlockSpec((rows, a.shape[1]), const2),
            pl.BlockSpec(w.shape, const2),
            pl.BlockSpec((rows, D_MODEL), lambda i: (blk0, 0)),
            pl.BlockSpec((1, D_MODEL), const2),
            pl.BlockSpec((1, D_MODEL), const2),
            pl.BlockSpec(memory_space=pl.ANY),
        ],
        out_specs=pl.BlockSpec((rows, D_MODEL), lambda i: (blk0, 0)),
        out_shape=jax.ShapeDtypeStruct(x_next.shape, F32),
        input_output_aliases={5: 0},
        compiler_params=_params(40, "arbitrary"),
        name="proj_ln_rows",
    )(a, w, x_all, ln_g, ln_b, x_next)


def _sgu_kernel(*refs, bm, chunk, emit_v, aliased):
    (x_ref, win_ref, bin_ref, lng_ref, lnb_ref, ws_ref, bs_ref, wout_ref, g1_ref, b1_ref) = refs[:10]
    rest = refs[10 + (1 if aliased else 0):]
    out_ref = rest[0]
    v_ref = rest[1] if emit_v else None
    ug_s = rest[-1]
    width = D_MODEL
    x = x_ref[...]
    z = jnp.dot(x.astype(BF16), win_ref[...], preferred_element_type=F32) + bin_ref[...]
    z = 0.5 * z * (1.0 + lax.erf(z * (2.0 ** -0.5)))
    v = _ln(z[:, width:], lng_ref[...], lnb_ref[...])
    if emit_v:
        v_ref[...] = v
    ii = lax.broadcasted_iota(jnp.int32, (SGU_CHUNK, SGU_CHUNK), 0)
    jj = lax.broadcasted_iota(jnp.int32, (SGU_CHUNK, SGU_CHUNK), 1)
    causal = (jj <= ii) & ((ii // chunk) == (jj // chunk))
    gw = width // SGU_GROUPS
    for gi in range(SGU_GROUPS):
        wsm = jnp.where(causal, ws_ref[gi], 0.0).astype(BF16)
        for c in range(bm // SGU_CHUNK):
            rows = slice(c * SGU_CHUNK, (c + 1) * SGU_CHUNK)
            cols = slice(gi * gw, (gi + 1) * gw)
            gate = jnp.dot(wsm, v[rows, cols].astype(BF16), preferred_element_type=F32) + bs_ref[gi]
            ug_s[rows, cols] = (z[rows, cols] * gate).astype(BF16)
    y = jnp.dot(ug_s[...], wout_ref[...], preferred_element_type=F32)
    out_ref[...] = _ln(DEEPNORM_ALPHA * x + y, g1_ref[...], b1_ref[...])


def _sgu(x_all, x_next, weights, row0, rows, bm, chunk, emit_v):
    w_in, b_in, ln_g, ln_b, ws, bs, w_out, g1, b1 = weights
    t_all = x_all.shape[0]
    blk0 = row0 // bm
    row = lambda i: (blk0 + i, 0)
    const2 = lambda i: (0, 0)
    const3 = lambda i: (0, 0, 0)
    aliased = x_next is not None
    in_specs = [
        pl.BlockSpec((bm, D_MODEL), row),
        pl.BlockSpec(w_in.shape, const2),
        pl.BlockSpec((1, 2 * D_MODEL), const2),
        pl.BlockSpec((1, D_MODEL), const2),
        pl.BlockSpec((1, D_MODEL), const2),
        pl.BlockSpec(ws.shape, const3),
        pl.BlockSpec(bs.shape, const3),
        pl.BlockSpec(w_out.shape, const2),
        pl.BlockSpec((1, D_MODEL), const2),
        pl.BlockSpec((1, D_MODEL), const2),
    ]
    args = [x_all, w_in, b_in, ln_g, ln_b, ws, bs, w_out, g1, b1]
    aliases = {}
    if aliased:
        in_specs.append(pl.BlockSpec(memory_space=pl.ANY))
        args.append(x_next)
        aliases = {10: 0}
    out_specs = [pl.BlockSpec((bm, D_MODEL), row)]
    out_shape = [jax.ShapeDtypeStruct((t_all, D_MODEL), F32)]
    if emit_v:
        out_specs.append(pl.BlockSpec((bm, D_MODEL), lambda i: (i, 0)))
        out_shape.append(jax.ShapeDtypeStruct((rows, D_MODEL), F32))
    return pl.pallas_call(
        functools.partial(_sgu_kernel, bm=bm, chunk=chunk, emit_v=emit_v, aliased=aliased),
        grid=(rows // bm,),
        in_specs=in_specs,
        out_specs=out_specs,
        out_shape=out_shape,
        scratch_shapes=[pltpu.VMEM((bm, D_MODEL), BF16)],
        input_output_aliases=aliases,
        compiler_params=_params(48, "arbitrary"),
        name="sgu",
    )(*args)


def _router_kernel(x_ref, wr_ref, br_ref, idx_ref, gate_ref, *, bm):
    logits = jnp.dot(x_ref[...], wr_ref[...], preferred_element_type=F32, precision=lax.Precision.HIGHEST)
    logits = logits + br_ref[...]
    lane = lax.broadcasted_iota(jnp.int32, (bm, LANES), 1)
    lane_f = lane.astype(F32)
    logits = jnp.where(lane < N_EXPERTS, logits, -jnp.inf)
    vals, idxs = [], []
    for _ in range(TOP_K):
        m = jnp.max(logits, axis=1, keepdims=True)
        i = jnp.min(jnp.where(logits == m, lane_f, float(LANES)), axis=1, keepdims=True)
        vals.append(m)
        idxs.append(i)
        logits = jnp.where(lane_f == i, -jnp.inf, logits)
    es = [jnp.exp(v - vals[0]) for v in vals]
    inv = 1.0 / functools.reduce(lambda a, b: a + b, es)
    idx_out = jnp.zeros((bm, LANES), F32)
    gate_out = jnp.zeros((bm, LANES), F32)
    for k in range(TOP_K):
        idx_out = jnp.where(lane == k, idxs[k], idx_out)
        gate_out = jnp.where(lane == k, es[k] * inv, gate_out)
    idx_ref[...] = idx_out.astype(jnp.int32)
    gate_ref[...] = gate_out


def _router(x_all, w_r, b_r):
    bm = 512
    t_all = x_all.shape[0]
    row = lambda i: (i, 0)
    const2 = lambda i: (0, 0)
    return pl.pallas_call(
        functools.partial(_router_kernel, bm=bm),
        grid=(pl.cdiv(t_all, bm),),
        in_specs=[
            pl.BlockSpec((bm, D_MODEL), row),
            pl.BlockSpec((D_MODEL, LANES), const2),
            pl.BlockSpec((1, LANES), const2),
        ],
        out_specs=[pl.BlockSpec((bm, LANES), row), pl.BlockSpec((bm, LANES), row)],
        out_shape=[
            jax.ShapeDtypeStruct((t_all, LANES), jnp.int32),
            jax.ShapeDtypeStruct((t_all, LANES), F32),
        ],
        compiler_params=_params(32, "arbitrary"),
        name="router",
    )(x_all, w_r, b_r)


def _expert_kernel(be_ref, nreal_ref, xs_ref, wgu_ref, bgu_ref, wdn_ref, bdn_ref, y_ref):
    del be_ref
    ff = wdn_ref.shape[1]

    @pl.when(pl.program_id(0) < nreal_ref[0])
    def _():
        h = jnp.dot(xs_ref[...], wgu_ref[0], preferred_element_type=F32) + bgu_ref[0]
        gate = jnp.minimum(h[:, :ff], SWIGLU_LIMIT)
        up = jnp.clip(h[:, ff:], -SWIGLU_LIMIT, SWIGLU_LIMIT)
        act = (up + 1.0) * gate * jax.nn.sigmoid(SWIGLU_ALPHA * gate)
        y_ref[...] = jnp.dot(act.astype(BF16), wdn_ref[0], preferred_element_type=F32) + bdn_ref[0]


def _experts(block_expert, n_real, xs, w_gu, b_gu, w_dn, b_dn):
    n_slots = xs.shape[0]
    n_blocks = n_slots // MOE_BLOCK
    ff = w_dn.shape[1]
    rows = lambda i, be, nr: (jnp.minimum(i, nr[0] - 1), 0)
    by_expert = lambda i, be, nr: (be[i], 0, 0)
    grid_spec = pltpu.PrefetchScalarGridSpec(
        num_scalar_prefetch=2,
        grid=(n_blocks,),
        in_specs=[
            pl.BlockSpec((MOE_BLOCK, D_MODEL), rows),
            pl.BlockSpec((1, D_MODEL, 2 * ff), by_expert),
            pl.BlockSpec((1, 1, 2 * ff), by_expert),
            pl.BlockSpec((1, ff, D_MODEL), by_expert),
            pl.BlockSpec((1, 1, D_MODEL), by_expert),
        ],
        out_specs=pl.BlockSpec((MOE_BLOCK, D_MODEL), rows),
    )
    return pl.pallas_call(
        _expert_kernel,
        grid_spec=grid_spec,
        out_shape=jax.ShapeDtypeStruct((n_slots, D_MODEL), F32),
        compiler_params=_params(40, "arbitrary"),
        name="experts",
    )(block_expert, n_real, xs, w_gu, b_gu, w_dn, b_dn)


def _merge_ln_kernel(x_ref, y_ref, g_ref, b_ref, out_ref):
    out_ref[...] = _ln(DEEPNORM_ALPHA * x_ref[...] + y_ref[...], g_ref[...], b_ref[...])


def _merge_ln(x_all, y_all, ln_g, ln_b):
    bm = 512
    t_all = x_all.shape[0]
    row = lambda i: (i, 0)
    const2 = lambda i: (0, 0)
    return pl.pallas_call(
        _merge_ln_kernel,
        grid=(pl.cdiv(t_all, bm),),
        in_specs=[
            pl.BlockSpec((bm, D_MODEL), row),
            pl.BlockSpec((bm, D_MODEL), row),
            pl.BlockSpec((1, D_MODEL), const2),
            pl.BlockSpec((1, D_MODEL), const2),
        ],
        out_specs=pl.BlockSpec((bm, D_MODEL), row),
        out_shape=jax.ShapeDtypeStruct((t_all, D_MODEL), F32),
        compiler_params=_params(32, "arbitrary"),
        name="merge_ln",
    )(x_all, y_all, ln_g, ln_b)


def _moe(x_all, w_r, b_r, w_gu, b_gu, w_dn, b_dn, ln_g, ln_b):
    t_all = x_all.shape[0]
    idx, gates = _router(x_all, w_r, b_r)
    idx = idx[:, :TOP_K]
    gates = gates[:, :TOP_K]
    n_rows = t_all * TOP_K
    e_flat = idx.reshape(n_rows)
    order = jnp.argsort(e_flat)
    e_sorted = e_flat[order]
    counts = jnp.zeros((N_EXPERTS,), jnp.int32).at[e_flat].add(1)
    padded = (counts + MOE_BLOCK - 1) // MOE_BLOCK * MOE_BLOCK
    pad_end = jnp.cumsum(padded)
    pad_start = pad_end - padded
    start = jnp.cumsum(counts) - counts
    dest_sorted = pad_start[e_sorted] + jnp.arange(n_rows, dtype=jnp.int32) - start[e_sorted]
    n_blocks = -(-n_rows // MOE_BLOCK) + N_EXPERTS
    n_slots = n_blocks * MOE_BLOCK
    slot_tok = jnp.zeros((n_slots,), jnp.int32).at[dest_sorted].set(order // TOP_K)
    pair_dest = jnp.zeros((n_rows,), jnp.int32).at[order].set(dest_sorted)
    block_start = jnp.arange(n_blocks, dtype=jnp.int32) * MOE_BLOCK
    block_expert = jnp.minimum(jnp.searchsorted(pad_end, block_start, side="right"), N_EXPERTS - 1).astype(jnp.int32)
    n_real = (pad_end[-1:] // MOE_BLOCK).astype(jnp.int32)
    xs = x_all.astype(BF16)[slot_tok]
    ys = _experts(block_expert, n_real, xs, w_gu, b_gu, w_dn, b_dn)
    y_all = jnp.sum(ys[pair_dest.reshape(t_all, TOP_K)] * gates[:, :, None], axis=1)
    return _merge_ln(x_all, y_all, ln_g, ln_b)


def _rope_tables(pos):
    half = HEAD_DIM // 2
    inv_freq = ROPE_THETA ** (-jnp.arange(half, dtype=F32) / half)
    ang = pos.astype(F32)[:, None] * inv_freq[None, :]
    cos = jnp.cos(ang)
    sin = jnp.sin(ang)
    cos = jnp.concatenate([cos, cos], axis=1)
    sin = jnp.concatenate([-sin, sin], axis=1)
    reps = LANES // HEAD_DIM
    return jnp.tile(cos, (1, reps)), jnp.tile(sin, (1, reps))


def _permute_qkv_weight(w_qkv):
    w3 = w_qkv.reshape(D_MODEL, 3, N_HEADS * HEAD_DIM)
    secs = []
    for part in range(3):
        for (h0, h1, _, _) in GROUPS:
            sec = w3[:, part, h0 * HEAD_DIM:h1 * HEAD_DIM]
            secs.append(jnp.pad(sec, ((0, 0), (0, SEC - sec.shape[1]))))
    return jnp.concatenate(secs, axis=1).astype(BF16)


def _permute_out_weight(w_o):
    secs = []
    for (h0, h1, _, _) in GROUPS:
        sec = w_o[h0 * HEAD_DIM:h1 * HEAD_DIM]
        secs.append(jnp.pad(sec, ((0, SEC - sec.shape[0]), (0, 0))))
    return jnp.stack(secs).astype(BF16)


def _sample_attention(y, bufs, dec_b, dec_s):
    pos = PAST_LEN + jnp.arange(dec_s, dtype=jnp.int32)
    cos, sin = _rope_tables(pos)
    qkv = y.reshape(dec_b, dec_s, 3, N_HEADS, HEAD_DIM)
    c = cos[None, :, None, :HEAD_DIM // 2]
    s = sin[None, :, None, HEAD_DIM // 2:HEAD_DIM]

    def rot(t):
        t1, t2 = t[..., :HEAD_DIM // 2], t[..., HEAD_DIM // 2:]
        return jnp.concatenate([t1 * c - t2 * s, t2 * c + t1 * s], axis=-1)

    q, k, v = rot(qkv[:, :, 0]), rot(qkv[:, :, 1]), qkv[:, :, 2]
    outs, glses, new_bufs = [], [], []
    for (h0, h1, win, dil), buf in zip(GROUPS, bufs):
        n_back = win // dil
        lb = buf.shape[1]
        ext_k = jnp.concatenate([buf[:, :, 0], k[:, :, h0:h1]], axis=1)
        ext_v = jnp.concatenate([buf[:, :, 1], v[:, :, h0:h1]], axis=1)
        idx = lb + jnp.arange(dec_s)[:, None] - dil * jnp.arange(n_back + 1)[None, :]
        valid = idx >= 0
        idx = jnp.maximum(idx, 0)
        kg = ext_k[:, idx]
        vg = ext_v[:, idx]
        sc = jnp.einsum("bjhe,bjkhe->bhjk", q[:, :, h0:h1], kg) * (HEAD_DIM ** -0.5)
        sc = jnp.where(valid, sc, NEG_INF)
        m = jnp.max(sc, axis=-1, keepdims=True)
        p = jnp.exp(sc - m)
        l = jnp.sum(p, axis=-1, keepdims=True)
        outs.append(jnp.einsum("bhjk,bjkhe->bjhe", p / l, vg))
        lse = (m + jnp.log(l))[..., 0].transpose(0, 2, 1)
        glses.append(jax.nn.logsumexp(lse, axis=-1) - math.log(h1 - h0))
        new_bufs.append(jnp.stack([ext_k[:, dec_s:], ext_v[:, dec_s:]], axis=2))
    w = jax.nn.softmax(jnp.stack(glses, axis=-1), axis=-1) * N_GROUPS
    o = jnp.concatenate([o_ * w[..., g, None, None] for g, o_ in enumerate(outs)], axis=2)
    return o.reshape(dec_b * dec_s, N_HEADS * HEAD_DIM), new_bufs


def _row2(a):
    return a.reshape(1, -1)


def _attn_layer(x_all, bufs, w_qkv, w_o, g1, b1, cos_p, sin_p, dims):
    batch, seq, dec_b, dec_s = dims
    t_prompt = batch * seq
    t_sample = dec_b * dec_s
    w_perm = _permute_qkv_weight(w_qkv)
    wo_perm = _permute_out_weight(w_o)
    *qkv_groups, kv_tok = _qkv_prompt(x_all, w_perm, cos_p, sin_p, batch, seq)
    os_, ls_ = [], []
    for g in range(N_GROUPS):
        o_g, l_g = _attn_group(qkv_groups[g], g, batch, seq)
        os_.append(o_g)
        ls_.append(l_g)
    x_next = _attn_out_prompt(os_, ls_, wo_perm, x_all, g1, b1, batch, seq)
    kvt = kv_tok.reshape(batch, seq, 2 * N_GROUPS, SEC)
    rows_p = []
    for g, (h0, h1, win, _) in enumerate(GROUPS):
        keep = min(win, seq)
        hw = (h1 - h0) * HEAD_DIM
        k_rows = kvt[:, seq - keep:, g, :hw].reshape(batch, keep, h1 - h0, HEAD_DIM)
        v_rows = kvt[:, seq - keep:, N_GROUPS + g, :hw].reshape(batch, keep, h1 - h0, HEAD_DIM)
        rows_p.append(jnp.stack([k_rows, v_rows], axis=2))
    y_s = _mm_rows(x_all, w_qkv.astype(BF16), t_prompt, t_sample)
    o_s, rows_s = _sample_attention(y_s, bufs, dec_b, dec_s)
    x_all = _proj_ln_rows(o_s, w_o.astype(BF16), x_all, g1, b1, x_next, t_prompt)
    return x_all, rows_p, rows_s


def _sgu_layer(x_all, w_in, b_in, ln_g, ln_b, ws, bs, w_out, g1, b1, dims):
    batch, seq, dec_b, dec_s = dims
    t_prompt = batch * seq
    t_sample = dec_b * dec_s
    common = (w_in.astype(BF16), _row2(b_in), _row2(ln_g), _row2(ln_b))
    tail = (w_out.astype(BF16), g1, b1)
    bs_p = jnp.broadcast_to(bs[:, :, None], (SGU_GROUPS, SGU_CHUNK, SGU_CHUNK))
    x_next = _sgu(x_all, None, common + (ws, bs_p) + tail, 0, t_prompt, 256, SGU_CHUNK, False)[0]
    c = min(SGU_CHUNK, dec_s)
    reps = SGU_CHUNK // c
    ws_s = jnp.tile(ws[:, :c, :c], (1, reps, reps))
    bs_s = jnp.broadcast_to(jnp.tile(bs[:, :c], (1, reps))[:, :, None], (SGU_GROUPS, SGU_CHUNK, SGU_CHUNK))
    x_all, v_new = _sgu(x_all, x_next, common + (ws_s, bs_s) + tail, t_prompt, t_sample, t_sample, c, True)
    return x_all, v_new.reshape(dec_b, dec_s, D_MODEL)


def _moe_layer(x_all, w_router, b_router, w_gu, b_gu, w_dn, b_dn, ln_g, ln_b):
    w_r = jnp.pad(w_router, ((0, 0), (0, LANES - N_EXPERTS)))
    b_r = jnp.pad(b_router, (0, LANES - N_EXPERTS)).reshape(1, LANES)
    return _moe(
        x_all, w_r, b_r,
        w_gu.astype(BF16), b_gu.reshape(N_EXPERTS, 1, -1),
        w_dn.astype(BF16), b_dn.reshape(N_EXPERTS, 1, -1),
        ln_g, ln_b)


def kernel(x_prompt, x_sample, cache_kv_w128, cache_kv_w512, cache_kv_w2048, attn_w_qkv, attn_w_o, sgu_w_in, sgu_b_in, sgu_ln_g, sgu_ln_b, sgu_w_s, sgu_b_s, sgu_w_out, moe_w_router, moe_b_router, moe_w_gu, moe_b_gu, moe_w_down, moe_b_down, ln1_g, ln1_b, ln2_g, ln2_b):
    batch, seq, _ = x_prompt.shape
    dec_b, dec_s, _ = x_sample.shape
    t_prompt = batch * seq
    t_sample = dec_b * dec_s
    caches = (cache_kv_w128, cache_kv_w512, cache_kv_w2048)
    x_all = jnp.concatenate([x_prompt.reshape(t_prompt, D_MODEL), x_sample.reshape(t_sample, D_MODEL)], axis=0)
    cos_p, sin_p = _rope_tables(jnp.arange(seq, dtype=jnp.int32))
    row2 = _row2
    dims = (batch, seq, dec_b, dec_s)

    kv_prompt = [[] for _ in GROUPS]
    kv_sample = [[] for _ in GROUPS]
    v_rows = []
    for i in range(DEPTH):
        j = i // 2
        g1, b1 = row2(ln1_g[i]), row2(ln1_b[i])
        if i % 2 == 0:
            x_all, rows_p, rows_s = _attn_layer(
                x_all, [c[j] for c in caches], attn_w_qkv[j], attn_w_o[j], g1, b1, cos_p, sin_p, dims)
            for g in range(N_GROUPS):
                kv_prompt[g].append(rows_p[g])
                kv_sample[g].append(rows_s[g])
        else:
            x_all, v_new = _sgu_layer(
                x_all, sgu_w_in[j], sgu_b_in[j], sgu_ln_g[j], sgu_ln_b[j], sgu_w_s[j], sgu_b_s[j], sgu_w_out[j],
                g1, b1, dims)
            v_rows.append(v_new)
        x_all = _moe_layer(
            x_all, moe_w_router[i], moe_b_router[i], moe_w_gu[i], moe_b_gu[i], moe_w_down[i], moe_b_down[i],
            row2(ln2_g[i]), row2(ln2_b[i]))
    y_prompt = x_all[:t_prompt].reshape(batch, seq, D_MODEL)
    y_sample = x_all[t_prompt:].reshape(dec_b, dec_s, D_MODEL)
    return (
        y_prompt, y_sample,
        jnp.stack(kv_prompt[0]), jnp.stack(kv_prompt[1]), jnp.stack(kv_prompt[2]),
        jnp.stack(kv_sample[0]), jnp.stack(kv_sample[1]), jnp.stack(kv_sample[2]),
        jnp.stack(v_rows),
    )
```

```python
import functools
import math

import jax
import jax.numpy as jnp
from jax import lax
from jax.experimental import pallas as pl
from jax.experimental.pallas import tpu as pltpu

F32 = jnp.float32
BF16 = jnp.bfloat16

D_MODEL = 1024
HEAD_DIM = 64
N_HEADS = D_MODEL // HEAD_DIM
GROUPS = ((0, 6, 128, 1), (6, 11, 512, 4), (11, 16, 2048, 16))
N_GROUPS = len(GROUPS)
Q_BLOCK = 128
SEC = 384
ROPE_THETA = 10000.0
NEG_INF = -1e30
PAST_LEN = 8192
SGU_GROUPS = 8
SGU_CHUNK = 128
N_EXPERTS = 32
TOP_K = 4
SWIGLU_LIMIT = 7.0
SWIGLU_ALPHA = 1.702
MOE_BLOCK = 256
DEPTH = 4
DEEPNORM_ALPHA = (2 * DEPTH) ** 0.25
LN_EPS = 1e-5
LANES = 128
MIB = 1024 * 1024


def _params(vmem_mib, *semantics):
    return pltpu.CompilerParams(dimension_semantics=semantics, vmem_limit_bytes=vmem_mib * MIB)


def _ln(x, g, b):
    mu = jnp.mean(x, axis=-1, keepdims=True)
    xc = x - mu
    var = jnp.mean(xc * xc, axis=-1, keepdims=True)
    return xc * lax.rsqrt(var + LN_EPS) * g + b


def _qkv_prompt_kernel(x_ref, w_ref, cos_ref, sin_ref, o1_ref, o2_ref, o3_ref, *rest, bm):
    win_refs = rest[:2 * N_GROUPS]
    y_s = rest[-1]
    y = jnp.dot(x_ref[...].astype(BF16), w_ref[...], preferred_element_type=F32)
    cos = cos_ref[...]
    sin = sin_ref[...]
    lane = lax.broadcasted_iota(jnp.int32, (bm, LANES), 1)
    first_half = (lane & (HEAD_DIM // 2)) == 0
    blocks_per_sec = SEC // LANES
    n_rot = 2 * N_GROUPS * blocks_per_sec
    for c in range(3 * N_GROUPS * blocks_per_sec):
        blk = y[:, c * LANES:(c + 1) * LANES]
        if c < n_rot:
            swapped = jnp.where(first_half, pltpu.roll(blk, LANES - HEAD_DIM // 2, 1), pltpu.roll(blk, HEAD_DIM // 2, 1))
            blk = blk * cos + swapped * sin
        if c < n_rot // 2:
            blk = blk * (HEAD_DIM ** -0.5)
        else:
            sec, p = divmod(c - n_rot // 2, blocks_per_sec)
            win_ref = win_refs[sec]
            keep = win_ref.shape[1]
            win_ref[0, :, p * LANES:(p + 1) * LANES] = blk[bm - keep:, :]
        y_s[c] = blk
    for g, o_ref in enumerate((o1_ref, o2_ref, o3_ref)):
        d = GROUPS[g][3]
        n = bm // d
        for r in range(d):
            for part in range(3):
                for p in range(blocks_per_sec):
                    c = (part * N_GROUPS + g) * blocks_per_sec + p
                    rows = y_s[c] if d == 1 else y_s[c, pl.ds(r, n, stride=d), :]
                    o_ref[0, r, :, part * SEC + p * LANES:part * SEC + (p + 1) * LANES] = rows.astype(BF16)


def _qkv_prompt(x_all, w_perm, cos, sin, batch, seq):
    bm = 256
    tiles = seq // bm
    out_shape = [jax.ShapeDtypeStruct((batch, d, seq // d, 3 * SEC), BF16) for (_, _, _, d) in GROUPS]
    out_specs = [pl.BlockSpec((1, d, bm // d, 3 * SEC), lambda b, i: (b, 0, i, 0)) for (_, _, _, d) in GROUPS]
    for _ in range(2):
        for (_, _, win, _) in GROUPS:
            keep = min(win, seq)
            rows = min(keep, bm)
            first = tiles - keep // rows
            out_shape.append(jax.ShapeDtypeStruct((batch, keep, SEC), F32))
            out_specs.append(pl.BlockSpec((1, rows, SEC), lambda b, i, first=first: (b, jnp.maximum(i - first, 0), 0)))
    n_cols = 3 * N_GROUPS * SEC
    return pl.pallas_call(
        functools.partial(_qkv_prompt_kernel, bm=bm),
        grid=(batch, tiles),
        in_specs=[
            pl.BlockSpec((bm, D_MODEL), lambda b, i: (b * tiles + i, 0)),
            pl.BlockSpec((D_MODEL, n_cols), lambda b, i: (0, 0)),
            pl.BlockSpec((bm, LANES), lambda b, i: (i, 0)),
            pl.BlockSpec((bm, LANES), lambda b, i: (i, 0)),
        ],
        out_specs=out_specs,
        out_shape=out_shape,
        scratch_shapes=[pltpu.VMEM((n_cols // LANES, bm, LANES), F32)],
        compiler_params=_params(48, "arbitrary", "arbitrary"),
        name="qkv_prompt",
    )(x_all, w_perm, cos, sin)


def _attn_group_kernel(qkv_ref, o_ref, lse_ref, *, d, m_len, n_heads):
    nb = m_len // Q_BLOCK
    kw = min(2 * Q_BLOCK, m_len)
    lane = lax.broadcasted_iota(jnp.int32, (1, LANES), 1)
    lo = lane < HEAD_DIM
    qi = lax.broadcasted_iota(jnp.int32, (Q_BLOCK, kw), 0)
    kj = lax.broadcasted_iota(jnp.int32, (Q_BLOCK, kw), 1)

    def block(i, carry):
        r = i // nb
        n = i % nb
        ks = pl.multiple_of(jnp.maximum(n - 1, 0) * Q_BLOCK, Q_BLOCK)
        qs = pl.multiple_of(n * Q_BLOCK, Q_BLOCK)
        delta = qs - ks + qi - kj
        valid = (delta >= 0) & (delta <= Q_BLOCK)
        if d == 1:
            dst = pl.ds(qs, Q_BLOCK)
        else:
            dst = pl.ds(r + qs * d, Q_BLOCK, stride=d)
        lses = []
        for p in range(SEC // LANES):
            cols = slice(p * LANES, (p + 1) * LANES)
            q2 = qkv_ref[0, r, pl.ds(qs, Q_BLOCK), cols]
            k2 = qkv_ref[0, r, pl.ds(ks, kw), SEC + p * LANES:SEC + (p + 1) * LANES]
            v2 = qkv_ref[0, r, pl.ds(ks, kw), 2 * SEC + p * LANES:2 * SEC + (p + 1) * LANES]
            o_pair = jnp.zeros((Q_BLOCK, LANES), F32)
            for half in range(2):
                if 2 * p + half >= n_heads:
                    continue
                msk = lo if half == 0 else jnp.logical_not(lo)
                qm = jnp.where(msk, q2, jnp.zeros_like(q2))
                vm = jnp.where(msk, v2, jnp.zeros_like(v2))
                s = lax.dot_general(qm, k2, (((1,), (1,)), ((), ())), preferred_element_type=F32)
                s = jnp.where(valid, s, NEG_INF)
                mx = jnp.max(s, axis=1, keepdims=True)
                pe = jnp.exp(s - mx)
                l = jnp.sum(pe, axis=1, keepdims=True)
                o_pair = o_pair + jnp.dot(pe.astype(BF16), vm, preferred_element_type=F32) / l
                lses.append(mx + jnp.log(l))
            o_ref[0, p, dst, :] = o_pair
        lmax = functools.reduce(jnp.maximum, lses)
        lsum = functools.reduce(lambda a, b: a + b, [jnp.exp(l - lmax) for l in lses])
        glse = lmax + jnp.log(lsum) - math.log(n_heads)
        lse_ref[0, dst, :] = jnp.broadcast_to(glse, (Q_BLOCK, LANES))
        return carry

    lax.fori_loop(0, d * nb, block, 0)


def _attn_group(qkv_g, g, batch, seq):
    h0, h1, _, d = GROUPS[g]
    m_len = seq // d
    return pl.pallas_call(
        functools.partial(_attn_group_kernel, d=d, m_len=m_len, n_heads=h1 - h0),
        grid=(batch,),
        in_specs=[pl.BlockSpec((1, d, m_len, 3 * SEC), lambda b: (b, 0, 0, 0))],
        out_specs=[
            pl.BlockSpec((1, SEC // LANES, seq, LANES), lambda b: (b, 0, 0, 0)),
            pl.BlockSpec((1, seq, LANES), lambda b: (b, 0, 0)),
        ],
        out_shape=[
            jax.ShapeDtypeStruct((batch, SEC // LANES, seq, LANES), F32),
            jax.ShapeDtypeStruct((batch, seq, LANES), F32),
        ],
        compiler_params=_params(40, "arbitrary"),
        name=f"attn_group{g}",
    )(qkv_g)


def _attn_out_kernel(o1_ref, o2_ref, o3_ref, l1_ref, l2_ref, l3_ref, wo_ref, x_ref, g_ref, b_ref, out_ref):
    ls = [l1_ref[0], l2_ref[0], l3_ref[0]]
    mx = jnp.maximum(jnp.maximum(ls[0], ls[1]), ls[2])
    es = [jnp.exp(l - mx) for l in ls]
    inv = float(N_GROUPS) / (es[0] + es[1] + es[2])
    acc = None
    for g, o_ref in enumerate((o1_ref, o2_ref, o3_ref)):
        w = es[g] * inv
        o_g = jnp.concatenate([o_ref[0, p] * w for p in range(SEC // LANES)], axis=1)
        part = jnp.dot(o_g.astype(BF16), wo_ref[g], preferred_element_type=F32)
        acc = part if acc is None else acc + part
    out_ref[...] = _ln(DEEPNORM_ALPHA * x_ref[...] + acc, g_ref[...], b_ref[...])


def _attn_out_prompt(os_, ls_, wo_perm, x_all, ln_g, ln_b, batch, seq):
    bm = 512
    tiles = seq // bm
    t_all = x_all.shape[0]
    row = lambda b, i: (b * tiles + i, 0)
    const2 = lambda b, i: (0, 0)
    return pl.pallas_call(
        _attn_out_kernel,
        grid=(batch, tiles),
        in_specs=[pl.BlockSpec((1, SEC // LANES, bm, LANES), lambda b, i: (b, 0, i, 0))] * 3
        + [pl.BlockSpec((1, bm, LANES), lambda b, i: (b, i, 0))] * 3 + [
            pl.BlockSpec((N_GROUPS, SEC, D_MODEL), lambda b, i: (0, 0, 0)),
            pl.BlockSpec((bm, D_MODEL), row),
            pl.BlockSpec((1, D_MODEL), const2),
            pl.BlockSpec((1, D_MODEL), const2),
        ],
        out_specs=pl.BlockSpec((bm, D_MODEL), row),
        out_shape=jax.ShapeDtypeStruct((t_all, D_MODEL), F32),
        compiler_params=_params(40, "arbitrary", "arbitrary"),
        name="attn_out_prompt",
    )(*os_, *ls_, wo_perm, x_all, ln_g, ln_b)


def _mm_kernel(x_ref, w_ref, o_ref):
    o_ref[...] = jnp.dot(x_ref[...].astype(BF16), w_ref[...], preferred_element_type=F32)


def _mm_rows(x_all, w, row0, rows):
    n = w.shape[1]
    blk0 = row0 // rows
    return pl.pallas_call(
        _mm_kernel,
        grid=(1,),
        in_specs=[
            pl.BlockSpec((rows, D_MODEL), lambda i: (blk0, 0)),
            pl.BlockSpec((D_MODEL, n), lambda i: (0, 0)),
        ],
        out_specs=pl.BlockSpec((rows, n), lambda i: (0, 0)),
        out_shape=jax.ShapeDtypeStruct((rows, n), F32),
        compiler_params=_params(40, "arbitrary"),
        name="mm_rows",
    )(x_all, w)


def _proj_ln_kernel(a_ref, w_ref, x_ref, g_ref, b_ref, prev_ref, out_ref):
    del prev_ref
    acc = jnp.dot(a_ref[...].astype(BF16), w_ref[...], preferred_element_type=F32)
    out_ref[...] = _ln(DEEPNORM_ALPHA * x_ref[...] + acc, g_ref[...], b_ref[...])


def _proj_ln_rows(a, w, x_all, ln_g, ln_b, x_next, row0):
    rows = a.shape[0]
    blk0 = row0 // rows
    const2 = lambda i: (0, 0)
    return pl.pallas_call(
        _proj_ln_kernel,
        grid=(1,),
        in_specs=[
            pl.BlockSpec((rows, a.shape[1]), const2),
            pl.BlockSpec(w.shape, const2),
            pl.BlockSpec((rows, D_MODEL), lambda i: (blk0, 0)),
            pl.BlockSpec((1, D_MODEL), const2),
            pl.BlockSpec((1, D_MODEL), const2),
            pl.BlockSpec(memory_space=pl.ANY),
        ],
        out_specs=pl.BlockSpec((rows, D_MODEL), lambda i: (blk0, 0)),
        out_shape=jax.ShapeDtypeStruct(x_next.shape, F32),
        input_output_aliases={5: 0},
        compiler_params=_params(40, "arbitrary"),
        name="proj_ln_rows",
    )(a, w, x_all, ln_g, ln_b, x_next)


def _sgu_kernel(*refs, bm, chunk, emit_v, aliased):
    (x_ref, win_ref, bin_ref, lng_ref, lnb_ref, ws_ref, bs_ref, wout_ref, g1_ref, b1_ref) = refs[:10]
    rest = refs[10 + (1 if aliased else 0):]
    out_ref = rest[0]
    v_ref = rest[1] if emit_v else None
    ug_s = rest[-1]
    width = D_MODEL
    x = x_ref[...]
    z = jnp.dot(x.astype(BF16), win_ref[...], preferred_element_type=F32) + bin_ref[...]
    z = 0.5 * z * (1.0 + lax.erf(z * (2.0 ** -0.5)))
    v = _ln(z[:, width:], lng_ref[...], lnb_ref[...])
    if emit_v:
        v_ref[...] = v
    ii = lax.broadcasted_iota(jnp.int32, (SGU_CHUNK, SGU_CHUNK), 0)
    jj = lax.broadcasted_iota(jnp.int32, (SGU_CHUNK, SGU_CHUNK), 1)
    causal = (jj <= ii) & ((ii // chunk) == (jj // chunk))
    gw = width // SGU_GROUPS
    for gi in range(SGU_GROUPS):
        wsm = jnp.where(causal, ws_ref[gi], 0.0).astype(BF16)
        for c in range(bm // SGU_CHUNK):
            rows = slice(c * SGU_CHUNK, (c + 1) * SGU_CHUNK)
            cols = slice(gi * gw, (gi + 1) * gw)
            gate = jnp.dot(wsm, v[rows, cols].astype(BF16), preferred_element_type=F32) + bs_ref[gi]
            ug_s[rows, cols] = (z[rows, cols] * gate).astype(BF16)
    y = jnp.dot(ug_s[...], wout_ref[...], preferred_element_type=F32)
    out_ref[...] = _ln(DEEPNORM_ALPHA * x + y, g1_ref[...], b1_ref[...])


def _sgu(x_all, x_next, weights, row0, rows, bm, chunk, emit_v):
    w_in, b_in, ln_g, ln_b, ws, bs, w_out, g1, b1 = weights
    t_all = x_all.shape[0]
    blk0 = row0 // bm
    row = lambda i: (blk0 + i, 0)
    const2 = lambda i: (0, 0)
    const3 = lambda i: (0, 0, 0)
    aliased = x_next is not None
    in_specs = [
        pl.BlockSpec((bm, D_MODEL), row),
        pl.BlockSpec(w_in.shape, const2),
        pl.BlockSpec((1, 2 * D_MODEL), const2),
        pl.BlockSpec((1, D_MODEL), const2),
        pl.BlockSpec((1, D_MODEL), const2),
        pl.BlockSpec(ws.shape, const3),
        pl.BlockSpec(bs.shape, const3),
        pl.BlockSpec(w_out.shape, const2),
        pl.BlockSpec((1, D_MODEL), const2),
        pl.BlockSpec((1, D_MODEL), const2),
    ]
    args = [x_all, w_in, b_in, ln_g, ln_b, ws, bs, w_out, g1, b1]
    aliases = {}
    if aliased:
        in_specs.append(pl.BlockSpec(memory_space=pl.ANY))
        args.append(x_next)
        aliases = {10: 0}
    out_specs = [pl.BlockSpec((bm, D_MODEL), row)]
    out_shape = [jax.ShapeDtypeStruct((t_all, D_MODEL), F32)]
    if emit_v:
        out_specs.append(pl.BlockSpec((bm, D_MODEL), lambda i: (i, 0)))
        out_shape.append(jax.ShapeDtypeStruct((rows, D_MODEL), F32))
    return pl.pallas_call(
        functools.partial(_sgu_kernel, bm=bm, chunk=chunk, emit_v=emit_v, aliased=aliased),
        grid=(rows // bm,),
        in_specs=in_specs,
        out_specs=out_specs,
        out_shape=out_shape,
        scratch_shapes=[pltpu.VMEM((bm, D_MODEL), BF16)],
        input_output_aliases=aliases,
        compiler_params=_params(48, "arbitrary"),
        name="sgu",
    )(*args)


def _router_kernel(x_ref, wr_ref, br_ref, idx_ref, gate_ref, *, bm):
    logits = jnp.dot(x_ref[...], wr_ref[...], preferred_element_type=F32, precision=lax.Precision.HIGHEST)
    logits = logits + br_ref[...]
    lane = lax.broadcasted_iota(jnp.int32, (bm, LANES), 1)
    lane_f = lane.astype(F32)
    logits = jnp.where(lane < N_EXPERTS, logits, -jnp.inf)
    vals, idxs = [], []
    for _ in range(TOP_K):
        m = jnp.max(logits, axis=1, keepdims=True)
        i = jnp.min(jnp.where(logits == m, lane_f, float(LANES)), axis=1, keepdims=True)
        vals.append(m)
        idxs.append(i)
        logits = jnp.where(lane_f == i, -jnp.inf, logits)
    es = [jnp.exp(v - vals[0]) for v in vals]
    inv = 1.0 / functools.reduce(lambda a, b: a + b, es)
    idx_out = jnp.zeros((bm, LANES), F32)
    gate_out = jnp.zeros((bm, LANES), F32)
    for k in range(TOP_K):
        idx_out = jnp.where(lane == k, idxs[k], idx_out)
        gate_out = jnp.where(lane == k, es[k] * inv, gate_out)
    idx_ref[...] = idx_out.astype(jnp.int32)
    gate_ref[...] = gate_out


def _router(x_all, w_r, b_r):
    bm = 512
    t_all = x_all.shape[0]
    row = lambda i: (i, 0)
    const2 = lambda i: (0, 0)
    return pl.pallas_call(
        functools.partial(_router_kernel, bm=bm),
        grid=(pl.cdiv(t_all, bm),),
        in_specs=[
            pl.BlockSpec((bm, D_MODEL), row),
            pl.BlockSpec((D_MODEL, LANES), const2),
            pl.BlockSpec((1, LANES), const2),
        ],
        out_specs=[pl.BlockSpec((bm, LANES), row), pl.BlockSpec((bm, LANES), row)],
        out_shape=[
            jax.ShapeDtypeStruct((t_all, LANES), jnp.int32),
            jax.ShapeDtypeStruct((t_all, LANES), F32),
        ],
        compiler_params=_params(32, "arbitrary"),
        name="router",
    )(x_all, w_r, b_r)


def _expert_kernel(be_ref, nreal_ref, cur_ref, nxt_ref, x_hbm, wgu_ref, bgu_ref, wdn_ref, bdn_ref, y_hbm,
                   xbuf, ybuf, wgu_bf, wdn_bf, sem, *, t_all, n_blocks):
    i = pl.program_id(0)
    n_real = nreal_ref[0]
    slot = i % 2
    ff = wdn_ref.shape[1]
    gather_sem = lambda s: sem.at[s]
    scatter_sem = sem.at[2]

    def start_gather(idx_ref, s):
        for r in range(MOE_BLOCK):
            tok = jnp.minimum(idx_ref[0, 0, r] >> 2, t_all - 1)
            pltpu.make_async_copy(x_hbm.at[pl.ds(tok, 1), :], xbuf.at[s, pl.ds(r, 1), :], gather_sem(s)).start()

    def wait_scatter():
        pltpu.make_async_copy(ybuf, y_hbm.at[0, pl.ds(0, MOE_BLOCK), :], scatter_sem).wait()

    @pl.when(i == 0)
    def _():
        start_gather(cur_ref, 0)

    @pl.when(i + 1 < n_real)
    def _():
        start_gather(nxt_ref, 1 - slot)

    @pl.when(i < n_real)
    def _():
        pltpu.make_async_copy(x_hbm.at[pl.ds(0, MOE_BLOCK), :], xbuf.at[slot], gather_sem(slot)).wait()

        @pl.when((i == 0) | (be_ref[i] != be_ref[jnp.maximum(i - 1, 0)]))
        def _():
            wgu_bf[...] = wgu_ref[0].astype(BF16)
            wdn_bf[...] = wdn_ref[0].astype(BF16)

        h = jnp.dot(xbuf[slot].astype(BF16), wgu_bf[...], preferred_element_type=F32) + bgu_ref[0]
        gate = jnp.minimum(h[:, :ff], SWIGLU_LIMIT)
        up = jnp.clip(h[:, ff:], -SWIGLU_LIMIT, SWIGLU_LIMIT)
        act = (up + 1.0) * gate * jax.nn.sigmoid(SWIGLU_ALPHA * gate)
        y = jnp.dot(act.astype(BF16), wdn_bf[...], preferred_element_type=F32) + bdn_ref[0]

        @pl.when(i > 0)
        def _():
            wait_scatter()

        ybuf[...] = y
        for r in range(MOE_BLOCK):
            pair = cur_ref[0, 0, r]
            pltpu.make_async_copy(ybuf.at[pl.ds(r, 1), :], y_hbm.at[pair & 3, pl.ds(pair >> 2, 1), :], scatter_sem).start()

    @pl.when(i == n_blocks - 1)
    def _():
        wait_scatter()


def _experts(block_expert, n_real, slot_pair, x_all, w_gu, b_gu, w_dn, b_dn):
    t_all = x_all.shape[0]
    n_blocks = slot_pair.shape[0]
    ff = w_dn.shape[1]
    by_expert = lambda i, be, nr: (be[i], 0, 0)
    grid_spec = pltpu.PrefetchScalarGridSpec(
        num_scalar_prefetch=2,
        grid=(n_blocks,),
        in_specs=[
            pl.BlockSpec((1, 1, MOE_BLOCK), lambda i, be, nr: (i, 0, 0), memory_space=pltpu.SMEM),
            pl.BlockSpec((1, 1, MOE_BLOCK), lambda i, be, nr: (jnp.minimum(i + 1, n_blocks - 1), 0, 0),
                         memory_space=pltpu.SMEM),
            pl.BlockSpec(memory_space=pl.ANY),
            pl.BlockSpec((1, D_MODEL, 2 * ff), by_expert),
            pl.BlockSpec((1, 1, 2 * ff), by_expert),
            pl.BlockSpec((1, ff, D_MODEL), by_expert),
            pl.BlockSpec((1, 1, D_MODEL), by_expert),
        ],
        out_specs=pl.BlockSpec(memory_space=pl.ANY),
        scratch_shapes=[
            pltpu.VMEM((2, MOE_BLOCK, D_MODEL), F32),
            pltpu.VMEM((MOE_BLOCK, D_MODEL), F32),
            pltpu.VMEM((D_MODEL, 2 * ff), BF16),
            pltpu.VMEM((ff, D_MODEL), BF16),
            pltpu.SemaphoreType.DMA((3,)),
        ],
    )
    return pl.pallas_call(
        functools.partial(_expert_kernel, t_all=t_all, n_blocks=n_blocks),
        grid_spec=grid_spec,
        out_shape=jax.ShapeDtypeStruct((TOP_K, t_all + MOE_BLOCK // TOP_K, D_MODEL), F32),
        compiler_params=_params(52, "arbitrary"),
        name="experts",
    )(block_expert, n_real, slot_pair, slot_pair, x_all, w_gu, b_gu, w_dn, b_dn)


def _merge_ln_kernel(x_ref, y_ref, gate_ref, g_ref, b_ref, out_ref):
    gates = gate_ref[...]
    acc = DEEPNORM_ALPHA * x_ref[...]
    for k in range(TOP_K):
        acc = acc + y_ref[k] * gates[:, k:k + 1]
    out_ref[...] = _ln(acc, g_ref[...], b_ref[...])


def _merge_ln(x_all, y_pairs, gates, ln_g, ln_b):
    bm = 512
    t_all = x_all.shape[0]
    row = lambda i: (i, 0)
    const2 = lambda i: (0, 0)
    return pl.pallas_call(
        _merge_ln_kernel,
        grid=(pl.cdiv(t_all, bm),),
        in_specs=[
            pl.BlockSpec((bm, D_MODEL), row),
            pl.BlockSpec((TOP_K, bm, D_MODEL), lambda i: (0, i, 0)),
            pl.BlockSpec((bm, LANES), row),
            pl.BlockSpec((1, D_MODEL), const2),
            pl.BlockSpec((1, D_MODEL), const2),
        ],
        out_specs=pl.BlockSpec((bm, D_MODEL), row),
        out_shape=jax.ShapeDtypeStruct((t_all, D_MODEL), F32),
        compiler_params=_params(48, "arbitrary"),
        name="merge_ln",
    )(x_all, y_pairs, gates, ln_g, ln_b)


def _moe(x_all, w_r, b_r, w_gu, b_gu, w_dn, b_dn, ln_g, ln_b):
    t_all = x_all.shape[0]
    idx, gates = _router(x_all, w_r, b_r)
    n_rows = t_all * TOP_K
    e_flat = idx[:, :TOP_K].reshape(n_rows)
    order = jnp.argsort(e_flat).astype(jnp.int32)
    experts = jnp.arange(N_EXPERTS, dtype=jnp.int32)
    counts = jnp.sum((e_flat[:, None] == experts[None, :]).astype(jnp.int32), axis=0)
    padded = (counts + MOE_BLOCK - 1) // MOE_BLOCK * MOE_BLOCK
    pad_end = jnp.cumsum(padded)
    pad_start = pad_end - padded
    start = jnp.cumsum(counts) - counts
    n_blocks = -(-n_rows // MOE_BLOCK) + N_EXPERTS
    block_start = jnp.arange(n_blocks, dtype=jnp.int32) * MOE_BLOCK
    block_expert = jnp.minimum(
        jnp.sum((pad_end[None, :] <= block_start[:, None]).astype(jnp.int32), axis=1), N_EXPERTS - 1)
    n_real = (pad_end[-1:] // MOE_BLOCK).astype(jnp.int32)
    within = jnp.arange(MOE_BLOCK, dtype=jnp.int32)[None, :]
    pos = block_start[:, None] + within - pad_start[block_expert][:, None]
    valid = pos < counts[block_expert][:, None]
    src = jnp.clip(start[block_expert][:, None] + pos, 0, n_rows - 1)
    slot_pair = jnp.where(valid, order[src], n_rows + within).reshape(n_blocks, 1, MOE_BLOCK)
    y_pairs = _experts(block_expert, n_real, slot_pair, x_all, w_gu, b_gu, w_dn, b_dn)
    return _merge_ln(x_all, y_pairs, gates, ln_g, ln_b)


def _rope_tables(pos):
    half = HEAD_DIM // 2
    inv_freq = ROPE_THETA ** (-jnp.arange(half, dtype=F32) / half)
    ang = pos.astype(F32)[:, None] * inv_freq[None, :]
    cos = jnp.cos(ang)
    sin = jnp.sin(ang)
    cos = jnp.concatenate([cos, cos], axis=1)
    sin = jnp.concatenate([-sin, sin], axis=1)
    reps = LANES // HEAD_DIM
    return jnp.tile(cos, (1, reps)), jnp.tile(sin, (1, reps))


def _permute_qkv_weight(w_qkv):
    w3 = w_qkv.reshape(D_MODEL, 3, N_HEADS * HEAD_DIM)
    secs = []
    for part in range(3):
        for (h0, h1, _, _) in GROUPS:
            sec = w3[:, part, h0 * HEAD_DIM:h1 * HEAD_DIM]
            secs.append(jnp.pad(sec, ((0, 0), (0, SEC - sec.shape[1]))))
    return jnp.concatenate(secs, axis=1).astype(BF16)


def _permute_out_weight(w_o):
    secs = []
    for (h0, h1, _, _) in GROUPS:
        sec = w_o[h0 * HEAD_DIM:h1 * HEAD_DIM]
        secs.append(jnp.pad(sec, ((0, SEC - sec.shape[0]), (0, 0))))
    return jnp.stack(secs).astype(BF16)


SUB = 8


def _sample_attn_kernel(y_ref, cos_ref, sin_ref, k1_ref, v1_ref, k2_ref, v2_ref, k3_ref, v3_ref, o_ref, kv_ref, *, dec_s):
    y = y_ref[0]
    cos = cos_ref[...]
    sin = sin_ref[...]
    lane = lax.broadcasted_iota(jnp.int32, (SUB, LANES), 1)
    first_half = (lane & (HEAD_DIM // 2)) == 0
    blocks_per_sec = SEC // LANES
    n_rot = 2 * N_GROUPS * blocks_per_sec
    blocks = []
    for c in range(3 * N_GROUPS * blocks_per_sec):
        blk = y[:, c * LANES:(c + 1) * LANES]
        if c < n_rot:
            swapped = jnp.where(first_half, pltpu.roll(blk, LANES - HEAD_DIM // 2, 1), pltpu.roll(blk, HEAD_DIM // 2, 1))
            blk = blk * cos + swapped * sin
        if c < n_rot // 2:
            blk = blk * (HEAD_DIM ** -0.5)
        blocks.append(blk)
    section = lambda s: jnp.concatenate(blocks[s * blocks_per_sec:(s + 1) * blocks_per_sec], axis=1)
    kv_ref[0] = jnp.concatenate(blocks[n_rot // 2:], axis=1)

    seg = (lax.broadcasted_iota(jnp.int32, (SEC, LANES), 0) // HEAD_DIM
           == lax.broadcasted_iota(jnp.int32, (SEC, LANES), 1)).astype(BF16)
    seg_t = (lax.broadcasted_iota(jnp.int32, (LANES, SEC), 1) // HEAD_DIM
             == lax.broadcasted_iota(jnp.int32, (LANES, SEC), 0)).astype(BF16)
    n_keys = Q_BLOCK + SUB
    key_row = lax.broadcasted_iota(jnp.int32, (n_keys, LANES), 0)
    head_lane = lax.broadcasted_iota(jnp.int32, (1, LANES), 1)
    out_row = lax.broadcasted_iota(jnp.int32, (SUB, 1), 0)

    outs, glses = [], []
    for g, (k_ref, v_ref) in enumerate(((k1_ref, v1_ref), (k2_ref, v2_ref), (k3_ref, v3_ref))):
        h0, h1, _, d = GROUPS[g]
        q_g, k_new, v_new = section(g), section(N_GROUPS + g), section(2 * N_GROUPS + g)
        o_g = jnp.zeros((SUB, SEC), F32)
        l_g = jnp.zeros((SUB, 1), F32)
        for j in range(dec_s):
            k_buf = k_ref[0] if d == 1 else k_ref[0, j]
            v_buf = v_ref[0] if d == 1 else v_ref[0, j]
            keys = jnp.concatenate([k_buf, k_new], axis=0)
            vals = jnp.concatenate([v_buf, v_new], axis=0)
            new_j = key_row - Q_BLOCK
            if d == 1:
                valid = ((key_row < Q_BLOCK) & (key_row >= j)) | ((key_row >= Q_BLOCK) & (new_j <= j))
            else:
                valid = (key_row < Q_BLOCK) | (new_j == j)
            prod = (keys * q_g[j:j + 1, :]).astype(BF16)
            s = jnp.dot(prod, seg, preferred_element_type=F32)
            s = jnp.where(valid, s, NEG_INF)
            mx = jnp.max(s, axis=0, keepdims=True)
            pe = jnp.exp(s - mx)
            l = jnp.sum(pe, axis=0, keepdims=True)
            spread = jnp.dot((pe / l).astype(BF16), seg_t, preferred_element_type=F32)
            o_j = jnp.sum(spread * vals, axis=0, keepdims=True)
            lse = jnp.where(head_lane < h1 - h0, mx + jnp.log(l), -jnp.inf)
            lmax = jnp.max(lse, axis=1, keepdims=True)
            glse = lmax + jnp.log(jnp.sum(jnp.exp(lse - lmax), axis=1, keepdims=True)) - math.log(h1 - h0)
            o_g = jnp.where(out_row == j, o_j, o_g)
            l_g = jnp.where(out_row == j, glse, l_g)
        outs.append(o_g)
        glses.append(l_g)
    gmax = jnp.maximum(jnp.maximum(glses[0], glses[1]), glses[2])
    es = [jnp.exp(l - gmax) for l in glses]
    inv = float(N_GROUPS) / (es[0] + es[1] + es[2])
    o_ref[0] = jnp.concatenate([outs[g] * (es[g] * inv) for g in range(N_GROUPS)], axis=1)


def _dense_window_rows(buf, part, d, dec_s):
    dec_b, lb, _, hg, _ = buf.shape
    rows = buf[:, :, part].reshape(dec_b, lb, hg * HEAD_DIM)
    rows = jnp.pad(rows, ((0, 0), (0, 0), (0, SEC - hg * HEAD_DIM)))
    if d == 1:
        return rows
    return rows.reshape(dec_b, lb // d, d, SEC)[:, :, :dec_s].transpose(0, 2, 1, 3)


def _sample_attention(y_s, bufs, cos_s, sin_s, dec_b, dec_s):
    n_cols = 3 * N_GROUPS * SEC
    y_pad = jnp.pad(y_s.reshape(dec_b, dec_s, n_cols), ((0, 0), (0, SUB - dec_s), (0, 0)))
    dense, specs = [], []
    for (h0, h1, win, d), buf in zip(GROUPS, bufs):
        assert buf.shape[1] == win and win // d == Q_BLOCK and (d == 1 or dec_s <= d)
        for part in range(2):
            dense.append(_dense_window_rows(buf, part, d, dec_s))
            if d == 1:
                specs.append(pl.BlockSpec((1, Q_BLOCK, SEC), lambda b: (b, 0, 0)))
            else:
                specs.append(pl.BlockSpec((1, dec_s, Q_BLOCK, SEC), lambda b: (b, 0, 0, 0)))
    o, kv_new = pl.pallas_call(
        functools.partial(_sample_attn_kernel, dec_s=dec_s),
        grid=(dec_b,),
        in_specs=[
            pl.BlockSpec((1, SUB, n_cols), lambda b: (b, 0, 0)),
            pl.BlockSpec((SUB, LANES), lambda b: (0, 0)),
            pl.BlockSpec((SUB, LANES), lambda b: (0, 0)),
        ] + specs,
        out_specs=[
            pl.BlockSpec((1, SUB, N_GROUPS * SEC), lambda b: (b, 0, 0)),
            pl.BlockSpec((1, SUB, 2 * N_GROUPS * SEC), lambda b: (b, 0, 0)),
        ],
        out_shape=[
            jax.ShapeDtypeStruct((dec_b, SUB, N_GROUPS * SEC), F32),
            jax.ShapeDtypeStruct((dec_b, SUB, 2 * N_GROUPS * SEC), F32),
        ],
        compiler_params=_params(32, "arbitrary"),
        name="sample_attn",
    )(y_pad, cos_s, sin_s, *dense)
    new_bufs = []
    for g, ((h0, h1, _, _), buf) in enumerate(zip(GROUPS, bufs)):
        hg = h1 - h0
        new_rows = [
            kv_new[:, :dec_s, (part * N_GROUPS + g) * SEC:(part * N_GROUPS + g) * SEC + hg * HEAD_DIM]
            .reshape(dec_b, dec_s, hg, HEAD_DIM) for part in range(2)]
        new_bufs.append(jnp.concatenate([buf[:, dec_s:], jnp.stack(new_rows, axis=2)], axis=1))
    return o[:, :dec_s].reshape(dec_b * dec_s, N_GROUPS * SEC), new_bufs


def _row2(a):
    return a.reshape(1, -1)


def _attn_layer(x_all, bufs, w_qkv, w_o, g1, b1, cos_p, sin_p, dims):
    batch, seq, dec_b, dec_s = dims
    t_prompt = batch * seq
    t_sample = dec_b * dec_s
    w_perm = _permute_qkv_weight(w_qkv)
    wo_perm = _permute_out_weight(w_o)
    outs = _qkv_prompt(x_all, w_perm, cos_p, sin_p, batch, seq)
    qkv_groups, windows = outs[:N_GROUPS], outs[N_GROUPS:]
    os_, ls_ = [], []
    for g in range(N_GROUPS):
        o_g, l_g = _attn_group(qkv_groups[g], g, batch, seq)
        os_.append(o_g)
        ls_.append(l_g)
    x_next = _attn_out_prompt(os_, ls_, wo_perm, x_all, g1, b1, batch, seq)
    rows_p = []
    for g, (h0, h1, _, _) in enumerate(GROUPS):
        hg = h1 - h0
        halves = [windows[part * N_GROUPS + g][:, :, :hg * HEAD_DIM].reshape(batch, -1, hg, HEAD_DIM) for part in range(2)]
        rows_p.append(jnp.stack(halves, axis=2))
    y_s = _mm_rows(x_all, w_perm, t_prompt, t_sample)
    cos_s, sin_s = _rope_tables(PAST_LEN + jnp.arange(SUB, dtype=jnp.int32))
    o_s, rows_s = _sample_attention(y_s, bufs, cos_s, sin_s, dec_b, dec_s)
    x_all = _proj_ln_rows(o_s, wo_perm.reshape(N_GROUPS * SEC, D_MODEL), x_all, g1, b1, x_next, t_prompt)
    return x_all, rows_p, rows_s


def _sgu_layer(x_all, w_in, b_in, ln_g, ln_b, ws, bs, w_out, g1, b1, dims):
    batch, seq, dec_b, dec_s = dims
    t_prompt = batch * seq
    t_sample = dec_b * dec_s
    common = (w_in.astype(BF16), _row2(b_in), _row2(ln_g), _row2(ln_b))
    tail = (w_out.astype(BF16), g1, b1)
    bs_p = jnp.broadcast_to(bs[:, :, None], (SGU_GROUPS, SGU_CHUNK, SGU_CHUNK))
    x_next = _sgu(x_all, None, common + (ws, bs_p) + tail, 0, t_prompt, 256, SGU_CHUNK, False)[0]
    c = min(SGU_CHUNK, dec_s)
    reps = SGU_CHUNK // c
    ws_s = jnp.tile(ws[:, :c, :c], (1, reps, reps))
    bs_s = jnp.broadcast_to(jnp.tile(bs[:, :c], (1, reps))[:, :, None], (SGU_GROUPS, SGU_CHUNK, SGU_CHUNK))
    x_all, v_new = _sgu(x_all, x_next, common + (ws_s, bs_s) + tail, t_prompt, t_sample, t_sample, c, True)
    return x_all, v_new.reshape(dec_b, dec_s, D_MODEL)


def _moe_layer(x_all, w_router, b_router, w_gu, b_gu, w_dn, b_dn, ln_g, ln_b):
    w_r = jnp.pad(w_router, ((0, 0), (0, LANES - N_EXPERTS)))
    b_r = jnp.pad(b_router, (0, LANES - N_EXPERTS)).reshape(1, LANES)
    return _moe(
        x_all, w_r, b_r,
        w_gu, b_gu.reshape(N_EXPERTS, 1, -1),
        w_dn, b_dn.reshape(N_EXPERTS, 1, -1),
        ln_g, ln_b)


def kernel(x_prompt, x_sample, cache_kv_w128, cache_kv_w512, cache_kv_w2048, attn_w_qkv, attn_w_o, sgu_w_in, sgu_b_in, sgu_ln_g, sgu_ln_b, sgu_w_s, sgu_b_s, sgu_w_out, moe_w_router, moe_b_router, moe_w_gu, moe_b_gu, moe_w_down, moe_b_down, ln1_g, ln1_b, ln2_g, ln2_b):
    batch, seq, _ = x_prompt.shape
    dec_b, dec_s, _ = x_sample.shape
    t_prompt = batch * seq
    t_sample = dec_b * dec_s
    caches = (cache_kv_w128, cache_kv_w512, cache_kv_w2048)
    x_all = jnp.concatenate([x_prompt.reshape(t_prompt, D_MODEL), x_sample.reshape(t_sample, D_MODEL)], axis=0)
    cos_p, sin_p = _rope_tables(jnp.arange(seq, dtype=jnp.int32))
    row2 = _row2
    dims = (batch, seq, dec_b, dec_s)

    kv_prompt = [[] for _ in GROUPS]
    kv_sample = [[] for _ in GROUPS]
    v_rows = []
    for i in range(DEPTH):
        j = i // 2
        g1, b1 = row2(ln1_g[i]), row2(ln1_b[i])
        if i % 2 == 0:
            x_all, rows_p, rows_s = _attn_layer(
                x_all, [c[j] for c in caches], attn_w_qkv[j], attn_w_o[j], g1, b1, cos_p, sin_p, dims)
            for g in range(N_GROUPS):
                kv_prompt[g].append(rows_p[g])
                kv_sample[g].append(rows_s[g])
        else:
            x_all, v_new = _sgu_layer(
                x_all, sgu_w_in[j], sgu_b_in[j], sgu_ln_g[j], sgu_ln_b[j], sgu_w_s[j], sgu_b_s[j], sgu_w_out[j],
                g1, b1, dims)
            v_rows.append(v_new)
        x_all = _moe_layer(
            x_all, moe_w_router[i], moe_b_router[i], moe_w_gu[i], moe_b_gu[i], moe_w_down[i], moe_b_down[i],
            row2(ln2_g[i]), row2(ln2_b[i]))
    y_prompt = x_all[:t_prompt].reshape(batch, seq, D_MODEL)
    y_sample = x_all[t_prompt:].reshape(dec_b, dec_s, D_MODEL)
    return (
        y_prompt, y_sample,
        jnp.stack(kv_prompt[0]), jnp.stack(kv_prompt[1]), jnp.stack(kv_prompt[2]),
        jnp.stack(kv_sample[0]), jnp.stack(kv_sample[1]), jnp.stack(kv_sample[2]),
        jnp.stack(v_rows),
    )
```

```python
import functools
import math

import jax
import jax.numpy as jnp
from jax import lax
from jax.experimental import pallas as pl
from jax.experimental.pallas import tpu as pltpu

F32 = jnp.float32
BF16 = jnp.bfloat16

D_MODEL = 1024
HEAD_DIM = 64
N_HEADS = D_MODEL // HEAD_DIM
GROUPS = ((0, 6, 128, 1), (6, 11, 512, 4), (11, 16, 2048, 16))
N_GROUPS = len(GROUPS)
Q_BLOCK = 128
SEC = 384
ROPE_THETA = 10000.0
NEG_INF = -1e30
PAST_LEN = 8192
SGU_GROUPS = 8
SGU_CHUNK = 128
N_EXPERTS = 32
TOP_K = 4
SWIGLU_LIMIT = 7.0
SWIGLU_ALPHA = 1.702
MOE_BLOCK = 256
MOE_TILE = 256
SUB = 8
DEPTH = 4
DEEPNORM_ALPHA = (2 * DEPTH) ** 0.25
LN_EPS = 1e-5
LANES = 128
MIB = 1024 * 1024


def _params(vmem_mib, *semantics):
    return pltpu.CompilerParams(dimension_semantics=semantics, vmem_limit_bytes=vmem_mib * MIB)


def _ln(x, g, b):
    mu = jnp.mean(x, axis=-1, keepdims=True)
    xc = x - mu
    var = jnp.mean(xc * xc, axis=-1, keepdims=True)
    return xc * lax.rsqrt(var + LN_EPS) * g + b


def _qkv_prompt_kernel(x_ref, w_ref, cos_ref, sin_ref, o1_ref, o2_ref, o3_ref, *rest, bm):
    win_refs = rest[:2 * N_GROUPS]
    y_s = rest[-1]
    y = jnp.dot(x_ref[...].astype(BF16), w_ref[...], preferred_element_type=F32)
    cos = cos_ref[...]
    sin = sin_ref[...]
    lane = lax.broadcasted_iota(jnp.int32, (bm, LANES), 1)
    first_half = (lane & (HEAD_DIM // 2)) == 0
    blocks_per_sec = SEC // LANES
    n_rot = 2 * N_GROUPS * blocks_per_sec
    for c in range(3 * N_GROUPS * blocks_per_sec):
        blk = y[:, c * LANES:(c + 1) * LANES]
        if c < n_rot:
            swapped = jnp.where(first_half, pltpu.roll(blk, LANES - HEAD_DIM // 2, 1), pltpu.roll(blk, HEAD_DIM // 2, 1))
            blk = blk * cos + swapped * sin
        if c < n_rot // 2:
            blk = blk * (HEAD_DIM ** -0.5)
        else:
            sec, p = divmod(c - n_rot // 2, blocks_per_sec)
            win_ref = win_refs[sec]
            keep = win_ref.shape[1]
            win_ref[0, :, p * LANES:(p + 1) * LANES] = blk[bm - keep:, :]
        y_s[c] = blk
    for g, o_ref in enumerate((o1_ref, o2_ref, o3_ref)):
        d = GROUPS[g][3]
        n = bm // d
        for r in range(d):
            for part in range(3):
                for p in range(blocks_per_sec):
                    c = (part * N_GROUPS + g) * blocks_per_sec + p
                    rows = y_s[c] if d == 1 else y_s[c, pl.ds(r, n, stride=d), :]
                    o_ref[0, r, :, part * SEC + p * LANES:part * SEC + (p + 1) * LANES] = rows.astype(BF16)


def _qkv_prompt(x_all, w_perm, cos, sin, batch, seq):
    bm = 256
    tiles = seq // bm
    out_shape = [jax.ShapeDtypeStruct((batch, d, seq // d, 3 * SEC), BF16) for (_, _, _, d) in GROUPS]
    out_specs = [pl.BlockSpec((1, d, bm // d, 3 * SEC), lambda b, i: (b, 0, i, 0)) for (_, _, _, d) in GROUPS]
    for _ in range(2):
        for (_, _, win, _) in GROUPS:
            keep = min(win, seq)
            rows = min(keep, bm)
            first = tiles - keep // rows
            out_shape.append(jax.ShapeDtypeStruct((batch, keep, SEC), F32))
            out_specs.append(pl.BlockSpec((1, rows, SEC), lambda b, i, first=first: (b, jnp.maximum(i - first, 0), 0)))
    n_cols = 3 * N_GROUPS * SEC
    return pl.pallas_call(
        functools.partial(_qkv_prompt_kernel, bm=bm),
        grid=(batch, tiles),
        in_specs=[
            pl.BlockSpec((bm, D_MODEL), lambda b, i: (b * tiles + i, 0)),
            pl.BlockSpec((D_MODEL, n_cols), lambda b, i: (0, 0)),
            pl.BlockSpec((bm, LANES), lambda b, i: (i, 0)),
            pl.BlockSpec((bm, LANES), lambda b, i: (i, 0)),
        ],
        out_specs=out_specs,
        out_shape=out_shape,
        scratch_shapes=[pltpu.VMEM((n_cols // LANES, bm, LANES), F32)],
        compiler_params=_params(48, "arbitrary", "arbitrary"),
        name="qkv_prompt",
    )(x_all, w_perm, cos, sin)


def _attn_group_kernel(qkv_ref, o_ref, lse_ref, *, d, m_len, n_heads):
    nb = m_len // Q_BLOCK
    kw = min(2 * Q_BLOCK, m_len)
    lane = lax.broadcasted_iota(jnp.int32, (1, LANES), 1)
    lo = lane < HEAD_DIM
    qi = lax.broadcasted_iota(jnp.int32, (Q_BLOCK, kw), 0)
    kj = lax.broadcasted_iota(jnp.int32, (Q_BLOCK, kw), 1)

    def block(i, carry):
        r = i // nb
        n = i % nb
        ks = pl.multiple_of(jnp.maximum(n - 1, 0) * Q_BLOCK, Q_BLOCK)
        qs = pl.multiple_of(n * Q_BLOCK, Q_BLOCK)
        delta = qs - ks + qi - kj
        valid = (delta >= 0) & (delta <= Q_BLOCK)
        if d == 1:
            dst = pl.ds(qs, Q_BLOCK)
        else:
            dst = pl.ds(r + qs * d, Q_BLOCK, stride=d)
        lses = []
        for p in range(SEC // LANES):
            cols = slice(p * LANES, (p + 1) * LANES)
            q2 = qkv_ref[0, r, pl.ds(qs, Q_BLOCK), cols]
            k2 = qkv_ref[0, r, pl.ds(ks, kw), SEC + p * LANES:SEC + (p + 1) * LANES]
            v2 = qkv_ref[0, r, pl.ds(ks, kw), 2 * SEC + p * LANES:2 * SEC + (p + 1) * LANES]
            o_pair = jnp.zeros((Q_BLOCK, LANES), F32)
            for half in range(2):
                if 2 * p + half >= n_heads:
                    continue
                msk = lo if half == 0 else jnp.logical_not(lo)
                qm = jnp.where(msk, q2, jnp.zeros_like(q2))
                vm = jnp.where(msk, v2, jnp.zeros_like(v2))
                s = lax.dot_general(qm, k2, (((1,), (1,)), ((), ())), preferred_element_type=F32)
                s = jnp.where(valid, s, NEG_INF)
                mx = jnp.max(s, axis=1, keepdims=True)
                pe = jnp.exp(s - mx)
                l = jnp.sum(pe, axis=1, keepdims=True)
                o_pair = o_pair + jnp.dot(pe.astype(BF16), vm, preferred_element_type=F32) / l
                lses.append(mx + jnp.log(l))
            o_ref[0, p, dst, :] = o_pair
        lmax = functools.reduce(jnp.maximum, lses)
        lsum = functools.reduce(lambda a, b: a + b, [jnp.exp(l - lmax) for l in lses])
        glse = lmax + jnp.log(lsum) - math.log(n_heads)
        lse_ref[0, dst, :] = jnp.broadcast_to(glse, (Q_BLOCK, LANES))
        return carry

    lax.fori_loop(0, d * nb, block, 0)


def _attn_group(qkv_g, g, batch, seq):
    h0, h1, _, d = GROUPS[g]
    m_len = seq // d
    return pl.pallas_call(
        functools.partial(_attn_group_kernel, d=d, m_len=m_len, n_heads=h1 - h0),
        grid=(batch,),
        in_specs=[pl.BlockSpec((1, d, m_len, 3 * SEC), lambda b: (b, 0, 0, 0))],
        out_specs=[
            pl.BlockSpec((1, SEC // LANES, seq, LANES), lambda b: (b, 0, 0, 0)),
            pl.BlockSpec((1, seq, LANES), lambda b: (b, 0, 0)),
        ],
        out_shape=[
            jax.ShapeDtypeStruct((batch, SEC // LANES, seq, LANES), F32),
            jax.ShapeDtypeStruct((batch, seq, LANES), F32),
        ],
        compiler_params=_params(40, "arbitrary"),
        name=f"attn_group{g}",
    )(qkv_g)


def _attn_out_kernel(o1_ref, o2_ref, o3_ref, l1_ref, l2_ref, l3_ref, wo_ref, x_ref, g_ref, b_ref, out_ref):
    ls = [l1_ref[0], l2_ref[0], l3_ref[0]]
    mx = jnp.maximum(jnp.maximum(ls[0], ls[1]), ls[2])
    es = [jnp.exp(l - mx) for l in ls]
    inv = float(N_GROUPS) / (es[0] + es[1] + es[2])
    acc = None
    for g, o_ref in enumerate((o1_ref, o2_ref, o3_ref)):
        w = es[g] * inv
        o_g = jnp.concatenate([o_ref[0, p] * w for p in range(SEC // LANES)], axis=1)
        part = jnp.dot(o_g.astype(BF16), wo_ref[g], preferred_element_type=F32)
        acc = part if acc is None else acc + part
    out_ref[...] = _ln(DEEPNORM_ALPHA * x_ref[...] + acc, g_ref[...], b_ref[...])


def _attn_out_prompt(os_, ls_, wo_perm, x_all, ln_g, ln_b, batch, seq):
    bm = 512
    tiles = seq // bm
    t_all = x_all.shape[0]
    row = lambda b, i: (b * tiles + i, 0)
    const2 = lambda b, i: (0, 0)
    return pl.pallas_call(
        _attn_out_kernel,
        grid=(batch, tiles),
        in_specs=[pl.BlockSpec((1, SEC // LANES, bm, LANES), lambda b, i: (b, 0, i, 0))] * 3
        + [pl.BlockSpec((1, bm, LANES), lambda b, i: (b, i, 0))] * 3 + [
            pl.BlockSpec((N_GROUPS, SEC, D_MODEL), lambda b, i: (0, 0, 0)),
            pl.BlockSpec((bm, D_MODEL), row),
            pl.BlockSpec((1, D_MODEL), const2),
            pl.BlockSpec((1, D_MODEL), const2),
        ],
        out_specs=pl.BlockSpec((bm, D_MODEL), row),
        out_shape=jax.ShapeDtypeStruct((t_all, D_MODEL), F32),
        compiler_params=_params(40, "arbitrary", "arbitrary"),
        name="attn_out_prompt",
    )(*os_, *ls_, wo_perm, x_all, ln_g, ln_b)


def _mm_kernel(x_ref, w_ref, o_ref):
    o_ref[...] = jnp.dot(x_ref[...].astype(BF16), w_ref[...], preferred_element_type=F32)


def _mm_rows(x_all, w, row0, rows):
    n = w.shape[1]
    blk0 = row0 // rows
    return pl.pallas_call(
        _mm_kernel,
        grid=(1,),
        in_specs=[
            pl.BlockSpec((rows, D_MODEL), lambda i: (blk0, 0)),
            pl.BlockSpec((D_MODEL, n), lambda i: (0, 0)),
        ],
        out_specs=pl.BlockSpec((rows, n), lambda i: (0, 0)),
        out_shape=jax.ShapeDtypeStruct((rows, n), F32),
        compiler_params=_params(40, "arbitrary"),
        name="mm_rows",
    )(x_all, w)


def _proj_ln_kernel(a_ref, w_ref, x_ref, g_ref, b_ref, prev_ref, out_ref):
    del prev_ref
    acc = jnp.dot(a_ref[...].astype(BF16), w_ref[...], preferred_element_type=F32)
    out_ref[...] = _ln(DEEPNORM_ALPHA * x_ref[...] + acc, g_ref[...], b_ref[...])


def _proj_ln_rows(a, w, x_all, ln_g, ln_b, x_next, row0):
    rows = a.shape[0]
    blk0 = row0 // rows
    const2 = lambda i: (0, 0)
    return pl.pallas_call(
        _proj_ln_kernel,
        grid=(1,),
        in_specs=[
            pl.BlockSpec((rows, a.shape[1]), const2),
            pl.BlockSpec(w.shape, const2),
            pl.BlockSpec((rows, D_MODEL), lambda i: (blk0, 0)),
            pl.BlockSpec((1, D_MODEL), const2),
            pl.BlockSpec((1, D_MODEL), const2),
            pl.BlockSpec(memory_space=pl.ANY),
        ],
        out_specs=pl.BlockSpec((rows, D_MODEL), lambda i: (blk0, 0)),
        out_shape=jax.ShapeDtypeStruct(x_next.shape, F32),
        input_output_aliases={5: 0},
        compiler_params=_params(40, "arbitrary"),
        name="proj_ln_rows",
    )(a, w, x_all, ln_g, ln_b, x_next)


def _sgu_kernel(*refs, bm, chunk, emit_v, aliased):
    (x_ref, win_ref, bin_ref, lng_ref, lnb_ref, ws_ref, bs_ref, wout_ref, g1_ref, b1_ref) = refs[:10]
    rest = refs[10 + (1 if aliased else 0):]
    out_ref = rest[0]
    v_ref = rest[1] if emit_v else None
    ug_s = rest[-1]
    width = D_MODEL
    x = x_ref[...]
    z = jnp.dot(x.astype(BF16), win_ref[...], preferred_element_type=F32) + bin_ref[...]
    z = 0.5 * z * (1.0 + lax.erf(z * (2.0 ** -0.5)))
    v = _ln(z[:, width:], lng_ref[...], lnb_ref[...])
    if emit_v:
        v_ref[...] = v
    ii = lax.broadcasted_iota(jnp.int32, (SGU_CHUNK, SGU_CHUNK), 0)
    jj = lax.broadcasted_iota(jnp.int32, (SGU_CHUNK, SGU_CHUNK), 1)
    causal = (jj <= ii) & ((ii // chunk) == (jj // chunk))
    gw = width // SGU_GROUPS
    for gi in range(SGU_GROUPS):
        wsm = jnp.where(causal, ws_ref[gi], 0.0).astype(BF16)
        for c in range(bm // SGU_CHUNK):
            rows = slice(c * SGU_CHUNK, (c + 1) * SGU_CHUNK)
            cols = slice(gi * gw, (gi + 1) * gw)
            gate = jnp.dot(wsm, v[rows, cols].astype(BF16), preferred_element_type=F32) + bs_ref[gi]
            ug_s[rows, cols] = (z[rows, cols] * gate).astype(BF16)
    y = jnp.dot(ug_s[...], wout_ref[...], preferred_element_type=F32)
    out_ref[...] = _ln(DEEPNORM_ALPHA * x + y, g1_ref[...], b1_ref[...])


def _sgu(x_all, x_next, weights, row0, rows, bm, chunk, emit_v):
    w_in, b_in, ln_g, ln_b, ws, bs, w_out, g1, b1 = weights
    t_all = x_all.shape[0]
    blk0 = row0 // bm
    row = lambda i: (blk0 + i, 0)
    const2 = lambda i: (0, 0)
    const3 = lambda i: (0, 0, 0)
    aliased = x_next is not None
    in_specs = [
        pl.BlockSpec((bm, D_MODEL), row),
        pl.BlockSpec(w_in.shape, const2),
        pl.BlockSpec((1, 2 * D_MODEL), const2),
        pl.BlockSpec((1, D_MODEL), const2),
        pl.BlockSpec((1, D_MODEL), const2),
        pl.BlockSpec(ws.shape, const3),
        pl.BlockSpec(bs.shape, const3),
        pl.BlockSpec(w_out.shape, const2),
        pl.BlockSpec((1, D_MODEL), const2),
        pl.BlockSpec((1, D_MODEL), const2),
    ]
    args = [x_all, w_in, b_in, ln_g, ln_b, ws, bs, w_out, g1, b1]
    aliases = {}
    if aliased:
        in_specs.append(pl.BlockSpec(memory_space=pl.ANY))
        args.append(x_next)
        aliases = {10: 0}
    out_specs = [pl.BlockSpec((bm, D_MODEL), row)]
    out_shape = [jax.ShapeDtypeStruct((t_all, D_MODEL), F32)]
    if emit_v:
        out_specs.append(pl.BlockSpec((bm, D_MODEL), lambda i: (i, 0)))
        out_shape.append(jax.ShapeDtypeStruct((rows, D_MODEL), F32))
    return pl.pallas_call(
        functools.partial(_sgu_kernel, bm=bm, chunk=chunk, emit_v=emit_v, aliased=aliased),
        grid=(rows // bm,),
        in_specs=in_specs,
        out_specs=out_specs,
        out_shape=out_shape,
        scratch_shapes=[pltpu.VMEM((bm, D_MODEL), BF16)],
        input_output_aliases=aliases,
        compiler_params=_params(48, "arbitrary"),
        name="sgu",
    )(*args)


def _router_kernel(x_ref, wr_ref, br_ref, idx_ref, gate_ref, cnt_ref, *, bm, t_all):
    x = x_ref[...]
    w = wr_ref[...]
    x_hi = x.astype(BF16)
    w_hi = w.astype(BF16)
    x_lo = (x - x_hi.astype(F32)).astype(BF16)
    w_lo = (w - w_hi.astype(F32)).astype(BF16)
    logits = (jnp.dot(x_hi, w_hi, preferred_element_type=F32) + jnp.dot(x_lo, w_hi, preferred_element_type=F32)
              + jnp.dot(x_hi, w_lo, preferred_element_type=F32))
    logits = logits + br_ref[...]
    lane = lax.broadcasted_iota(jnp.int32, (bm, LANES), 1)
    lane_f = lane.astype(F32)
    row = pl.program_id(0) * bm + lax.broadcasted_iota(jnp.int32, (bm, LANES), 0)
    logits = jnp.where(row < t_all, logits, 0.0)
    logits = jnp.where(lane < N_EXPERTS, logits, -jnp.inf)
    vals, idxs = [], []
    for _ in range(TOP_K):
        m = jnp.max(logits, axis=1, keepdims=True)
        i = jnp.min(jnp.where(logits == m, lane_f, float(LANES)), axis=1, keepdims=True)
        vals.append(m)
        idxs.append(i)
        logits = jnp.where(lane_f == i, -jnp.inf, logits)
    es = [jnp.exp(v - vals[0]) for v in vals]
    inv = 1.0 / functools.reduce(lambda a, b: a + b, es)
    idx_out = jnp.zeros((bm, LANES), F32)
    gate_out = jnp.zeros((bm, LANES), F32)
    chosen = jnp.zeros((bm, LANES), F32)
    for k in range(TOP_K):
        idx_out = jnp.where(lane == k, idxs[k], idx_out)
        gate_out = jnp.where(lane == k, es[k] * inv, gate_out)
        chosen = chosen + jnp.where((lane_f == idxs[k]) & (row < t_all), 1.0, 0.0)
    idx_ref[...] = idx_out.astype(jnp.int32)
    gate_ref[...] = gate_out
    cnt_ref[0] = jnp.broadcast_to(jnp.sum(chosen, axis=0, keepdims=True), (SUB, LANES))


def _router(x_all, w_r, b_r):
    bm = MOE_TILE
    t_all = x_all.shape[0]
    n_tiles = pl.cdiv(t_all, bm)
    row = lambda i: (i, 0)
    const2 = lambda i: (0, 0)
    return pl.pallas_call(
        functools.partial(_router_kernel, bm=bm, t_all=t_all),
        grid=(n_tiles,),
        in_specs=[
            pl.BlockSpec((bm, D_MODEL), row),
            pl.BlockSpec((D_MODEL, LANES), const2),
            pl.BlockSpec((1, LANES), const2),
        ],
        out_specs=[pl.BlockSpec((bm, LANES), row), pl.BlockSpec((bm, LANES), row),
                   pl.BlockSpec((1, SUB, LANES), lambda i: (i, 0, 0))],
        out_shape=[
            jax.ShapeDtypeStruct((t_all, LANES), jnp.int32),
            jax.ShapeDtypeStruct((t_all, LANES), F32),
            jax.ShapeDtypeStruct((n_tiles, SUB, LANES), F32),
        ],
        compiler_params=_params(32, "arbitrary"),
        name="router",
    )(x_all, w_r, b_r)


def _local_slots(idx, row_valid):
    tile = idx.shape[0]
    lane = lax.broadcasted_iota(jnp.int32, (tile, LANES), 1)
    idx = jnp.where(row_valid, idx, -1)
    picks = [lane == idx[:, k:k + 1] for k in range(TOP_K)]
    chosen = functools.reduce(lambda a, b: a + b, [p.astype(F32) for p in picks])
    earlier = (lax.broadcasted_iota(jnp.int32, (tile, tile), 1)
               < lax.broadcasted_iota(jnp.int32, (tile, tile), 0)).astype(BF16)
    rank = jnp.dot(earlier, chosen.astype(BF16), preferred_element_type=F32)
    per_expert = jnp.broadcast_to(jnp.sum(chosen, axis=0, keepdims=True), (SUB, LANES))
    lower = (lax.broadcasted_iota(jnp.int32, (LANES, LANES), 0)
             < lax.broadcasted_iota(jnp.int32, (LANES, LANES), 1)).astype(BF16)
    base = jnp.dot(per_expert.astype(BF16), lower, preferred_element_type=F32)[0:1]
    place = base + rank + 1.0
    return [jnp.sum(jnp.where(p, place, 0.0), axis=1, keepdims=True) - 1.0 for p in picks]


def _segment_copies(cnt_ref, off_ref, j, hbm, buf, sem, to_hbm):
    local = 0
    for e in range(N_EXPERTS):
        cnt = cnt_ref[j * N_EXPERTS + e]
        off = off_ref[j * N_EXPERTS + e]
        for bit in range(MOE_TILE.bit_length()):
            size = 1 << bit

            @pl.when((cnt & size) != 0)
            def _(cnt=cnt, off=off, local=local, size=size):
                done = cnt & (size - 1)
                a = buf.at[pl.ds(pl.multiple_of((local + done) * SUB, SUB), size * SUB), :]
                b = hbm.at[pl.ds(pl.multiple_of((off + done) * SUB, SUB), size * SUB), :]
                (pltpu.make_async_copy(a, b, sem) if to_hbm else pltpu.make_async_copy(b, a, sem)).start()

        local = local + cnt


def _rows_to_tiles(ref, value):
    rows = value.shape[0]
    for s in range(D_MODEL // LANES):
        ref[pl.ds(s, rows, stride=SUB), :] = value[:, s * LANES:(s + 1) * LANES]


def _tiles_to_rows(ref, rows):
    return jnp.concatenate([ref[pl.ds(s, rows, stride=SUB), :] for s in range(D_MODEL // LANES)], axis=1)


def _dispatch_kernel(cnt_ref, off_ref, x_ref, idx_ref, xs_hbm, slot_ref, buf, sem, *, t_all, n_tiles):
    j = pl.program_id(0)
    par = j % 2
    tile = MOE_TILE
    row = j * tile + lax.broadcasted_iota(jnp.int32, (tile, LANES), 0)
    row_valid = row < t_all
    slots = _local_slots(idx_ref[...], row_valid)
    lane = lax.broadcasted_iota(jnp.int32, (tile, LANES), 1)
    slot_out = jnp.zeros((tile, LANES), F32)
    for k in range(TOP_K):
        slot_out = jnp.where(lane == k, slots[k], slot_out)
    slot_ref[...] = slot_out
    slots_t = slot_out.T
    pos = lax.broadcasted_iota(jnp.int32, (TOP_K * tile, tile), 0).astype(F32)
    hit = functools.reduce(jnp.logical_or, [pos == slots_t[k:k + 1, :] for k in range(TOP_K)])
    place = jnp.where(hit, 1.0, 0.0).astype(BF16)
    in_range = (j * tile + lax.broadcasted_iota(jnp.int32, (tile, D_MODEL), 0)) < t_all
    x = jnp.where(in_range, x_ref[...], 0.0).astype(BF16)
    ordered = jnp.dot(place, x, preferred_element_type=F32)
    _rows_to_tiles(buf.at[par], ordered)

    full = TOP_K * tile * SUB
    last = TOP_K * (t_all - (n_tiles - 1) * tile) * SUB

    @pl.when(j > 0)
    def _():
        pltpu.make_async_copy(buf.at[1 - par], xs_hbm.at[pl.ds(0, full), :], sem.at[0]).wait()

    _segment_copies(cnt_ref, off_ref, j, xs_hbm, buf.at[par], sem.at[0], True)

    @pl.when(j == n_tiles - 1)
    def _():
        pltpu.make_async_copy(buf.at[par, pl.ds(0, last), :], xs_hbm.at[pl.ds(0, last), :], sem.at[0]).wait()


def _dispatch(cnt, off, x_all, idx, n_slots):
    t_all = x_all.shape[0]
    n_tiles = pl.cdiv(t_all, MOE_TILE)
    row = lambda j, c, o: (j, 0)
    grid_spec = pltpu.PrefetchScalarGridSpec(
        num_scalar_prefetch=2,
        grid=(n_tiles,),
        in_specs=[pl.BlockSpec((MOE_TILE, D_MODEL), row), pl.BlockSpec((MOE_TILE, LANES), row)],
        out_specs=[pl.BlockSpec(memory_space=pl.ANY), pl.BlockSpec((MOE_TILE, LANES), row)],
        scratch_shapes=[pltpu.VMEM((2, TOP_K * MOE_TILE * SUB, LANES), F32), pltpu.SemaphoreType.DMA((1,))],
    )
    return pl.pallas_call(
        functools.partial(_dispatch_kernel, t_all=t_all, n_tiles=n_tiles),
        grid_spec=grid_spec,
        out_shape=[jax.ShapeDtypeStruct((n_slots * SUB, LANES), F32), jax.ShapeDtypeStruct((t_all, LANES), F32)],
        compiler_params=_params(40, "arbitrary"),
        name="moe_dispatch",
    )(cnt, off, x_all, idx)


def _expert_kernel(be_ref, nreal_ref, xs_ref, wgu_ref, bgu_ref, wdn_ref, bdn_ref, ys_ref, wgu_bf, wdn_bf):
    i = pl.program_id(0)
    ff = wdn_ref.shape[1]

    @pl.when(i < nreal_ref[0])
    def _():
        @pl.when((i == 0) | (be_ref[i] != be_ref[jnp.maximum(i - 1, 0)]))
        def _():
            wgu_bf[...] = wgu_ref[0].astype(BF16)
            wdn_bf[...] = wdn_ref[0].astype(BF16)

        x = _tiles_to_rows(xs_ref, MOE_BLOCK).astype(BF16)
        h = jnp.dot(x, wgu_bf[...], preferred_element_type=F32) + bgu_ref[0]
        gate = jnp.minimum(h[:, :ff], SWIGLU_LIMIT)
        up = jnp.clip(h[:, ff:], -SWIGLU_LIMIT, SWIGLU_LIMIT)
        act = (up + 1.0) * gate * jax.nn.sigmoid(SWIGLU_ALPHA * gate)
        y = jnp.dot(act.astype(BF16), wdn_bf[...], preferred_element_type=F32) + bdn_ref[0]
        _rows_to_tiles(ys_ref, y)


def _experts(block_expert, n_real, xs, w_gu, b_gu, w_dn, b_dn):
    n_blocks = xs.shape[0] // (MOE_BLOCK * SUB)
    ff = w_dn.shape[1]
    rows = lambda i, be, nr: (jnp.minimum(i, nr[0] - 1), 0)
    by_expert = lambda i, be, nr: (be[i], 0, 0)
    grid_spec = pltpu.PrefetchScalarGridSpec(
        num_scalar_prefetch=2,
        grid=(n_blocks,),
        in_specs=[
            pl.BlockSpec((MOE_BLOCK * SUB, LANES), rows),
            pl.BlockSpec((1, D_MODEL, 2 * ff), by_expert),
            pl.BlockSpec((1, 1, 2 * ff), by_expert),
            pl.BlockSpec((1, ff, D_MODEL), by_expert),
            pl.BlockSpec((1, 1, D_MODEL), by_expert),
        ],
        out_specs=pl.BlockSpec((MOE_BLOCK * SUB, LANES), rows),
        scratch_shapes=[pltpu.VMEM((D_MODEL, 2 * ff), BF16), pltpu.VMEM((ff, D_MODEL), BF16)],
    )
    return pl.pallas_call(
        _expert_kernel,
        grid_spec=grid_spec,
        out_shape=jax.ShapeDtypeStruct(xs.shape, F32),
        compiler_params=_params(52, "arbitrary"),
        name="experts",
    )(block_expert, n_real, xs, w_gu, b_gu, w_dn, b_dn)


def _combine_kernel(cnt_ref, off_ref, x_ref, slot_ref, gate_ref, g_ref, b_ref, ys_hbm, out_ref, buf, sem, *, t_all, n_tiles):
    j = pl.program_id(0)
    par = j % 2
    tile = MOE_TILE
    full = TOP_K * tile * SUB
    last = TOP_K * (t_all - (n_tiles - 1) * tile) * SUB

    @pl.when(j == 0)
    def _():
        _segment_copies(cnt_ref, off_ref, 0, ys_hbm, buf.at[0], sem.at[0], False)

    @pl.when(j + 1 < n_tiles)
    def _():
        _segment_copies(cnt_ref, off_ref, j + 1, ys_hbm, buf.at[1 - par], sem.at[1 - par], False)

    @pl.when(j < n_tiles - 1)
    def _():
        pltpu.make_async_copy(ys_hbm.at[pl.ds(0, full), :], buf.at[par], sem.at[par]).wait()

    @pl.when(j == n_tiles - 1)
    def _():
        pltpu.make_async_copy(ys_hbm.at[pl.ds(0, last), :], buf.at[par, pl.ds(0, last), :], sem.at[par]).wait()

    n_valid = TOP_K * jnp.minimum(tile, t_all - j * tile)
    ys = _tiles_to_rows(buf.at[par], TOP_K * tile)
    ys = jnp.where(lax.broadcasted_iota(jnp.int32, ys.shape, 0) < n_valid, ys, 0.0).astype(BF16)
    slots = slot_ref[...]
    gates = gate_ref[...]
    pos = lax.broadcasted_iota(jnp.int32, (tile, TOP_K * tile), 1).astype(F32)
    weights = jnp.zeros((tile, TOP_K * tile), F32)
    for k in range(TOP_K):
        weights = jnp.where(pos == slots[:, k:k + 1], gates[:, k:k + 1], weights)
    y = jnp.dot(weights.astype(BF16), ys, preferred_element_type=F32)
    out_ref[...] = _ln(DEEPNORM_ALPHA * x_ref[...] + y, g_ref[...], b_ref[...])


def _combine(cnt, off, x_all, slots, gates, ln_g, ln_b, ys):
    t_all = x_all.shape[0]
    n_tiles = pl.cdiv(t_all, MOE_TILE)
    row = lambda j, c, o: (j, 0)
    const2 = lambda j, c, o: (0, 0)
    grid_spec = pltpu.PrefetchScalarGridSpec(
        num_scalar_prefetch=2,
        grid=(n_tiles,),
        in_specs=[
            pl.BlockSpec((MOE_TILE, D_MODEL), row),
            pl.BlockSpec((MOE_TILE, LANES), row),
            pl.BlockSpec((MOE_TILE, LANES), row),
            pl.BlockSpec((1, D_MODEL), const2),
            pl.BlockSpec((1, D_MODEL), const2),
            pl.BlockSpec(memory_space=pl.ANY),
        ],
        out_specs=pl.BlockSpec((MOE_TILE, D_MODEL), row),
        scratch_shapes=[pltpu.VMEM((2, TOP_K * MOE_TILE * SUB, LANES), F32), pltpu.SemaphoreType.DMA((2,))],
    )
    return pl.pallas_call(
        functools.partial(_combine_kernel, t_all=t_all, n_tiles=n_tiles),
        grid_spec=grid_spec,
        out_shape=jax.ShapeDtypeStruct((t_all, D_MODEL), F32),
        compiler_params=_params(40, "arbitrary"),
        name="moe_combine",
    )(cnt, off, x_all, slots, gates, ln_g, ln_b, ys)


def _moe(x_all, w_r, b_r, w_gu, b_gu, w_dn, b_dn, ln_g, ln_b):
    t_all = x_all.shape[0]
    idx, gates, tile_counts = _router(x_all, w_r, b_r)
    cnt = tile_counts[:, 0, :N_EXPERTS].astype(jnp.int32)
    counts = jnp.sum(cnt, axis=0)
    padded = (counts + MOE_BLOCK - 1) // MOE_BLOCK * MOE_BLOCK
    pad_end = jnp.cumsum(padded)
    pad_start = pad_end - padded
    off = pad_start[None, :] + jnp.cumsum(cnt, axis=0) - cnt
    n_blocks = -(-t_all * TOP_K // MOE_BLOCK) + N_EXPERTS
    block_start = jnp.arange(n_blocks, dtype=jnp.int32) * MOE_BLOCK
    block_expert = jnp.minimum(
        jnp.sum((pad_end[None, :] <= block_start[:, None]).astype(jnp.int32), axis=1), N_EXPERTS - 1)
    n_real = (pad_end[-1:] // MOE_BLOCK).astype(jnp.int32)
    cnt_flat, off_flat = cnt.reshape(-1), off.reshape(-1).astype(jnp.int32)
    xs, slots = _dispatch(cnt_flat, off_flat, x_all, idx, n_blocks * MOE_BLOCK)
    ys = _experts(block_expert, n_real, xs, w_gu, b_gu, w_dn, b_dn)
    return _combine(cnt_flat, off_flat, x_all, slots, gates, ln_g, ln_b, ys)


def _rope_tables(pos):
    half = HEAD_DIM // 2
    inv_freq = ROPE_THETA ** (-jnp.arange(half, dtype=F32) / half)
    ang = pos.astype(F32)[:, None] * inv_freq[None, :]
    cos = jnp.cos(ang)
    sin = jnp.sin(ang)
    cos = jnp.concatenate([cos, cos], axis=1)
    sin = jnp.concatenate([-sin, sin], axis=1)
    reps = LANES // HEAD_DIM
    return jnp.tile(cos, (1, reps)), jnp.tile(sin, (1, reps))


def _permute_qkv_weight(w_qkv):
    w3 = w_qkv.reshape(D_MODEL, 3, N_HEADS * HEAD_DIM)
    secs = []
    for part in range(3):
        for (h0, h1, _, _) in GROUPS:
            sec = w3[:, part, h0 * HEAD_DIM:h1 * HEAD_DIM]
            secs.append(jnp.pad(sec, ((0, 0), (0, SEC - sec.shape[1]))))
    return jnp.concatenate(secs, axis=1).astype(BF16)


def _permute_out_weight(w_o):
    secs = []
    for (h0, h1, _, _) in GROUPS:
        sec = w_o[h0 * HEAD_DIM:h1 * HEAD_DIM]
        secs.append(jnp.pad(sec, ((0, SEC - sec.shape[0]), (0, 0))))
    return jnp.stack(secs).astype(BF16)


def _sample_attn_kernel(y_ref, cos_ref, sin_ref, k1_ref, v1_ref, k2_ref, v2_ref, k3_ref, v3_ref, o_ref, kv_ref, *, dec_s):
    y = y_ref[0]
    cos = cos_ref[...]
    sin = sin_ref[...]
    lane = lax.broadcasted_iota(jnp.int32, (SUB, LANES), 1)
    first_half = (lane & (HEAD_DIM // 2)) == 0
    blocks_per_sec = SEC // LANES
    n_rot = 2 * N_GROUPS * blocks_per_sec
    blocks = []
    for c in range(3 * N_GROUPS * blocks_per_sec):
        blk = y[:, c * LANES:(c + 1) * LANES]
        if c < n_rot:
            swapped = jnp.where(first_half, pltpu.roll(blk, LANES - HEAD_DIM // 2, 1), pltpu.roll(blk, HEAD_DIM // 2, 1))
            blk = blk * cos + swapped * sin
        if c < n_rot // 2:
            blk = blk * (HEAD_DIM ** -0.5)
        blocks.append(blk)
    section = lambda s: jnp.concatenate(blocks[s * blocks_per_sec:(s + 1) * blocks_per_sec], axis=1)
    kv_ref[0] = jnp.concatenate(blocks[n_rot // 2:], axis=1)

    seg = (lax.broadcasted_iota(jnp.int32, (SEC, LANES), 0) // HEAD_DIM
           == lax.broadcasted_iota(jnp.int32, (SEC, LANES), 1)).astype(BF16)
    seg_t = (lax.broadcasted_iota(jnp.int32, (LANES, SEC), 1) // HEAD_DIM
             == lax.broadcasted_iota(jnp.int32, (LANES, SEC), 0)).astype(BF16)
    n_keys = Q_BLOCK + SUB
    key_row = lax.broadcasted_iota(jnp.int32, (n_keys, LANES), 0)
    head_lane = lax.broadcasted_iota(jnp.int32, (1, LANES), 1)
    out_row = lax.broadcasted_iota(jnp.int32, (SUB, 1), 0)

    outs, glses = [], []
    for g, (k_ref, v_ref) in enumerate(((k1_ref, v1_ref), (k2_ref, v2_ref), (k3_ref, v3_ref))):
        h0, h1, _, d = GROUPS[g]
        q_g, k_new, v_new = section(g), section(N_GROUPS + g), section(2 * N_GROUPS + g)
        o_g = jnp.zeros((SUB, SEC), F32)
        l_g = jnp.zeros((SUB, 1), F32)
        for j in range(dec_s):
            k_buf = k_ref[0] if d == 1 else k_ref[0, j]
            v_buf = v_ref[0] if d == 1 else v_ref[0, j]
            keys = jnp.concatenate([k_buf, k_new], axis=0)
            vals = jnp.concatenate([v_buf, v_new], axis=0)
            new_j = key_row - Q_BLOCK
            if d == 1:
                valid = ((key_row < Q_BLOCK) & (key_row >= j)) | ((key_row >= Q_BLOCK) & (new_j <= j))
            else:
                valid = (key_row < Q_BLOCK) | (new_j == j)
            prod = (keys * q_g[j:j + 1, :]).astype(BF16)
            s = jnp.dot(prod, seg, preferred_element_type=F32)
            s = jnp.where(valid, s, NEG_INF)
            mx = jnp.max(s, axis=0, keepdims=True)
            pe = jnp.exp(s - mx)
            l = jnp.sum(pe, axis=0, keepdims=True)
            spread = jnp.dot((pe / l).astype(BF16), seg_t, preferred_element_type=F32)
            o_j = jnp.sum(spread * vals, axis=0, keepdims=True)
            lse = jnp.where(head_lane < h1 - h0, mx + jnp.log(l), -jnp.inf)
            lmax = jnp.max(lse, axis=1, keepdims=True)
            glse = lmax + jnp.log(jnp.sum(jnp.exp(lse - lmax), axis=1, keepdims=True)) - math.log(h1 - h0)
            o_g = jnp.where(out_row == j, o_j, o_g)
            l_g = jnp.where(out_row == j, glse, l_g)
        outs.append(o_g)
        glses.append(l_g)
    gmax = jnp.maximum(jnp.maximum(glses[0], glses[1]), glses[2])
    es = [jnp.exp(l - gmax) for l in glses]
    inv = float(N_GROUPS) / (es[0] + es[1] + es[2])
    o_ref[0] = jnp.concatenate([outs[g] * (es[g] * inv) for g in range(N_GROUPS)], axis=1)


def _dense_window_rows(buf, part, d, dec_s):
    dec_b, lb, _, hg, _ = buf.shape
    rows = buf[:, :, part].reshape(dec_b, lb, hg * HEAD_DIM)
    rows = jnp.pad(rows, ((0, 0), (0, 0), (0, SEC - hg * HEAD_DIM)))
    if d == 1:
        return rows
    return rows.reshape(dec_b, lb // d, d, SEC)[:, :, :dec_s].transpose(0, 2, 1, 3)


def _sample_attention(y_s, bufs, cos_s, sin_s, dec_b, dec_s):
    n_cols = 3 * N_GROUPS * SEC
    y_pad = jnp.pad(y_s.reshape(dec_b, dec_s, n_cols), ((0, 0), (0, SUB - dec_s), (0, 0)))
    dense, specs = [], []
    for (h0, h1, win, d), buf in zip(GROUPS, bufs):
        assert buf.shape[1] == win and win // d == Q_BLOCK and (d == 1 or dec_s <= d)
        for part in range(2):
            dense.append(_dense_window_rows(buf, part, d, dec_s))
            if d == 1:
                specs.append(pl.BlockSpec((1, Q_BLOCK, SEC), lambda b: (b, 0, 0)))
            else:
                specs.append(pl.BlockSpec((1, dec_s, Q_BLOCK, SEC), lambda b: (b, 0, 0, 0)))
    o, kv_new = pl.pallas_call(
        functools.partial(_sample_attn_kernel, dec_s=dec_s),
        grid=(dec_b,),
        in_specs=[
            pl.BlockSpec((1, SUB, n_cols), lambda b: (b, 0, 0)),
            pl.BlockSpec((SUB, LANES), lambda b: (0, 0)),
            pl.BlockSpec((SUB, LANES), lambda b: (0, 0)),
        ] + specs,
        out_specs=[
            pl.BlockSpec((1, SUB, N_GROUPS * SEC), lambda b: (b, 0, 0)),
            pl.BlockSpec((1, SUB, 2 * N_GROUPS * SEC), lambda b: (b, 0, 0)),
        ],
        out_shape=[
            jax.ShapeDtypeStruct((dec_b, SUB, N_GROUPS * SEC), F32),
            jax.ShapeDtypeStruct((dec_b, SUB, 2 * N_GROUPS * SEC), F32),
        ],
        compiler_params=_params(32, "arbitrary"),
        name="sample_attn",
    )(y_pad, cos_s, sin_s, *dense)
    new_rows = []
    for g, (h0, h1, _, _) in enumerate(GROUPS):
        hg = h1 - h0
        halves = [
            kv_new[:, :dec_s, (part * N_GROUPS + g) * SEC:(part * N_GROUPS + g) * SEC + hg * HEAD_DIM]
            .reshape(dec_b, dec_s, hg, HEAD_DIM) for part in range(2)]
        new_rows.append(jnp.stack(halves, axis=2))
    return o[:, :dec_s].reshape(dec_b * dec_s, N_GROUPS * SEC), new_rows


def _row2(a):
    return a.reshape(1, -1)


def _attn_layer(x_all, bufs, w_qkv, w_o, g1, b1, cos_p, sin_p, dims):
    batch, seq, dec_b, dec_s = dims
    t_prompt = batch * seq
    t_sample = dec_b * dec_s
    w_perm = _permute_qkv_weight(w_qkv)
    wo_perm = _permute_out_weight(w_o)
    outs = _qkv_prompt(x_all, w_perm, cos_p, sin_p, batch, seq)
    qkv_groups, windows = outs[:N_GROUPS], outs[N_GROUPS:]
    os_, ls_ = [], []
    for g in range(N_GROUPS):
        o_g, l_g = _attn_group(qkv_groups[g], g, batch, seq)
        os_.append(o_g)
        ls_.append(l_g)
    x_next = _attn_out_prompt(os_, ls_, wo_perm, x_all, g1, b1, batch, seq)
    rows_p = []
    for g, (h0, h1, _, _) in enumerate(GROUPS):
        hg = h1 - h0
        halves = [windows[part * N_GROUPS + g][:, :, :hg * HEAD_DIM].reshape(batch, -1, hg, HEAD_DIM) for part in range(2)]
        rows_p.append(jnp.stack(halves, axis=2))
    y_s = _mm_rows(x_all, w_perm, t_prompt, t_sample)
    cos_s, sin_s = _rope_tables(PAST_LEN + jnp.arange(SUB, dtype=jnp.int32))
    o_s, rows_s = _sample_attention(y_s, bufs, cos_s, sin_s, dec_b, dec_s)
    x_all = _proj_ln_rows(o_s, wo_perm.reshape(N_GROUPS * SEC, D_MODEL), x_all, g1, b1, x_next, t_prompt)
    return x_all, rows_p, rows_s


def _sgu_layer(x_all, w_in, b_in, ln_g, ln_b, ws, bs, w_out, g1, b1, dims):
    batch, seq, dec_b, dec_s = dims
    t_prompt = batch * seq
    t_sample = dec_b * dec_s
    common = (w_in.astype(BF16), _row2(b_in), _row2(ln_g), _row2(ln_b))
    tail = (w_out.astype(BF16), g1, b1)
    bs_p = jnp.broadcast_to(bs[:, :, None], (SGU_GROUPS, SGU_CHUNK, SGU_CHUNK))
    x_next = _sgu(x_all, None, common + (ws, bs_p) + tail, 0, t_prompt, 256, SGU_CHUNK, False)[0]
    c = min(SGU_CHUNK, dec_s)
    reps = SGU_CHUNK // c
    ws_s = jnp.tile(ws[:, :c, :c], (1, reps, reps))
    bs_s = jnp.broadcast_to(jnp.tile(bs[:, :c], (1, reps))[:, :, None], (SGU_GROUPS, SGU_CHUNK, SGU_CHUNK))
    x_all, v_new = _sgu(x_all, x_next, common + (ws_s, bs_s) + tail, t_prompt, t_sample, t_sample, c, True)
    return x_all, v_new.reshape(dec_b, dec_s, D_MODEL)


def _moe_layer(x_all, w_router, b_router, w_gu, b_gu, w_dn, b_dn, ln_g, ln_b):
    w_r = jnp.pad(w_router, ((0, 0), (0, LANES - N_EXPERTS)))
    b_r = jnp.pad(b_router, (0, LANES - N_EXPERTS)).reshape(1, LANES)
    return _moe(
        x_all, w_r, b_r,
        w_gu, b_gu.reshape(N_EXPERTS, 1, -1),
        w_dn, b_dn.reshape(N_EXPERTS, 1, -1),
        ln_g, ln_b)


def kernel(x_prompt, x_sample, cache_kv_w128, cache_kv_w512, cache_kv_w2048, attn_w_qkv, attn_w_o, sgu_w_in, sgu_b_in, sgu_ln_g, sgu_ln_b, sgu_w_s, sgu_b_s, sgu_w_out, moe_w_router, moe_b_router, moe_w_gu, moe_b_gu, moe_w_down, moe_b_down, ln1_g, ln1_b, ln2_g, ln2_b):
    batch, seq, _ = x_prompt.shape
    dec_b, dec_s, _ = x_sample.shape
    t_prompt = batch * seq
    t_sample = dec_b * dec_s
    caches = (cache_kv_w128, cache_kv_w512, cache_kv_w2048)
    x_all = jnp.concatenate([x_prompt.reshape(t_prompt, D_MODEL), x_sample.reshape(t_sample, D_MODEL)], axis=0)
    cos_p, sin_p = _rope_tables(jnp.arange(seq, dtype=jnp.int32))
    row2 = _row2
    dims = (batch, seq, dec_b, dec_s)

    kv_prompt = [[] for _ in GROUPS]
    kv_sample = [[] for _ in GROUPS]
    v_rows = []
    for i in range(DEPTH):
        j = i // 2
        g1, b1 = row2(ln1_g[i]), row2(ln1_b[i])
        if i % 2 == 0:
            x_all, rows_p, rows_s = _attn_layer(
                x_all, [c[j] for c in caches], attn_w_qkv[j], attn_w_o[j], g1, b1, cos_p, sin_p, dims)
            for g in range(N_GROUPS):
                kv_prompt[g].append(rows_p[g])
                kv_sample[g].append(rows_s[g])
        else:
            x_all, v_new = _sgu_layer(
                x_all, sgu_w_in[j], sgu_b_in[j], sgu_ln_g[j], sgu_ln_b[j], sgu_w_s[j], sgu_b_s[j], sgu_w_out[j],
                g1, b1, dims)
            v_rows.append(v_new)
        x_all = _moe_layer(
            x_all, moe_w_router[i], moe_b_router[i], moe_w_gu[i], moe_b_gu[i], moe_w_down[i], moe_b_down[i],
            row2(ln2_g[i]), row2(ln2_b[i]))
    y_prompt = x_all[:t_prompt].reshape(batch, seq, D_MODEL)
    y_sample = x_all[t_prompt:].reshape(dec_b, dec_s, D_MODEL)
    kv_sample = [jnp.concatenate([cache[:, :, dec_s:], jnp.stack(rows)], axis=2) for cache, rows in zip(caches, kv_sample)]
    return (
        y_prompt, y_sample,
        jnp.stack(kv_prompt[0]), jnp.stack(kv_prompt[1]), jnp.stack(kv_prompt[2]),
        kv_sample[0], kv_sample[1], kv_sample[2],
        jnp.stack(v_rows),
    )
```

```python
import functools
import math

import jax
import jax.numpy as jnp
from jax import lax
from jax.experimental import pallas as pl
from jax.experimental.pallas import tpu as pltpu

F32 = jnp.float32
BF16 = jnp.bfloat16

D_MODEL = 1024
HEAD_DIM = 64
N_HEADS = D_MODEL // HEAD_DIM
GROUPS = ((0, 6, 128, 1), (6, 11, 512, 4), (11, 16, 2048, 16))
N_GROUPS = len(GROUPS)
Q_BLOCK = 128
SEC = 384
ROPE_THETA = 10000.0
NEG_INF = -1e30
PAST_LEN = 8192
SGU_GROUPS = 8
SGU_CHUNK = 128
N_EXPERTS = 32
TOP_K = 4
SWIGLU_LIMIT = 7.0
SWIGLU_ALPHA = 1.702
MOE_BLOCK = 256
MOE_TILE = 256
SUB = 8
DEPTH = 4
DEEPNORM_ALPHA = (2 * DEPTH) ** 0.25
LN_EPS = 1e-5
LANES = 128
MIB = 1024 * 1024


def _params(vmem_mib, *semantics):
    return pltpu.CompilerParams(dimension_semantics=semantics, vmem_limit_bytes=vmem_mib * MIB)


def _ln(x, g, b):
    mu = jnp.mean(x, axis=-1, keepdims=True)
    xc = x - mu
    var = jnp.mean(xc * xc, axis=-1, keepdims=True)
    return xc * lax.rsqrt(var + LN_EPS) * g + b


def _qkv_prompt_kernel(x_ref, w_ref, cos_ref, sin_ref, o1_ref, o2_ref, o3_ref, *rest, bm):
    win_refs = rest[:2 * N_GROUPS]
    y_s = rest[-1]
    y = jnp.dot(x_ref[...].astype(BF16), w_ref[...], preferred_element_type=F32)
    cos = cos_ref[...]
    sin = sin_ref[...]
    lane = lax.broadcasted_iota(jnp.int32, (bm, LANES), 1)
    first_half = (lane & (HEAD_DIM // 2)) == 0
    blocks_per_sec = SEC // LANES
    n_rot = 2 * N_GROUPS * blocks_per_sec
    for c in range(3 * N_GROUPS * blocks_per_sec):
        blk = y[:, c * LANES:(c + 1) * LANES]
        if c < n_rot:
            swapped = jnp.where(first_half, pltpu.roll(blk, LANES - HEAD_DIM // 2, 1), pltpu.roll(blk, HEAD_DIM // 2, 1))
            blk = blk * cos + swapped * sin
        if c < n_rot // 2:
            blk = blk * (HEAD_DIM ** -0.5)
        else:
            sec, p = divmod(c - n_rot // 2, blocks_per_sec)
            win_ref = win_refs[sec]
            keep = win_ref.shape[1]
            win_ref[0, :, p * LANES:(p + 1) * LANES] = blk[bm - keep:, :]
        y_s[c] = blk
    for g, o_ref in enumerate((o1_ref, o2_ref, o3_ref)):
        d = GROUPS[g][3]
        n = bm // d
        for r in range(d):
            for part in range(3):
                for p in range(blocks_per_sec):
                    c = (part * N_GROUPS + g) * blocks_per_sec + p
                    rows = y_s[c] if d == 1 else y_s[c, pl.ds(r, n, stride=d), :]
                    o_ref[0, r, :, part * SEC + p * LANES:part * SEC + (p + 1) * LANES] = rows.astype(BF16)


def _qkv_prompt(x_all, w_perm, cos, sin, batch, seq):
    bm = 256
    tiles = seq // bm
    out_shape = [jax.ShapeDtypeStruct((batch, d, seq // d, 3 * SEC), BF16) for (_, _, _, d) in GROUPS]
    out_specs = [pl.BlockSpec((1, d, bm // d, 3 * SEC), lambda b, i: (b, 0, i, 0)) for (_, _, _, d) in GROUPS]
    for _ in range(2):
        for (_, _, win, _) in GROUPS:
            keep = min(win, seq)
            rows = min(keep, bm)
            first = tiles - keep // rows
            out_shape.append(jax.ShapeDtypeStruct((batch, keep, SEC), F32))
            out_specs.append(pl.BlockSpec((1, rows, SEC), lambda b, i, first=first: (b, jnp.maximum(i - first, 0), 0)))
    n_cols = 3 * N_GROUPS * SEC
    return pl.pallas_call(
        functools.partial(_qkv_prompt_kernel, bm=bm),
        grid=(batch, tiles),
        in_specs=[
            pl.BlockSpec((bm, D_MODEL), lambda b, i: (b * tiles + i, 0)),
            pl.BlockSpec((D_MODEL, n_cols), lambda b, i: (0, 0)),
            pl.BlockSpec((bm, LANES), lambda b, i: (i, 0)),
            pl.BlockSpec((bm, LANES), lambda b, i: (i, 0)),
        ],
        out_specs=out_specs,
        out_shape=out_shape,
        scratch_shapes=[pltpu.VMEM((n_cols // LANES, bm, LANES), F32)],
        compiler_params=_params(48, "arbitrary", "arbitrary"),
        name="qkv_prompt",
    )(x_all, w_perm, cos, sin)


def _attn_group_kernel(qkv_ref, o_ref, lse_ref, *, d, m_len, n_heads):
    nb = m_len // Q_BLOCK
    kw = min(2 * Q_BLOCK, m_len)
    lane = lax.broadcasted_iota(jnp.int32, (1, LANES), 1)
    lo = lane < HEAD_DIM
    qi = lax.broadcasted_iota(jnp.int32, (Q_BLOCK, kw), 0)
    kj = lax.broadcasted_iota(jnp.int32, (Q_BLOCK, kw), 1)

    def block(i, carry):
        r = i // nb
        n = i % nb
        ks = pl.multiple_of(jnp.maximum(n - 1, 0) * Q_BLOCK, Q_BLOCK)
        qs = pl.multiple_of(n * Q_BLOCK, Q_BLOCK)
        delta = qs - ks + qi - kj
        valid = (delta >= 0) & (delta <= Q_BLOCK)
        if d == 1:
            dst = pl.ds(qs, Q_BLOCK)
        else:
            dst = pl.ds(r + qs * d, Q_BLOCK, stride=d)
        lses = []
        for p in range(SEC // LANES):
            cols = slice(p * LANES, (p + 1) * LANES)
            q2 = qkv_ref[0, r, pl.ds(qs, Q_BLOCK), cols]
            k2 = qkv_ref[0, r, pl.ds(ks, kw), SEC + p * LANES:SEC + (p + 1) * LANES]
            v2 = qkv_ref[0, r, pl.ds(ks, kw), 2 * SEC + p * LANES:2 * SEC + (p + 1) * LANES]
            o_pair = jnp.zeros((Q_BLOCK, LANES), F32)
            for half in range(2):
                if 2 * p + half >= n_heads:
                    continue
                msk = lo if half == 0 else jnp.logical_not(lo)
                qm = jnp.where(msk, q2, jnp.zeros_like(q2))
                vm = jnp.where(msk, v2, jnp.zeros_like(v2))
                s = lax.dot_general(qm, k2, (((1,), (1,)), ((), ())), preferred_element_type=F32)
                s = jnp.where(valid, s, NEG_INF)
                mx = jnp.max(s, axis=1, keepdims=True)
                pe = jnp.exp(s - mx)
                l = jnp.sum(pe, axis=1, keepdims=True)
                o_pair = o_pair + jnp.dot(pe.astype(BF16), vm, preferred_element_type=F32) / l
                lses.append(mx + jnp.log(l))
            o_ref[0, p, dst, :] = o_pair
        lmax = functools.reduce(jnp.maximum, lses)
        lsum = functools.reduce(lambda a, b: a + b, [jnp.exp(l - lmax) for l in lses])
        glse = lmax + jnp.log(lsum) - math.log(n_heads)
        lse_ref[0, dst, :] = jnp.broadcast_to(glse, (Q_BLOCK, LANES))
        return carry

    lax.fori_loop(0, d * nb, block, 0)


def _attn_group(qkv_g, g, batch, seq):
    h0, h1, _, d = GROUPS[g]
    m_len = seq // d
    return pl.pallas_call(
        functools.partial(_attn_group_kernel, d=d, m_len=m_len, n_heads=h1 - h0),
        grid=(batch,),
        in_specs=[pl.BlockSpec((1, d, m_len, 3 * SEC), lambda b: (b, 0, 0, 0))],
        out_specs=[
            pl.BlockSpec((1, SEC // LANES, seq, LANES), lambda b: (b, 0, 0, 0)),
            pl.BlockSpec((1, seq, LANES), lambda b: (b, 0, 0)),
        ],
        out_shape=[
            jax.ShapeDtypeStruct((batch, SEC // LANES, seq, LANES), F32),
            jax.ShapeDtypeStruct((batch, seq, LANES), F32),
        ],
        compiler_params=_params(40, "arbitrary"),
        name=f"attn_group{g}",
    )(qkv_g)


def _attn_out_kernel(o1_ref, o2_ref, o3_ref, l1_ref, l2_ref, l3_ref, wo_ref, x_ref, g_ref, b_ref, out_ref):
    ls = [l1_ref[0], l2_ref[0], l3_ref[0]]
    mx = jnp.maximum(jnp.maximum(ls[0], ls[1]), ls[2])
    es = [jnp.exp(l - mx) for l in ls]
    inv = float(N_GROUPS) / (es[0] + es[1] + es[2])
    acc = None
    for g, o_ref in enumerate((o1_ref, o2_ref, o3_ref)):
        w = es[g] * inv
        o_g = jnp.concatenate([o_ref[0, p] * w for p in range(SEC // LANES)], axis=1)
        part = jnp.dot(o_g.astype(BF16), wo_ref[g], preferred_element_type=F32)
        acc = part if acc is None else acc + part
    out_ref[...] = _ln(DEEPNORM_ALPHA * x_ref[...] + acc, g_ref[...], b_ref[...])


def _attn_out_prompt(os_, ls_, wo_perm, x_all, ln_g, ln_b, batch, seq):
    bm = 512
    tiles = seq // bm
    t_all = x_all.shape[0]
    row = lambda b, i: (b * tiles + i, 0)
    const2 = lambda b, i: (0, 0)
    return pl.pallas_call(
        _attn_out_kernel,
        grid=(batch, tiles),
        in_specs=[pl.BlockSpec((1, SEC // LANES, bm, LANES), lambda b, i: (b, 0, i, 0))] * 3
        + [pl.BlockSpec((1, bm, LANES), lambda b, i: (b, i, 0))] * 3 + [
            pl.BlockSpec((N_GROUPS, SEC, D_MODEL), lambda b, i: (0, 0, 0)),
            pl.BlockSpec((bm, D_MODEL), row),
            pl.BlockSpec((1, D_MODEL), const2),
            pl.BlockSpec((1, D_MODEL), const2),
        ],
        out_specs=pl.BlockSpec((bm, D_MODEL), row),
        out_shape=jax.ShapeDtypeStruct((t_all, D_MODEL), F32),
        compiler_params=_params(40, "arbitrary", "arbitrary"),
        name="attn_out_prompt",
    )(*os_, *ls_, wo_perm, x_all, ln_g, ln_b)


def _mm_kernel(x_ref, w_ref, o_ref):
    o_ref[...] = jnp.dot(x_ref[...].astype(BF16), w_ref[...], preferred_element_type=F32)


def _mm_rows(x_all, w, row0, rows):
    n = w.shape[1]
    blk0 = row0 // rows
    return pl.pallas_call(
        _mm_kernel,
        grid=(1,),
        in_specs=[
            pl.BlockSpec((rows, D_MODEL), lambda i: (blk0, 0)),
            pl.BlockSpec((D_MODEL, n), lambda i: (0, 0)),
        ],
        out_specs=pl.BlockSpec((rows, n), lambda i: (0, 0)),
        out_shape=jax.ShapeDtypeStruct((rows, n), F32),
        compiler_params=_params(40, "arbitrary"),
        name="mm_rows",
    )(x_all, w)


def _proj_ln_kernel(a_ref, w_ref, x_ref, g_ref, b_ref, prev_ref, out_ref):
    del prev_ref
    acc = jnp.dot(a_ref[...].astype(BF16), w_ref[...], preferred_element_type=F32)
    out_ref[...] = _ln(DEEPNORM_ALPHA * x_ref[...] + acc, g_ref[...], b_ref[...])


def _proj_ln_rows(a, w, x_all, ln_g, ln_b, x_next, row0):
    rows = a.shape[0]
    blk0 = row0 // rows
    const2 = lambda i: (0, 0)
    return pl.pallas_call(
        _proj_ln_kernel,
        grid=(1,),
        in_specs=[
            pl.BlockSpec((rows, a.shape[1]), const2),
            pl.BlockSpec(w.shape, const2),
            pl.BlockSpec((rows, D_MODEL), lambda i: (blk0, 0)),
            pl.BlockSpec((1, D_MODEL), const2),
            pl.BlockSpec((1, D_MODEL), const2),
            pl.BlockSpec(memory_space=pl.ANY),
        ],
        out_specs=pl.BlockSpec((rows, D_MODEL), lambda i: (blk0, 0)),
        out_shape=jax.ShapeDtypeStruct(x_next.shape, F32),
        input_output_aliases={5: 0},
        compiler_params=_params(40, "arbitrary"),
        name="proj_ln_rows",
    )(a, w, x_all, ln_g, ln_b, x_next)


def _sgu_kernel(*refs, bm, chunk, emit_v, aliased):
    (x_ref, win_ref, bin_ref, lng_ref, lnb_ref, ws_ref, bs_ref, wout_ref, g1_ref, b1_ref) = refs[:10]
    rest = refs[10 + (1 if aliased else 0):]
    out_ref = rest[0]
    v_ref = rest[1] if emit_v else None
    ug_s = rest[-1]
    width = D_MODEL
    x = x_ref[...]
    z = jnp.dot(x.astype(BF16), win_ref[...], preferred_element_type=F32) + bin_ref[...]
    z = 0.5 * z * (1.0 + lax.erf(z * (2.0 ** -0.5)))
    v = _ln(z[:, width:], lng_ref[...], lnb_ref[...])
    if emit_v:
        v_ref[...] = v
    ii = lax.broadcasted_iota(jnp.int32, (SGU_CHUNK, SGU_CHUNK), 0)
    jj = lax.broadcasted_iota(jnp.int32, (SGU_CHUNK, SGU_CHUNK), 1)
    causal = (jj <= ii) & ((ii // chunk) == (jj // chunk))
    gw = width // SGU_GROUPS
    for gi in range(SGU_GROUPS):
        wsm = jnp.where(causal, ws_ref[gi], 0.0).astype(BF16)
        for c in range(bm // SGU_CHUNK):
            rows = slice(c * SGU_CHUNK, (c + 1) * SGU_CHUNK)
            cols = slice(gi * gw, (gi + 1) * gw)
            gate = jnp.dot(wsm, v[rows, cols].astype(BF16), preferred_element_type=F32) + bs_ref[gi]
            ug_s[rows, cols] = (z[rows, cols] * gate).astype(BF16)
    y = jnp.dot(ug_s[...], wout_ref[...], preferred_element_type=F32)
    out_ref[...] = _ln(DEEPNORM_ALPHA * x + y, g1_ref[...], b1_ref[...])


def _sgu(x_all, x_next, weights, row0, rows, bm, chunk, emit_v):
    w_in, b_in, ln_g, ln_b, ws, bs, w_out, g1, b1 = weights
    t_all = x_all.shape[0]
    blk0 = row0 // bm
    row = lambda i: (blk0 + i, 0)
    const2 = lambda i: (0, 0)
    const3 = lambda i: (0, 0, 0)
    aliased = x_next is not None
    in_specs = [
        pl.BlockSpec((bm, D_MODEL), row),
        pl.BlockSpec(w_in.shape, const2),
        pl.BlockSpec((1, 2 * D_MODEL), const2),
        pl.BlockSpec((1, D_MODEL), const2),
        pl.BlockSpec((1, D_MODEL), const2),
        pl.BlockSpec(ws.shape, const3),
        pl.BlockSpec(bs.shape, const3),
        pl.BlockSpec(w_out.shape, const2),
        pl.BlockSpec((1, D_MODEL), const2),
        pl.BlockSpec((1, D_MODEL), const2),
    ]
    args = [x_all, w_in, b_in, ln_g, ln_b, ws, bs, w_out, g1, b1]
    aliases = {}
    if aliased:
        in_specs.append(pl.BlockSpec(memory_space=pl.ANY))
        args.append(x_next)
        aliases = {10: 0}
    out_specs = [pl.BlockSpec((bm, D_MODEL), row)]
    out_shape = [jax.ShapeDtypeStruct((t_all, D_MODEL), F32)]
    if emit_v:
        out_specs.append(pl.BlockSpec((bm, D_MODEL), lambda i: (i, 0)))
        out_shape.append(jax.ShapeDtypeStruct((rows, D_MODEL), F32))
    return pl.pallas_call(
        functools.partial(_sgu_kernel, bm=bm, chunk=chunk, emit_v=emit_v, aliased=aliased),
        grid=(rows // bm,),
        in_specs=in_specs,
        out_specs=out_specs,
        out_shape=out_shape,
        scratch_shapes=[pltpu.VMEM((bm, D_MODEL), BF16)],
        input_output_aliases=aliases,
        compiler_params=_params(48, "arbitrary"),
        name="sgu",
    )(*args)


def _router_kernel(x_ref, wr_ref, br_ref, idx_ref, gate_ref, cnt_ref, *, bm, t_all):
    x = x_ref[...]
    w = wr_ref[...]
    x_hi = x.astype(BF16)
    w_hi = w.astype(BF16)
    x_lo = (x - x_hi.astype(F32)).astype(BF16)
    w_lo = (w - w_hi.astype(F32)).astype(BF16)
    logits = (jnp.dot(x_hi, w_hi, preferred_element_type=F32) + jnp.dot(x_lo, w_hi, preferred_element_type=F32)
              + jnp.dot(x_hi, w_lo, preferred_element_type=F32))
    logits = logits + br_ref[...]
    lane = lax.broadcasted_iota(jnp.int32, (bm, LANES), 1)
    lane_f = lane.astype(F32)
    row = pl.program_id(0) * bm + lax.broadcasted_iota(jnp.int32, (bm, LANES), 0)
    logits = jnp.where(row < t_all, logits, 0.0)
    logits = jnp.where(lane < N_EXPERTS, logits, -jnp.inf)
    vals, idxs = [], []
    for _ in range(TOP_K):
        m = jnp.max(logits, axis=1, keepdims=True)
        i = jnp.min(jnp.where(logits == m, lane_f, float(LANES)), axis=1, keepdims=True)
        vals.append(m)
        idxs.append(i)
        logits = jnp.where(lane_f == i, -jnp.inf, logits)
    es = [jnp.exp(v - vals[0]) for v in vals]
    inv = 1.0 / functools.reduce(lambda a, b: a + b, es)
    idx_out = jnp.zeros((bm, LANES), F32)
    gate_out = jnp.zeros((bm, LANES), F32)
    chosen = jnp.zeros((bm, LANES), F32)
    for k in range(TOP_K):
        idx_out = jnp.where(lane == k, idxs[k], idx_out)
        gate_out = jnp.where(lane == k, es[k] * inv, gate_out)
        chosen = chosen + jnp.where((lane_f == idxs[k]) & (row < t_all), 1.0, 0.0)
    idx_ref[...] = idx_out.astype(jnp.int32)
    gate_ref[...] = gate_out
    cnt_ref[0] = jnp.broadcast_to(jnp.sum(chosen, axis=0, keepdims=True), (SUB, LANES))


def _router(x_all, w_r, b_r):
    bm = MOE_TILE
    t_all = x_all.shape[0]
    n_tiles = pl.cdiv(t_all, bm)
    row = lambda i: (i, 0)
    const2 = lambda i: (0, 0)
    return pl.pallas_call(
        functools.partial(_router_kernel, bm=bm, t_all=t_all),
        grid=(n_tiles,),
        in_specs=[
            pl.BlockSpec((bm, D_MODEL), row),
            pl.BlockSpec((D_MODEL, LANES), const2),
            pl.BlockSpec((1, LANES), const2),
        ],
        out_specs=[pl.BlockSpec((bm, LANES), row), pl.BlockSpec((bm, LANES), row),
                   pl.BlockSpec((1, SUB, LANES), lambda i: (i, 0, 0))],
        out_shape=[
            jax.ShapeDtypeStruct((t_all, LANES), jnp.int32),
            jax.ShapeDtypeStruct((t_all, LANES), F32),
            jax.ShapeDtypeStruct((n_tiles, SUB, LANES), F32),
        ],
        compiler_params=_params(32, "arbitrary"),
        name="router",
    )(x_all, w_r, b_r)


def _local_slots(idx, row_valid):
    tile = idx.shape[0]
    lane = lax.broadcasted_iota(jnp.int32, (tile, LANES), 1)
    idx = jnp.where(row_valid, idx, -1)
    picks = [lane == idx[:, k:k + 1] for k in range(TOP_K)]
    chosen = functools.reduce(lambda a, b: a + b, [p.astype(F32) for p in picks])
    earlier = (lax.broadcasted_iota(jnp.int32, (tile, tile), 1)
               < lax.broadcasted_iota(jnp.int32, (tile, tile), 0)).astype(BF16)
    rank = jnp.dot(earlier, chosen.astype(BF16), preferred_element_type=F32)
    per_expert = jnp.broadcast_to(jnp.sum(chosen, axis=0, keepdims=True), (SUB, LANES))
    lower = (lax.broadcasted_iota(jnp.int32, (LANES, LANES), 0)
             < lax.broadcasted_iota(jnp.int32, (LANES, LANES), 1)).astype(BF16)
    base = jnp.dot(per_expert.astype(BF16), lower, preferred_element_type=F32)[0:1]
    place = base + rank + 1.0
    return [jnp.sum(jnp.where(p, place, 0.0), axis=1, keepdims=True) - 1.0 for p in picks]


def _segment_copies(cnt_ref, off_ref, j, hbm, buf, sem, to_hbm):
    local = 0
    for e in range(N_EXPERTS):
        cnt = cnt_ref[j * N_EXPERTS + e]
        off = off_ref[j * N_EXPERTS + e]
        for bit in range(MOE_TILE.bit_length()):
            size = 1 << bit

            @pl.when((cnt & size) != 0)
            def _(cnt=cnt, off=off, local=local, size=size):
                done = cnt & (size - 1)
                a = buf.at[pl.ds(pl.multiple_of((local + done) * SUB, SUB), size * SUB), :]
                b = hbm.at[pl.ds(pl.multiple_of((off + done) * SUB, SUB), size * SUB), :]
                (pltpu.make_async_copy(a, b, sem) if to_hbm else pltpu.make_async_copy(b, a, sem)).start()

        local = local + cnt


def _rows_to_tiles(ref, value):
    rows = value.shape[0]
    for s in range(D_MODEL // LANES):
        ref[pl.ds(s, rows, stride=SUB), :] = value[:, s * LANES:(s + 1) * LANES]


def _tiles_to_rows(ref, rows):
    return jnp.concatenate([ref[pl.ds(s, rows, stride=SUB), :] for s in range(D_MODEL // LANES)], axis=1)


def _dispatch_kernel(cnt_ref, off_ref, x_ref, idx_ref, xs_hbm, slot_ref, buf, sem, *, t_all, n_tiles):
    j = pl.program_id(0)
    par = j % 2
    tile = MOE_TILE
    row = j * tile + lax.broadcasted_iota(jnp.int32, (tile, LANES), 0)
    row_valid = row < t_all
    slots = _local_slots(idx_ref[...], row_valid)
    lane = lax.broadcasted_iota(jnp.int32, (tile, LANES), 1)
    slot_out = jnp.zeros((tile, LANES), F32)
    for k in range(TOP_K):
        slot_out = jnp.where(lane == k, slots[k], slot_out)
    slot_ref[...] = slot_out
    slots_t = slot_out.T
    pos = lax.broadcasted_iota(jnp.int32, (TOP_K * tile, tile), 0).astype(F32)
    hit = functools.reduce(jnp.logical_or, [pos == slots_t[k:k + 1, :] for k in range(TOP_K)])
    place = jnp.where(hit, 1.0, 0.0).astype(BF16)
    in_range = (j * tile + lax.broadcasted_iota(jnp.int32, (tile, D_MODEL), 0)) < t_all
    x = jnp.where(in_range, x_ref[...], 0.0).astype(BF16)
    ordered = jnp.dot(place, x, preferred_element_type=F32)
    _rows_to_tiles(buf.at[par], ordered)

    full = TOP_K * tile * SUB
    last = TOP_K * (t_all - (n_tiles - 1) * tile) * SUB

    @pl.when(j > 0)
    def _():
        pltpu.make_async_copy(buf.at[1 - par], xs_hbm.at[pl.ds(0, full), :], sem.at[0]).wait()

    _segment_copies(cnt_ref, off_ref, j, xs_hbm, buf.at[par], sem.at[0], True)

    @pl.when(j == n_tiles - 1)
    def _():
        pltpu.make_async_copy(buf.at[par, pl.ds(0, last), :], xs_hbm.at[pl.ds(0, last), :], sem.at[0]).wait()


def _dispatch(cnt, off, x_all, idx, n_slots):
    t_all = x_all.shape[0]
    n_tiles = pl.cdiv(t_all, MOE_TILE)
    row = lambda j, c, o: (j, 0)
    grid_spec = pltpu.PrefetchScalarGridSpec(
        num_scalar_prefetch=2,
        grid=(n_tiles,),
        in_specs=[pl.BlockSpec((MOE_TILE, D_MODEL), row), pl.BlockSpec((MOE_TILE, LANES), row)],
        out_specs=[pl.BlockSpec(memory_space=pl.ANY), pl.BlockSpec((MOE_TILE, LANES), row)],
        scratch_shapes=[pltpu.VMEM((2, TOP_K * MOE_TILE * SUB, LANES), F32), pltpu.SemaphoreType.DMA((1,))],
    )
    return pl.pallas_call(
        functools.partial(_dispatch_kernel, t_all=t_all, n_tiles=n_tiles),
        grid_spec=grid_spec,
        out_shape=[jax.ShapeDtypeStruct((n_slots * SUB, LANES), F32), jax.ShapeDtypeStruct((t_all, LANES), F32)],
        compiler_params=_params(40, "arbitrary"),
        name="moe_dispatch",
    )(cnt, off, x_all, idx)


def _expert_kernel(be_ref, nreal_ref, xs_ref, wgu_ref, bgu_ref, wdn_ref, bdn_ref, ys_ref, wgu_bf, wdn_bf):
    i = pl.program_id(0)
    ff = wdn_ref.shape[1]

    @pl.when(i < nreal_ref[0])
    def _():
        @pl.when((i == 0) | (be_ref[i] != be_ref[jnp.maximum(i - 1, 0)]))
        def _():
            wgu_bf[...] = wgu_ref[0].astype(BF16)
            wdn_bf[...] = wdn_ref[0].astype(BF16)

        x = _tiles_to_rows(xs_ref, MOE_BLOCK).astype(BF16)
        h = jnp.dot(x, wgu_bf[...], preferred_element_type=F32) + bgu_ref[0]
        gate = jnp.minimum(h[:, :ff], SWIGLU_LIMIT)
        up = jnp.clip(h[:, ff:], -SWIGLU_LIMIT, SWIGLU_LIMIT)
        act = (up + 1.0) * gate * jax.nn.sigmoid(SWIGLU_ALPHA * gate)
        y = jnp.dot(act.astype(BF16), wdn_bf[...], preferred_element_type=F32) + bdn_ref[0]
        _rows_to_tiles(ys_ref, y)


def _experts(block_expert, n_real, xs, w_gu, b_gu, w_dn, b_dn, layer):
    n_blocks = xs.shape[0] // (MOE_BLOCK * SUB)
    ff = w_dn.shape[2]
    rows = lambda i, be, nr: (jnp.minimum(i, nr[0] - 1), 0)
    by_expert = lambda i, be, nr: (layer, be[i], 0, 0)
    grid_spec = pltpu.PrefetchScalarGridSpec(
        num_scalar_prefetch=2,
        grid=(n_blocks,),
        in_specs=[
            pl.BlockSpec((MOE_BLOCK * SUB, LANES), rows),
            pl.BlockSpec((None, 1, D_MODEL, 2 * ff), by_expert),
            pl.BlockSpec((None, 1, 1, 2 * ff), by_expert),
            pl.BlockSpec((None, 1, ff, D_MODEL), by_expert),
            pl.BlockSpec((None, 1, 1, D_MODEL), by_expert),
        ],
        out_specs=pl.BlockSpec((MOE_BLOCK * SUB, LANES), rows),
        scratch_shapes=[pltpu.VMEM((D_MODEL, 2 * ff), BF16), pltpu.VMEM((ff, D_MODEL), BF16)],
    )
    return pl.pallas_call(
        _expert_kernel,
        grid_spec=grid_spec,
        out_shape=jax.ShapeDtypeStruct(xs.shape, F32),
        compiler_params=_params(52, "arbitrary"),
        name="experts",
    )(block_expert, n_real, xs, w_gu, b_gu, w_dn, b_dn)


def _combine_kernel(cnt_ref, off_ref, x_ref, slot_ref, gate_ref, g_ref, b_ref, ys_hbm, out_ref, buf, sem, *, t_all, n_tiles):
    j = pl.program_id(0)
    par = j % 2
    tile = MOE_TILE
    full = TOP_K * tile * SUB
    last = TOP_K * (t_all - (n_tiles - 1) * tile) * SUB

    @pl.when(j == 0)
    def _():
        _segment_copies(cnt_ref, off_ref, 0, ys_hbm, buf.at[0], sem.at[0], False)

    @pl.when(j + 1 < n_tiles)
    def _():
        _segment_copies(cnt_ref, off_ref, j + 1, ys_hbm, buf.at[1 - par], sem.at[1 - par], False)

    @pl.when(j < n_tiles - 1)
    def _():
        pltpu.make_async_copy(ys_hbm.at[pl.ds(0, full), :], buf.at[par], sem.at[par]).wait()

    @pl.when(j == n_tiles - 1)
    def _():
        pltpu.make_async_copy(ys_hbm.at[pl.ds(0, last), :], buf.at[par, pl.ds(0, last), :], sem.at[par]).wait()

    n_valid = TOP_K * jnp.minimum(tile, t_all - j * tile)
    ys = _tiles_to_rows(buf.at[par], TOP_K * tile)
    ys = jnp.where(lax.broadcasted_iota(jnp.int32, ys.shape, 0) < n_valid, ys, 0.0).astype(BF16)
    slots = slot_ref[...]
    gates = gate_ref[...]
    pos = lax.broadcasted_iota(jnp.int32, (tile, TOP_K * tile), 1).astype(F32)
    weights = jnp.zeros((tile, TOP_K * tile), F32)
    for k in range(TOP_K):
        weights = jnp.where(pos == slots[:, k:k + 1], gates[:, k:k + 1], weights)
    y = jnp.dot(weights.astype(BF16), ys, preferred_element_type=F32)
    out_ref[...] = _ln(DEEPNORM_ALPHA * x_ref[...] + y, g_ref[...], b_ref[...])


def _combine(cnt, off, x_all, slots, gates, ln_g, ln_b, ys):
    t_all = x_all.shape[0]
    n_tiles = pl.cdiv(t_all, MOE_TILE)
    row = lambda j, c, o: (j, 0)
    const2 = lambda j, c, o: (0, 0)
    grid_spec = pltpu.PrefetchScalarGridSpec(
        num_scalar_prefetch=2,
        grid=(n_tiles,),
        in_specs=[
            pl.BlockSpec((MOE_TILE, D_MODEL), row),
            pl.BlockSpec((MOE_TILE, LANES), row),
            pl.BlockSpec((MOE_TILE, LANES), row),
            pl.BlockSpec((1, D_MODEL), const2),
            pl.BlockSpec((1, D_MODEL), const2),
            pl.BlockSpec(memory_space=pl.ANY),
        ],
        out_specs=pl.BlockSpec((MOE_TILE, D_MODEL), row),
        scratch_shapes=[pltpu.VMEM((2, TOP_K * MOE_TILE * SUB, LANES), F32), pltpu.SemaphoreType.DMA((2,))],
    )
    return pl.pallas_call(
        functools.partial(_combine_kernel, t_all=t_all, n_tiles=n_tiles),
        grid_spec=grid_spec,
        out_shape=jax.ShapeDtypeStruct((t_all, D_MODEL), F32),
        compiler_params=_params(40, "arbitrary"),
        name="moe_combine",
    )(cnt, off, x_all, slots, gates, ln_g, ln_b, ys)


def _moe(x_all, w_r, b_r, w_gu, b_gu, w_dn, b_dn, ln_g, ln_b, layer):
    t_all = x_all.shape[0]
    idx, gates, tile_counts = _router(x_all, w_r, b_r)
    cnt = tile_counts[:, 0, :N_EXPERTS].astype(jnp.int32)
    counts = jnp.sum(cnt, axis=0)
    padded = (counts + MOE_BLOCK - 1) // MOE_BLOCK * MOE_BLOCK
    pad_end = jnp.cumsum(padded)
    pad_start = pad_end - padded
    off = pad_start[None, :] + jnp.cumsum(cnt, axis=0) - cnt
    n_blocks = -(-t_all * TOP_K // MOE_BLOCK) + N_EXPERTS
    block_start = jnp.arange(n_blocks, dtype=jnp.int32) * MOE_BLOCK
    block_expert = jnp.minimum(
        jnp.sum((pad_end[None, :] <= block_start[:, None]).astype(jnp.int32), axis=1), N_EXPERTS - 1)
    n_real = (pad_end[-1:] // MOE_BLOCK).astype(jnp.int32)
    cnt_flat, off_flat = cnt.reshape(-1), off.reshape(-1).astype(jnp.int32)
    xs, slots = _dispatch(cnt_flat, off_flat, x_all, idx, n_blocks * MOE_BLOCK)
    ys = _experts(block_expert, n_real, xs, w_gu, b_gu, w_dn, b_dn, layer)
    return _combine(cnt_flat, off_flat, x_all, slots, gates, ln_g, ln_b, ys)


def _rope_tables(pos):
    half = HEAD_DIM // 2
    inv_freq = ROPE_THETA ** (-jnp.arange(half, dtype=F32) / half)
    ang = pos.astype(F32)[:, None] * inv_freq[None, :]
    cos = jnp.cos(ang)
    sin = jnp.sin(ang)
    cos = jnp.concatenate([cos, cos], axis=1)
    sin = jnp.concatenate([-sin, sin], axis=1)
    reps = LANES // HEAD_DIM
    return jnp.tile(cos, (1, reps)), jnp.tile(sin, (1, reps))


def _permute_qkv_weight(w_qkv):
    w3 = w_qkv.reshape(D_MODEL, 3, N_HEADS * HEAD_DIM)
    secs = []
    for part in range(3):
        for (h0, h1, _, _) in GROUPS:
            sec = w3[:, part, h0 * HEAD_DIM:h1 * HEAD_DIM]
            secs.append(jnp.pad(sec, ((0, 0), (0, SEC - sec.shape[1]))))
    return jnp.concatenate(secs, axis=1).astype(BF16)


def _permute_out_weight(w_o):
    secs = []
    for (h0, h1, _, _) in GROUPS:
        sec = w_o[h0 * HEAD_DIM:h1 * HEAD_DIM]
        secs.append(jnp.pad(sec, ((0, SEC - sec.shape[0]), (0, 0))))
    return jnp.stack(secs).astype(BF16)


def _sample_step_kernel(*refs, dec_s, aliased):
    y_ref, cos_ref, sin_ref = refs[:3]
    cache_refs = refs[3:3 + N_GROUPS]
    outs = refs[3 + N_GROUPS * (2 if aliased else 1):]
    o_ref, new_refs = outs[0], outs[1:]
    y = y_ref[0]
    cos = cos_ref[...]
    sin = sin_ref[...]
    lane = lax.broadcasted_iota(jnp.int32, (SUB, LANES), 1)
    first_half = (lane & (HEAD_DIM // 2)) == 0
    blocks_per_sec = SEC // LANES
    n_rot = 2 * N_GROUPS * blocks_per_sec
    blocks = []
    for c in range(3 * N_GROUPS * blocks_per_sec):
        blk = y[:, c * LANES:(c + 1) * LANES]
        if c < n_rot:
            swapped = jnp.where(first_half, pltpu.roll(blk, LANES - HEAD_DIM // 2, 1), pltpu.roll(blk, HEAD_DIM // 2, 1))
            blk = blk * cos + swapped * sin
        if c < n_rot // 2:
            blk = blk * (HEAD_DIM ** -0.5)
        blocks.append(blk)

    lo = lane < HEAD_DIM
    new_row = lax.broadcasted_iota(jnp.int32, (SUB, SUB), 1)
    qry_row = lax.broadcasted_iota(jnp.int32, (SUB, SUB), 0)
    pad_rows = jnp.zeros((LANES - SUB, LANES), F32)
    contract_lanes = (((1,), (1,)), ((), ()))
    o_blocks, glses = [], []
    for g in range(N_GROUPS):
        h0, h1, win, d = GROUPS[g]
        n_heads = h1 - h0
        cache_ref, new_ref = cache_refs[g], new_refs[g]
        lb = cache_ref.shape[-1]
        key = lax.broadcasted_iota(jnp.int32, (SUB, lb), 1)
        dist = lb + lax.broadcasted_iota(jnp.int32, (SUB, lb), 0) - key
        visible = ((dist & (d - 1)) == 0) & (dist <= win)
        dist_new = qry_row - new_row
        visible_new = (dist_new >= 0) & ((dist_new & (d - 1)) == 0) & (new_row < dec_s)
        tail_lane = lax.broadcasted_iota(jnp.int32, (HEAD_DIM, LANES), 1)
        lses = []
        for p in range(blocks_per_sec):
            heads = [h for h in (2 * p, 2 * p + 1) if h < n_heads]
            if not heads:
                o_blocks.append(jnp.zeros((SUB, LANES), F32))
                continue
            q2 = blocks[g * blocks_per_sec + p]
            k_new = blocks[(N_GROUPS + g) * blocks_per_sec + p]
            v_new = blocks[(2 * N_GROUPS + g) * blocks_per_sec + p]
            slabs = [[cache_ref[part, h] for h in heads] for part in range(2)]
            zero_slab = jnp.zeros((HEAD_DIM, lb), F32)
            kt2 = jnp.concatenate(slabs[0] + [zero_slab] * (2 - len(heads)), axis=0).astype(BF16)
            vt2 = jnp.concatenate(slabs[1] + [zero_slab] * (2 - len(heads)), axis=0).astype(BF16)
            q_both = jnp.concatenate([jnp.where(lo, q2, 0.0), jnp.where(lo, 0.0, q2)], axis=0).astype(BF16)
            s_both = jnp.dot(q_both, kt2, preferred_element_type=F32)
            s_new_both = lax.dot_general(q_both, k_new.astype(BF16), contract_lanes, preferred_element_type=F32)
            probs, probs_new = [], []
            for half in range(2):
                s = jnp.where(visible, s_both[half * SUB:(half + 1) * SUB], NEG_INF)
                s_new = jnp.where(visible_new, s_new_both[half * SUB:(half + 1) * SUB], NEG_INF)
                mx = jnp.maximum(jnp.max(s, axis=1, keepdims=True), jnp.max(s_new, axis=1, keepdims=True))
                pe = jnp.exp(s - mx)
                pe_new = jnp.exp(s_new - mx)
                l = jnp.sum(pe, axis=1, keepdims=True) + jnp.sum(pe_new, axis=1, keepdims=True)
                probs.append(pe / l)
                probs_new.append(pe_new / l)
                if half < len(heads):
                    lses.append(mx + jnp.log(l))
            pv = lax.dot_general(jnp.concatenate(probs, axis=0).astype(BF16), vt2, contract_lanes, preferred_element_type=F32)
            pv_new = jnp.dot(jnp.concatenate(probs_new, axis=0).astype(BF16), v_new.astype(BF16), preferred_element_type=F32)
            both = pv + pv_new
            o_blocks.append(jnp.where(lo, both[:SUB], both[SUB:]))
            for part, new in enumerate((k_new, v_new)):
                new_t = jnp.concatenate([new, pad_rows], axis=0).T
                for i, h in enumerate(heads):
                    shifted = pltpu.roll(slabs[part][i], lb - dec_s, 1)
                    tail = shifted[:, lb - LANES:]
                    feats = new_t[i * HEAD_DIM:(i + 1) * HEAD_DIM]
                    for jj in range(dec_s):
                        tail = jnp.where(tail_lane == LANES - dec_s + jj, feats[:, jj:jj + 1], tail)
                    if lb > LANES:
                        new_ref[part, h, :, :lb - LANES] = shifted[:, :lb - LANES]
                    new_ref[part, h, :, lb - LANES:] = tail
        lmax = functools.reduce(jnp.maximum, lses)
        lsum = functools.reduce(lambda a, b: a + b, [jnp.exp(l - lmax) for l in lses])
        glses.append(lmax + jnp.log(lsum) - math.log(n_heads))
    gmax = functools.reduce(jnp.maximum, glses)
    es = [jnp.exp(l - gmax) for l in glses]
    inv = float(N_GROUPS) / functools.reduce(lambda a, b: a + b, es)
    scaled = [o_blocks[g * blocks_per_sec + p] * (es[g] * inv) for g in range(N_GROUPS) for p in range(blocks_per_sec)]
    o_ref[0] = jnp.concatenate(scaled, axis=1)


def _sample_step(y_s, caches_t, new_caches_t, layer, cos_s, sin_s, dec_b, dec_s):
    n_cols = 3 * N_GROUPS * SEC
    y_pad = jnp.pad(y_s.reshape(dec_b, dec_s, n_cols), ((0, 0), (0, SUB - dec_s), (0, 0)))
    aliased = new_caches_t is not None
    cache_specs = []
    for (h0, h1, win, d), cache in zip(GROUPS, caches_t):
        assert cache.shape[-1] == win and dec_s <= LANES and win % LANES == 0
        cache_specs.append(pl.BlockSpec((None, None) + cache.shape[2:], lambda b: (layer, b, 0, 0, 0, 0)))
    in_specs = [
        pl.BlockSpec((1, SUB, n_cols), lambda b: (b, 0, 0)),
        pl.BlockSpec((SUB, LANES), lambda b: (0, 0)),
        pl.BlockSpec((SUB, LANES), lambda b: (0, 0)),
    ] + cache_specs
    args = [y_pad, cos_s, sin_s, *caches_t]
    aliases = {}
    if aliased:
        in_specs += [pl.BlockSpec(memory_space=pl.ANY)] * N_GROUPS
        aliases = {len(args) + g: 1 + g for g in range(N_GROUPS)}
        args += list(new_caches_t)
    outs = pl.pallas_call(
        functools.partial(_sample_step_kernel, dec_s=dec_s, aliased=aliased),
        grid=(dec_b,),
        in_specs=in_specs,
        out_specs=[pl.BlockSpec((1, SUB, N_GROUPS * SEC), lambda b: (b, 0, 0))] + cache_specs,
        out_shape=[jax.ShapeDtypeStruct((dec_b, SUB, N_GROUPS * SEC), F32)]
        + [jax.ShapeDtypeStruct(c.shape, F32) for c in caches_t],
        input_output_aliases=aliases,
        compiler_params=_params(52, "arbitrary"),
        name="sample_step",
    )(*args)
    return outs[0][:, :dec_s].reshape(dec_b * dec_s, N_GROUPS * SEC), outs[1:]


def _row2(a):
    return a.reshape(1, -1)


def _attn_layer(x_all, caches_t, new_caches_t, layer, w_qkv, w_o, g1, b1, cos_p, sin_p, dims):
    batch, seq, dec_b, dec_s = dims
    t_prompt = batch * seq
    t_sample = dec_b * dec_s
    w_perm = _permute_qkv_weight(w_qkv)
    wo_perm = _permute_out_weight(w_o)
    outs = _qkv_prompt(x_all, w_perm, cos_p, sin_p, batch, seq)
    qkv_groups, windows = outs[:N_GROUPS], outs[N_GROUPS:]
    os_, ls_ = [], []
    for g in range(N_GROUPS):
        o_g, l_g = _attn_group(qkv_groups[g], g, batch, seq)
        os_.append(o_g)
        ls_.append(l_g)
    x_next = _attn_out_prompt(os_, ls_, wo_perm, x_all, g1, b1, batch, seq)
    rows_p = []
    for g, (h0, h1, _, _) in enumerate(GROUPS):
        hg = h1 - h0
        halves = [windows[part * N_GROUPS + g][:, :, :hg * HEAD_DIM].reshape(batch, -1, hg, HEAD_DIM) for part in range(2)]
        rows_p.append(jnp.stack(halves, axis=2))
    y_s = _mm_rows(x_all, w_perm, t_prompt, t_sample)
    cos_s, sin_s = _rope_tables(PAST_LEN + jnp.arange(SUB, dtype=jnp.int32))
    o_s, new_caches_t = _sample_step(y_s, caches_t, new_caches_t, layer, cos_s, sin_s, dec_b, dec_s)
    x_all = _proj_ln_rows(o_s, wo_perm.reshape(N_GROUPS * SEC, D_MODEL), x_all, g1, b1, x_next, t_prompt)
    return x_all, rows_p, new_caches_t


def _sgu_layer(x_all, w_in, b_in, ln_g, ln_b, ws, bs, w_out, g1, b1, dims):
    batch, seq, dec_b, dec_s = dims
    t_prompt = batch * seq
    t_sample = dec_b * dec_s
    common = (w_in.astype(BF16), _row2(b_in), _row2(ln_g), _row2(ln_b))
    tail = (w_out.astype(BF16), g1, b1)
    bs_p = jnp.broadcast_to(bs[:, :, None], (SGU_GROUPS, SGU_CHUNK, SGU_CHUNK))
    x_next = _sgu(x_all, None, common + (ws, bs_p) + tail, 0, t_prompt, 256, SGU_CHUNK, False)[0]
    c = min(SGU_CHUNK, dec_s)
    reps = SGU_CHUNK // c
    ws_s = jnp.tile(ws[:, :c, :c], (1, reps, reps))
    bs_s = jnp.broadcast_to(jnp.tile(bs[:, :c], (1, reps))[:, :, None], (SGU_GROUPS, SGU_CHUNK, SGU_CHUNK))
    x_all, v_new = _sgu(x_all, x_next, common + (ws_s, bs_s) + tail, t_prompt, t_sample, t_sample, c, True)
    return x_all, v_new.reshape(dec_b, dec_s, D_MODEL)


def _moe_layer(x_all, w_router, b_router, w_gu, b_gu, w_dn, b_dn, ln_g, ln_b, layer):
    w_r = jnp.pad(w_router, ((0, 0), (0, LANES - N_EXPERTS)))
    b_r = jnp.pad(b_router, (0, LANES - N_EXPERTS)).reshape(1, LANES)
    n_layers = w_gu.shape[0]
    return _moe(
        x_all, w_r, b_r,
        w_gu, b_gu.reshape(n_layers, N_EXPERTS, 1, -1),
        w_dn, b_dn.reshape(n_layers, N_EXPERTS, 1, -1),
        ln_g, ln_b, layer)


def kernel(x_prompt, x_sample, cache_kv_w128, cache_kv_w512, cache_kv_w2048, attn_w_qkv, attn_w_o, sgu_w_in, sgu_b_in, sgu_ln_g, sgu_ln_b, sgu_w_s, sgu_b_s, sgu_w_out, moe_w_router, moe_b_router, moe_w_gu, moe_b_gu, moe_w_down, moe_b_down, ln1_g, ln1_b, ln2_g, ln2_b):
    batch, seq, _ = x_prompt.shape
    dec_b, dec_s, _ = x_sample.shape
    t_prompt = batch * seq
    t_sample = dec_b * dec_s
    caches = (cache_kv_w128, cache_kv_w512, cache_kv_w2048)
    x_all = jnp.concatenate([x_prompt.reshape(t_prompt, D_MODEL), x_sample.reshape(t_sample, D_MODEL)], axis=0)
    cos_p, sin_p = _rope_tables(jnp.arange(seq, dtype=jnp.int32))
    row2 = _row2
    dims = (batch, seq, dec_b, dec_s)

    kv_prompt = [[] for _ in GROUPS]
    caches_t = [c.transpose(0, 1, 3, 4, 5, 2) for c in caches]
    new_caches_t = None
    v_rows = []
    for i in range(DEPTH):
        j = i // 2
        g1, b1 = row2(ln1_g[i]), row2(ln1_b[i])
        if i % 2 == 0:
            x_all, rows_p, new_caches_t = _attn_layer(
                x_all, caches_t, new_caches_t, j, attn_w_qkv[j], attn_w_o[j], g1, b1, cos_p, sin_p, dims)
            for g in range(N_GROUPS):
                kv_prompt[g].append(rows_p[g])
        else:
            x_all, v_new = _sgu_layer(
                x_all, sgu_w_in[j], sgu_b_in[j], sgu_ln_g[j], sgu_ln_b[j], sgu_w_s[j], sgu_b_s[j], sgu_w_out[j],
                g1, b1, dims)
            v_rows.append(v_new)
        x_all = _moe_layer(
            x_all, moe_w_router[i], moe_b_router[i], moe_w_gu, moe_b_gu, moe_w_down, moe_b_down,
            row2(ln2_g[i]), row2(ln2_b[i]), i)
    y_prompt = x_all[:t_prompt].reshape(batch, seq, D_MODEL)
    y_sample = x_all[t_prompt:].reshape(dec_b, dec_s, D_MODEL)
    kv_sample = [c.transpose(0, 1, 5, 2, 3, 4) for c in new_caches_t]
    return (
        y_prompt, y_sample,
        jnp.stack(kv_prompt[0]), jnp.stack(kv_prompt[1]), jnp.stack(kv_prompt[2]),
        kv_sample[0], kv_sample[1], kv_sample[2],
        jnp.stack(v_rows),
    )
```

```python
import functools
import math

import jax
import jax.numpy as jnp
from jax import lax
from jax.experimental import pallas as pl
from jax.experimental.pallas import tpu as pltpu

F32 = jnp.float32
BF16 = jnp.bfloat16

D_MODEL = 1024
HEAD_DIM = 64
N_HEADS = D_MODEL // HEAD_DIM
GROUPS = ((0, 6, 128, 1), (6, 11, 512, 4), (11, 16, 2048, 16))
N_GROUPS = len(GROUPS)
Q_BLOCK = 128
SEC = 384
ROPE_THETA = 10000.0
NEG_INF = -1e30
PAST_LEN = 8192
SGU_GROUPS = 8
SGU_CHUNK = 128
N_EXPERTS = 32
TOP_K = 4
SWIGLU_LIMIT = 7.0
SWIGLU_ALPHA = 1.702
MOE_BLOCK = 512
MOE_TILE = 256
SUB = 8
DEPTH = 4
DEEPNORM_ALPHA = (2 * DEPTH) ** 0.25
LN_EPS = 1e-5
LANES = 128
MIB = 1024 * 1024


def _params(vmem_mib, *semantics):
    return pltpu.CompilerParams(dimension_semantics=semantics, vmem_limit_bytes=vmem_mib * MIB)


def _ln(x, g, b):
    mu = jnp.mean(x, axis=-1, keepdims=True)
    xc = x - mu
    var = jnp.mean(xc * xc, axis=-1, keepdims=True)
    return xc * lax.rsqrt(var + LN_EPS) * g + b


def _qkv_prompt_kernel(x_ref, w_ref, cos_ref, sin_ref, o1_ref, o2_ref, o3_ref, *rest, bm):
    win_refs = rest[:2 * N_GROUPS]
    y_s = rest[-1]
    y = jnp.dot(x_ref[...].astype(BF16), w_ref[...], preferred_element_type=F32)
    cos = cos_ref[...]
    sin = sin_ref[...]
    lane = lax.broadcasted_iota(jnp.int32, (bm, LANES), 1)
    first_half = (lane & (HEAD_DIM // 2)) == 0
    blocks_per_sec = SEC // LANES
    n_rot = 2 * N_GROUPS * blocks_per_sec
    for c in range(3 * N_GROUPS * blocks_per_sec):
        blk = y[:, c * LANES:(c + 1) * LANES]
        if c < n_rot:
            swapped = jnp.where(first_half, pltpu.roll(blk, LANES - HEAD_DIM // 2, 1), pltpu.roll(blk, HEAD_DIM // 2, 1))
            blk = blk * cos + swapped * sin
        if c < n_rot // 2:
            blk = blk * (HEAD_DIM ** -0.5)
        else:
            sec, p = divmod(c - n_rot // 2, blocks_per_sec)
            win_ref = win_refs[sec]
            keep = win_ref.shape[1]
            win_ref[0, :, p * LANES:(p + 1) * LANES] = blk[bm - keep:, :]
        y_s[c] = blk
    for g, o_ref in enumerate((o1_ref, o2_ref, o3_ref)):
        d = GROUPS[g][3]
        n = bm // d
        for r in range(d):
            for part in range(3):
                for p in range(blocks_per_sec):
                    c = (part * N_GROUPS + g) * blocks_per_sec + p
                    rows = y_s[c] if d == 1 else y_s[c, pl.ds(r, n, stride=d), :]
                    o_ref[0, r, :, part * SEC + p * LANES:part * SEC + (p + 1) * LANES] = rows.astype(BF16)


def _qkv_prompt(x_all, w_perm, cos, sin, batch, seq):
    bm = 256
    tiles = seq // bm
    out_shape = [jax.ShapeDtypeStruct((batch, d, seq // d, 3 * SEC), BF16) for (_, _, _, d) in GROUPS]
    out_specs = [pl.BlockSpec((1, d, bm // d, 3 * SEC), lambda b, i: (b, 0, i, 0)) for (_, _, _, d) in GROUPS]
    for _ in range(2):
        for (_, _, win, _) in GROUPS:
            keep = min(win, seq)
            rows = min(keep, bm)
            first = tiles - keep // rows
            out_shape.append(jax.ShapeDtypeStruct((batch, keep, SEC), F32))
            out_specs.append(pl.BlockSpec((1, rows, SEC), lambda b, i, first=first: (b, jnp.maximum(i - first, 0), 0)))
    n_cols = 3 * N_GROUPS * SEC
    return pl.pallas_call(
        functools.partial(_qkv_prompt_kernel, bm=bm),
        grid=(batch, tiles),
        in_specs=[
            pl.BlockSpec((bm, D_MODEL), lambda b, i: (b * tiles + i, 0)),
            pl.BlockSpec((D_MODEL, n_cols), lambda b, i: (0, 0)),
            pl.BlockSpec((bm, LANES), lambda b, i: (i, 0)),
            pl.BlockSpec((bm, LANES), lambda b, i: (i, 0)),
        ],
        out_specs=out_specs,
        out_shape=out_shape,
        scratch_shapes=[pltpu.VMEM((n_cols // LANES, bm, LANES), F32)],
        compiler_params=_params(48, "arbitrary", "arbitrary"),
        name="qkv_prompt",
    )(x_all, w_perm, cos, sin)


def _attn_group_kernel(qkv_ref, o_ref, lse_ref, *, d, m_len, n_heads):
    nb = m_len // Q_BLOCK
    kw = min(2 * Q_BLOCK, m_len)
    lane = lax.broadcasted_iota(jnp.int32, (1, LANES), 1)
    lo = lane < HEAD_DIM
    qi = lax.broadcasted_iota(jnp.int32, (Q_BLOCK, kw), 0)
    kj = lax.broadcasted_iota(jnp.int32, (Q_BLOCK, kw), 1)

    def block(i, carry):
        r = i // nb
        n = i % nb
        ks = pl.multiple_of(jnp.maximum(n - 1, 0) * Q_BLOCK, Q_BLOCK)
        qs = pl.multiple_of(n * Q_BLOCK, Q_BLOCK)
        delta = qs - ks + qi - kj
        valid = (delta >= 0) & (delta <= Q_BLOCK)
        if d == 1:
            dst = pl.ds(qs, Q_BLOCK)
        else:
            dst = pl.ds(r + qs * d, Q_BLOCK, stride=d)
        lses = []
        for p in range(SEC // LANES):
            cols = slice(p * LANES, (p + 1) * LANES)
            q2 = qkv_ref[0, r, pl.ds(qs, Q_BLOCK), cols]
            k2 = qkv_ref[0, r, pl.ds(ks, kw), SEC + p * LANES:SEC + (p + 1) * LANES]
            v2 = qkv_ref[0, r, pl.ds(ks, kw), 2 * SEC + p * LANES:2 * SEC + (p + 1) * LANES]
            o_pair = jnp.zeros((Q_BLOCK, LANES), F32)
            for half in range(2):
                if 2 * p + half >= n_heads:
                    continue
                msk = lo if half == 0 else jnp.logical_not(lo)
                qm = jnp.where(msk, q2, jnp.zeros_like(q2))
                vm = jnp.where(msk, v2, jnp.zeros_like(v2))
                s = lax.dot_general(qm, k2, (((1,), (1,)), ((), ())), preferred_element_type=F32)
                s = jnp.where(valid, s, NEG_INF)
                mx = jnp.max(s, axis=1, keepdims=True)
                pe = jnp.exp(s - mx)
                l = jnp.sum(pe, axis=1, keepdims=True)
                o_pair = o_pair + jnp.dot(pe.astype(BF16), vm, preferred_element_type=F32) / l
                lses.append(mx + jnp.log(l))
            o_ref[0, p, dst, :] = o_pair
        lmax = functools.reduce(jnp.maximum, lses)
        lsum = functools.reduce(lambda a, b: a + b, [jnp.exp(l - lmax) for l in lses])
        glse = lmax + jnp.log(lsum) - math.log(n_heads)
        lse_ref[0, dst, :] = jnp.broadcast_to(glse, (Q_BLOCK, LANES))
        return carry

    lax.fori_loop(0, d * nb, block, 0)


def _attn_group(qkv_g, g, batch, seq):
    h0, h1, _, d = GROUPS[g]
    m_len = seq // d
    return pl.pallas_call(
        functools.partial(_attn_group_kernel, d=d, m_len=m_len, n_heads=h1 - h0),
        grid=(batch,),
        in_specs=[pl.BlockSpec((1, d, m_len, 3 * SEC), lambda b: (b, 0, 0, 0))],
        out_specs=[
            pl.BlockSpec((1, SEC // LANES, seq, LANES), lambda b: (b, 0, 0, 0)),
            pl.BlockSpec((1, seq, LANES), lambda b: (b, 0, 0)),
        ],
        out_shape=[
            jax.ShapeDtypeStruct((batch, SEC // LANES, seq, LANES), F32),
            jax.ShapeDtypeStruct((batch, seq, LANES), F32),
        ],
        compiler_params=_params(40, "arbitrary"),
        name=f"attn_group{g}",
    )(qkv_g)


def _attn_out_kernel(o1_ref, o2_ref, o3_ref, l1_ref, l2_ref, l3_ref, wo_ref, x_ref, g_ref, b_ref, out_ref):
    ls = [l1_ref[0], l2_ref[0], l3_ref[0]]
    mx = jnp.maximum(jnp.maximum(ls[0], ls[1]), ls[2])
    es = [jnp.exp(l - mx) for l in ls]
    inv = float(N_GROUPS) / (es[0] + es[1] + es[2])
    acc = None
    for g, o_ref in enumerate((o1_ref, o2_ref, o3_ref)):
        w = es[g] * inv
        o_g = jnp.concatenate([o_ref[0, p] * w for p in range(SEC // LANES)], axis=1)
        part = jnp.dot(o_g.astype(BF16), wo_ref[g], preferred_element_type=F32)
        acc = part if acc is None else acc + part
    out_ref[...] = _ln(DEEPNORM_ALPHA * x_ref[...] + acc, g_ref[...], b_ref[...])


def _attn_out_prompt(os_, ls_, wo_perm, x_all, ln_g, ln_b, batch, seq):
    bm = 512
    tiles = seq // bm
    t_all = x_all.shape[0]
    row = lambda b, i: (b * tiles + i, 0)
    const2 = lambda b, i: (0, 0)
    return pl.pallas_call(
        _attn_out_kernel,
        grid=(batch, tiles),
        in_specs=[pl.BlockSpec((1, SEC // LANES, bm, LANES), lambda b, i: (b, 0, i, 0))] * 3
        + [pl.BlockSpec((1, bm, LANES), lambda b, i: (b, i, 0))] * 3 + [
            pl.BlockSpec((N_GROUPS, SEC, D_MODEL), lambda b, i: (0, 0, 0)),
            pl.BlockSpec((bm, D_MODEL), row),
            pl.BlockSpec((1, D_MODEL), const2),
            pl.BlockSpec((1, D_MODEL), const2),
        ],
        out_specs=pl.BlockSpec((bm, D_MODEL), row),
        out_shape=jax.ShapeDtypeStruct((t_all, D_MODEL), F32),
        compiler_params=_params(40, "arbitrary", "arbitrary"),
        name="attn_out_prompt",
    )(*os_, *ls_, wo_perm, x_all, ln_g, ln_b)


def _mm_kernel(x_ref, w_ref, o_ref):
    o_ref[...] = jnp.dot(x_ref[...].astype(BF16), w_ref[...], preferred_element_type=F32)


def _mm_rows(x_all, w, row0, rows):
    n = w.shape[1]
    blk0 = row0 // rows
    return pl.pallas_call(
        _mm_kernel,
        grid=(1,),
        in_specs=[
            pl.BlockSpec((rows, D_MODEL), lambda i: (blk0, 0)),
            pl.BlockSpec((D_MODEL, n), lambda i: (0, 0)),
        ],
        out_specs=pl.BlockSpec((rows, n), lambda i: (0, 0)),
        out_shape=jax.ShapeDtypeStruct((rows, n), F32),
        compiler_params=_params(40, "arbitrary"),
        name="mm_rows",
    )(x_all, w)


def _proj_ln_kernel(a_ref, w_ref, x_ref, g_ref, b_ref, prev_ref, out_ref):
    del prev_ref
    acc = jnp.dot(a_ref[...].astype(BF16), w_ref[...], preferred_element_type=F32)
    out_ref[...] = _ln(DEEPNORM_ALPHA * x_ref[...] + acc, g_ref[...], b_ref[...])


def _proj_ln_rows(a, w, x_all, ln_g, ln_b, x_next, row0):
    rows = a.shape[0]
    blk0 = row0 // rows
    const2 = lambda i: (0, 0)
    return pl.pallas_call(
        _proj_ln_kernel,
        grid=(1,),
        in_specs=[
            pl.BlockSpec((rows, a.shape[1]), const2),
            pl.BlockSpec(w.shape, const2),
            pl.BlockSpec((rows, D_MODEL), lambda i: (blk0, 0)),
            pl.BlockSpec((1, D_MODEL), const2),
            pl.BlockSpec((1, D_MODEL), const2),
            pl.BlockSpec(memory_space=pl.ANY),
        ],
        out_specs=pl.BlockSpec((rows, D_MODEL), lambda i: (blk0, 0)),
        out_shape=jax.ShapeDtypeStruct(x_next.shape, F32),
        input_output_aliases={5: 0},
        compiler_params=_params(40, "arbitrary"),
        name="proj_ln_rows",
    )(a, w, x_all, ln_g, ln_b, x_next)


def _sgu_kernel(*refs, bm, chunk, emit_v, aliased):
    (x_ref, win_ref, bin_ref, lng_ref, lnb_ref, ws_ref, bs_ref, wout_ref, g1_ref, b1_ref) = refs[:10]
    rest = refs[10 + (1 if aliased else 0):]
    out_ref = rest[0]
    v_ref = rest[1] if emit_v else None
    ug_s = rest[-1]
    width = D_MODEL
    x = x_ref[...]
    z = jnp.dot(x.astype(BF16), win_ref[...], preferred_element_type=F32) + bin_ref[...]
    z = 0.5 * z * (1.0 + lax.erf(z * (2.0 ** -0.5)))
    v = _ln(z[:, width:], lng_ref[...], lnb_ref[...])
    if emit_v:
        v_ref[...] = v
    ii = lax.broadcasted_iota(jnp.int32, (SGU_CHUNK, SGU_CHUNK), 0)
    jj = lax.broadcasted_iota(jnp.int32, (SGU_CHUNK, SGU_CHUNK), 1)
    causal = (jj <= ii) & ((ii // chunk) == (jj // chunk))
    gw = width // SGU_GROUPS
    for gi in range(SGU_GROUPS):
        wsm = jnp.where(causal, ws_ref[gi], 0.0).astype(BF16)
        for c in range(bm // SGU_CHUNK):
            rows = slice(c * SGU_CHUNK, (c + 1) * SGU_CHUNK)
            cols = slice(gi * gw, (gi + 1) * gw)
            gate = jnp.dot(wsm, v[rows, cols].astype(BF16), preferred_element_type=F32) + bs_ref[gi]
            ug_s[rows, cols] = (z[rows, cols] * gate).astype(BF16)
    y = jnp.dot(ug_s[...], wout_ref[...], preferred_element_type=F32)
    out_ref[...] = _ln(DEEPNORM_ALPHA * x + y, g1_ref[...], b1_ref[...])


def _sgu(x_all, x_next, weights, row0, rows, bm, chunk, emit_v):
    w_in, b_in, ln_g, ln_b, ws, bs, w_out, g1, b1 = weights
    t_all = x_all.shape[0]
    blk0 = row0 // bm
    row = lambda i: (blk0 + i, 0)
    const2 = lambda i: (0, 0)
    const3 = lambda i: (0, 0, 0)
    aliased = x_next is not None
    in_specs = [
        pl.BlockSpec((bm, D_MODEL), row),
        pl.BlockSpec(w_in.shape, const2),
        pl.BlockSpec((1, 2 * D_MODEL), const2),
        pl.BlockSpec((1, D_MODEL), const2),
        pl.BlockSpec((1, D_MODEL), const2),
        pl.BlockSpec(ws.shape, const3),
        pl.BlockSpec(bs.shape, const3),
        pl.BlockSpec(w_out.shape, const2),
        pl.BlockSpec((1, D_MODEL), const2),
        pl.BlockSpec((1, D_MODEL), const2),
    ]
    args = [x_all, w_in, b_in, ln_g, ln_b, ws, bs, w_out, g1, b1]
    aliases = {}
    if aliased:
        in_specs.append(pl.BlockSpec(memory_space=pl.ANY))
        args.append(x_next)
        aliases = {10: 0}
    out_specs = [pl.BlockSpec((bm, D_MODEL), row)]
    out_shape = [jax.ShapeDtypeStruct((t_all, D_MODEL), F32)]
    if emit_v:
        out_specs.append(pl.BlockSpec((bm, D_MODEL), lambda i: (i, 0)))
        out_shape.append(jax.ShapeDtypeStruct((rows, D_MODEL), F32))
    return pl.pallas_call(
        functools.partial(_sgu_kernel, bm=bm, chunk=chunk, emit_v=emit_v, aliased=aliased),
        grid=(rows // bm,),
        in_specs=in_specs,
        out_specs=out_specs,
        out_shape=out_shape,
        scratch_shapes=[pltpu.VMEM((bm, D_MODEL), BF16)],
        input_output_aliases=aliases,
        compiler_params=_params(48, "arbitrary"),
        name="sgu",
    )(*args)


def _router_kernel(x_ref, wr_ref, br_ref, idx_ref, gate_ref, cnt_ref, *, bm, t_all):
    x = x_ref[...]
    w = wr_ref[...]
    x_hi = x.astype(BF16)
    w_hi = w.astype(BF16)
    x_lo = (x - x_hi.astype(F32)).astype(BF16)
    w_lo = (w - w_hi.astype(F32)).astype(BF16)
    logits = (jnp.dot(x_hi, w_hi, preferred_element_type=F32) + jnp.dot(x_lo, w_hi, preferred_element_type=F32)
              + jnp.dot(x_hi, w_lo, preferred_element_type=F32))
    logits = logits + br_ref[...]
    lane = lax.broadcasted_iota(jnp.int32, (bm, LANES), 1)
    lane_f = lane.astype(F32)
    row = pl.program_id(0) * bm + lax.broadcasted_iota(jnp.int32, (bm, LANES), 0)
    logits = jnp.where(row < t_all, logits, 0.0)
    logits = jnp.where(lane < N_EXPERTS, logits, -jnp.inf)
    vals, idxs = [], []
    for _ in range(TOP_K):
        m = jnp.max(logits, axis=1, keepdims=True)
        i = jnp.min(jnp.where(logits == m, lane_f, float(LANES)), axis=1, keepdims=True)
        vals.append(m)
        idxs.append(i)
        logits = jnp.where(lane_f == i, -jnp.inf, logits)
    es = [jnp.exp(v - vals[0]) for v in vals]
    inv = 1.0 / functools.reduce(lambda a, b: a + b, es)
    idx_out = jnp.zeros((bm, LANES), F32)
    gate_out = jnp.zeros((bm, LANES), F32)
    chosen = jnp.zeros((bm, LANES), F32)
    for k in range(TOP_K):
        idx_out = jnp.where(lane == k, idxs[k], idx_out)
        gate_out = jnp.where(lane == k, es[k] * inv, gate_out)
        chosen = chosen + jnp.where((lane_f == idxs[k]) & (row < t_all), 1.0, 0.0)
    idx_ref[...] = idx_out.astype(jnp.int32)
    gate_ref[...] = gate_out
    cnt_ref[0] = jnp.broadcast_to(jnp.sum(chosen, axis=0, keepdims=True), (SUB, LANES))


def _router(x_all, w_r, b_r):
    bm = MOE_TILE
    t_all = x_all.shape[0]
    n_tiles = pl.cdiv(t_all, bm)
    row = lambda i: (i, 0)
    const2 = lambda i: (0, 0)
    return pl.pallas_call(
        functools.partial(_router_kernel, bm=bm, t_all=t_all),
        grid=(n_tiles,),
        in_specs=[
            pl.BlockSpec((bm, D_MODEL), row),
            pl.BlockSpec((D_MODEL, LANES), const2),
            pl.BlockSpec((1, LANES), const2),
        ],
        out_specs=[pl.BlockSpec((bm, LANES), row), pl.BlockSpec((bm, LANES), row),
                   pl.BlockSpec((1, SUB, LANES), lambda i: (i, 0, 0))],
        out_shape=[
            jax.ShapeDtypeStruct((t_all, LANES), jnp.int32),
            jax.ShapeDtypeStruct((t_all, LANES), F32),
            jax.ShapeDtypeStruct((n_tiles, SUB, LANES), F32),
        ],
        compiler_params=_params(32, "arbitrary"),
        name="router",
    )(x_all, w_r, b_r)


def _local_slots(idx, row_valid):
    tile = idx.shape[0]
    lane = lax.broadcasted_iota(jnp.int32, (tile, LANES), 1)
    idx = jnp.where(row_valid, idx, -1)
    picks = [lane == idx[:, k:k + 1] for k in range(TOP_K)]
    chosen = functools.reduce(lambda a, b: a + b, [p.astype(F32) for p in picks])
    earlier = (lax.broadcasted_iota(jnp.int32, (tile, tile), 1)
               < lax.broadcasted_iota(jnp.int32, (tile, tile), 0)).astype(BF16)
    rank = jnp.dot(earlier, chosen.astype(BF16), preferred_element_type=F32)
    per_expert = jnp.broadcast_to(jnp.sum(chosen, axis=0, keepdims=True), (SUB, LANES))
    lower = (lax.broadcasted_iota(jnp.int32, (LANES, LANES), 0)
             < lax.broadcasted_iota(jnp.int32, (LANES, LANES), 1)).astype(BF16)
    base = jnp.dot(per_expert.astype(BF16), lower, preferred_element_type=F32)[0:1]
    place = base + rank + 1.0
    return [jnp.sum(jnp.where(p, place, 0.0), axis=1, keepdims=True) - 1.0 for p in picks]


def _segment_copies(cnt_ref, off_ref, j, hbm, buf, sem, to_hbm):
    local = 0
    for e in range(N_EXPERTS):
        cnt = cnt_ref[j * N_EXPERTS + e]
        off = off_ref[j * N_EXPERTS + e]
        for bit in range(MOE_TILE.bit_length()):
            size = 1 << bit

            @pl.when((cnt & size) != 0)
            def _(cnt=cnt, off=off, local=local, size=size):
                done = cnt & (size - 1)
                a = buf.at[pl.ds(pl.multiple_of((local + done) * SUB, SUB), size * SUB), :]
                b = hbm.at[pl.ds(pl.multiple_of((off + done) * SUB, SUB), size * SUB), :]
                (pltpu.make_async_copy(a, b, sem) if to_hbm else pltpu.make_async_copy(b, a, sem)).start()

        local = local + cnt


def _rows_to_tiles(ref, value):
    rows = value.shape[0]
    for s in range(D_MODEL // LANES):
        ref[pl.ds(s, rows, stride=SUB), :] = value[:, s * LANES:(s + 1) * LANES]


def _tiles_to_rows(ref, rows):
    return jnp.concatenate([ref[pl.ds(s, rows, stride=SUB), :] for s in range(D_MODEL // LANES)], axis=1)


def _dispatch_kernel(cnt_ref, off_ref, x_ref, idx_ref, xs_hbm, slot_ref, buf, sem, *, t_all, n_tiles):
    j = pl.program_id(0)
    par = j % 2
    tile = MOE_TILE
    row = j * tile + lax.broadcasted_iota(jnp.int32, (tile, LANES), 0)
    row_valid = row < t_all
    slots = _local_slots(idx_ref[...], row_valid)
    lane = lax.broadcasted_iota(jnp.int32, (tile, LANES), 1)
    slot_out = jnp.zeros((tile, LANES), F32)
    for k in range(TOP_K):
        slot_out = jnp.where(lane == k, slots[k], slot_out)
    slot_ref[...] = slot_out
    slots_t = slot_out.T
    pos = lax.broadcasted_iota(jnp.int32, (TOP_K * tile, tile), 0).astype(F32)
    hit = functools.reduce(jnp.logical_or, [pos == slots_t[k:k + 1, :] for k in range(TOP_K)])
    place = jnp.where(hit, 1.0, 0.0).astype(BF16)
    in_range = (j * tile + lax.broadcasted_iota(jnp.int32, (tile, D_MODEL), 0)) < t_all
    x = jnp.where(in_range, x_ref[...], 0.0).astype(BF16)
    ordered = jnp.dot(place, x, preferred_element_type=F32)
    _rows_to_tiles(buf.at[par], ordered)

    full = TOP_K * tile * SUB
    last = TOP_K * (t_all - (n_tiles - 1) * tile) * SUB

    @pl.when(j > 0)
    def _():
        pltpu.make_async_copy(buf.at[1 - par], xs_hbm.at[pl.ds(0, full), :], sem.at[0]).wait()

    _segment_copies(cnt_ref, off_ref, j, xs_hbm, buf.at[par], sem.at[0], True)

    @pl.when(j == n_tiles - 1)
    def _():
        pltpu.make_async_copy(buf.at[par, pl.ds(0, last), :], xs_hbm.at[pl.ds(0, last), :], sem.at[0]).wait()


def _dispatch(cnt, off, x_all, idx, n_slots):
    t_all = x_all.shape[0]
    n_tiles = pl.cdiv(t_all, MOE_TILE)
    row = lambda j, c, o: (j, 0)
    grid_spec = pltpu.PrefetchScalarGridSpec(
        num_scalar_prefetch=2,
        grid=(n_tiles,),
        in_specs=[pl.BlockSpec((MOE_TILE, D_MODEL), row), pl.BlockSpec((MOE_TILE, LANES), row)],
        out_specs=[pl.BlockSpec(memory_space=pl.ANY), pl.BlockSpec((MOE_TILE, LANES), row)],
        scratch_shapes=[pltpu.VMEM((2, TOP_K * MOE_TILE * SUB, LANES), F32), pltpu.SemaphoreType.DMA((1,))],
    )
    return pl.pallas_call(
        functools.partial(_dispatch_kernel, t_all=t_all, n_tiles=n_tiles),
        grid_spec=grid_spec,
        out_shape=[jax.ShapeDtypeStruct((n_slots * SUB, LANES), F32), jax.ShapeDtypeStruct((t_all, LANES), F32)],
        compiler_params=_params(40, "arbitrary"),
        name="moe_dispatch",
    )(cnt, off, x_all, idx)


def _expert_kernel(be_ref, nreal_ref, first_ref, par_ref, next_ref, xs_ref, wgu_hbm, bgu_ref, wdn_hbm, bdn_ref, ys_ref,
                   wgu_f32, wdn_f32, wgu_bf, wdn_bf, sem, *, layer):
    i = pl.program_id(0)
    ff = wdn_bf.shape[0]

    def copies(e, slot):
        return (pltpu.make_async_copy(wgu_hbm.at[layer, e], wgu_f32.at[slot], sem.at[0, slot]),
                pltpu.make_async_copy(wdn_hbm.at[layer, e], wdn_f32.at[slot], sem.at[1, slot]))

    @pl.when(i == 0)
    def _():
        for c in copies(be_ref[0], 0):
            c.start()

    @pl.when(i < nreal_ref[0])
    def _():
        @pl.when(first_ref[i] == 1)
        def _():
            slot = par_ref[i]
            for c in copies(be_ref[i], slot):
                c.wait()
            wgu_bf[...] = wgu_f32[slot].astype(BF16)
            wdn_bf[...] = wdn_f32[slot].astype(BF16)

            @pl.when(next_ref[i] >= 0)
            def _():
                for c in copies(next_ref[i], 1 - slot):
                    c.start()

        x = _tiles_to_rows(xs_ref, MOE_BLOCK).astype(BF16)
        h = jnp.dot(x, wgu_bf[...], preferred_element_type=F32) + bgu_ref[0]
        gate = jnp.minimum(h[:, :ff], SWIGLU_LIMIT)
        up = jnp.clip(h[:, ff:], -SWIGLU_LIMIT, SWIGLU_LIMIT)
        act = (up + 1.0) * gate * jax.nn.sigmoid(SWIGLU_ALPHA * gate)
        y = jnp.dot(act.astype(BF16), wdn_bf[...], preferred_element_type=F32) + bdn_ref[0]
        _rows_to_tiles(ys_ref, y)


def _experts(block_expert, n_real, run_first, run_parity, next_expert, xs, w_gu, b_gu, w_dn, b_dn, layer):
    n_blocks = xs.shape[0] // (MOE_BLOCK * SUB)
    ff = w_dn.shape[2]
    rows = lambda i, be, nr, *_: (jnp.minimum(i, nr[0] - 1), 0)
    by_expert = lambda i, be, *_: (layer, be[i], 0, 0)
    grid_spec = pltpu.PrefetchScalarGridSpec(
        num_scalar_prefetch=5,
        grid=(n_blocks,),
        in_specs=[
            pl.BlockSpec((MOE_BLOCK * SUB, LANES), rows),
            pl.BlockSpec(memory_space=pl.ANY),
            pl.BlockSpec((None, 1, 1, 2 * ff), by_expert),
            pl.BlockSpec(memory_space=pl.ANY),
            pl.BlockSpec((None, 1, 1, D_MODEL), by_expert),
        ],
        out_specs=pl.BlockSpec((MOE_BLOCK * SUB, LANES), rows),
        scratch_shapes=[
            pltpu.VMEM((2, D_MODEL, 2 * ff), F32),
            pltpu.VMEM((2, ff, D_MODEL), F32),
            pltpu.VMEM((D_MODEL, 2 * ff), BF16),
            pltpu.VMEM((ff, D_MODEL), BF16),
            pltpu.SemaphoreType.DMA((2, 2)),
        ],
    )
    return pl.pallas_call(
        functools.partial(_expert_kernel, layer=layer),
        grid_spec=grid_spec,
        out_shape=jax.ShapeDtypeStruct(xs.shape, F32),
        compiler_params=_params(56, "arbitrary"),
        name="experts",
    )(block_expert, n_real, run_first, run_parity, next_expert, xs, w_gu, b_gu, w_dn, b_dn)


def _combine_kernel(cnt_ref, off_ref, x_ref, slot_ref, gate_ref, g_ref, b_ref, ys_hbm, out_ref, buf, sem, *, t_all, n_tiles):
    j = pl.program_id(0)
    par = j % 2
    tile = MOE_TILE
    full = TOP_K * tile * SUB
    last = TOP_K * (t_all - (n_tiles - 1) * tile) * SUB

    @pl.when(j == 0)
    def _():
        _segment_copies(cnt_ref, off_ref, 0, ys_hbm, buf.at[0], sem.at[0], False)

    @pl.when(j + 1 < n_tiles)
    def _():
        _segment_copies(cnt_ref, off_ref, j + 1, ys_hbm, buf.at[1 - par], sem.at[1 - par], False)

    @pl.when(j < n_tiles - 1)
    def _():
        pltpu.make_async_copy(ys_hbm.at[pl.ds(0, full), :], buf.at[par], sem.at[par]).wait()

    @pl.when(j == n_tiles - 1)
    def _():
        pltpu.make_async_copy(ys_hbm.at[pl.ds(0, last), :], buf.at[par, pl.ds(0, last), :], sem.at[par]).wait()

    n_valid = TOP_K * jnp.minimum(tile, t_all - j * tile)
    ys = _tiles_to_rows(buf.at[par], TOP_K * tile)
    ys = jnp.where(lax.broadcasted_iota(jnp.int32, ys.shape, 0) < n_valid, ys, 0.0).astype(BF16)
    slots = slot_ref[...]
    gates = gate_ref[...]
    pos = lax.broadcasted_iota(jnp.int32, (tile, TOP_K * tile), 1).astype(F32)
    weights = jnp.zeros((tile, TOP_K * tile), F32)
    for k in range(TOP_K):
        weights = jnp.where(pos == slots[:, k:k + 1], gates[:, k:k + 1], weights)
    y = jnp.dot(weights.astype(BF16), ys, preferred_element_type=F32)
    out_ref[...] = _ln(DEEPNORM_ALPHA * x_ref[...] + y, g_ref[...], b_ref[...])


def _combine(cnt, off, x_all, slots, gates, ln_g, ln_b, ys):
    t_all = x_all.shape[0]
    n_tiles = pl.cdiv(t_all, MOE_TILE)
    row = lambda j, c, o: (j, 0)
    const2 = lambda j, c, o: (0, 0)
    grid_spec = pltpu.PrefetchScalarGridSpec(
        num_scalar_prefetch=2,
        grid=(n_tiles,),
        in_specs=[
            pl.BlockSpec((MOE_TILE, D_MODEL), row),
            pl.BlockSpec((MOE_TILE, LANES), row),
            pl.BlockSpec((MOE_TILE, LANES), row),
            pl.BlockSpec((1, D_MODEL), const2),
            pl.BlockSpec((1, D_MODEL), const2),
            pl.BlockSpec(memory_space=pl.ANY),
        ],
        out_specs=pl.BlockSpec((MOE_TILE, D_MODEL), row),
        scratch_shapes=[pltpu.VMEM((2, TOP_K * MOE_TILE * SUB, LANES), F32), pltpu.SemaphoreType.DMA((2,))],
    )
    return pl.pallas_call(
        functools.partial(_combine_kernel, t_all=t_all, n_tiles=n_tiles),
        grid_spec=grid_spec,
        out_shape=jax.ShapeDtypeStruct((t_all, D_MODEL), F32),
        compiler_params=_params(40, "arbitrary"),
        name="moe_combine",
    )(cnt, off, x_all, slots, gates, ln_g, ln_b, ys)


def _moe(x_all, w_r, b_r, w_gu, b_gu, w_dn, b_dn, ln_g, ln_b, layer):
    t_all = x_all.shape[0]
    idx, gates, tile_counts = _router(x_all, w_r, b_r)
    cnt = tile_counts[:, 0, :N_EXPERTS].astype(jnp.int32)
    counts = jnp.sum(cnt, axis=0)
    padded = (counts + MOE_BLOCK - 1) // MOE_BLOCK * MOE_BLOCK
    pad_end = jnp.cumsum(padded)
    pad_start = pad_end - padded
    off = pad_start[None, :] + jnp.cumsum(cnt, axis=0) - cnt
    n_blocks = -(-t_all * TOP_K // MOE_BLOCK) + N_EXPERTS
    block_start = jnp.arange(n_blocks, dtype=jnp.int32) * MOE_BLOCK
    block_expert = jnp.minimum(
        jnp.sum((pad_end[None, :] <= block_start[:, None]).astype(jnp.int32), axis=1), N_EXPERTS - 1)
    n_real = (pad_end[-1:] // MOE_BLOCK).astype(jnp.int32)
    block = jnp.arange(n_blocks, dtype=jnp.int32)
    prev_expert = jnp.concatenate([jnp.full((1,), -1, jnp.int32), block_expert[:-1]])
    run_first = ((block < n_real[0]) & (block_expert != prev_expert)).astype(jnp.int32)
    run_parity = (jnp.cumsum(run_first) - 1) % 2
    after_run = pad_end[block_expert] // MOE_BLOCK
    next_expert = jnp.where(after_run < n_real[0], block_expert[jnp.minimum(after_run, n_blocks - 1)], -1)
    cnt_flat, off_flat = cnt.reshape(-1), off.reshape(-1).astype(jnp.int32)
    xs, slots = _dispatch(cnt_flat, off_flat, x_all, idx, n_blocks * MOE_BLOCK)
    ys = _experts(block_expert, n_real, run_first, run_parity.astype(jnp.int32), next_expert.astype(jnp.int32),
                  xs, w_gu, b_gu, w_dn, b_dn, layer)
    return _combine(cnt_flat, off_flat, x_all, slots, gates, ln_g, ln_b, ys)


def _rope_tables(pos):
    half = HEAD_DIM // 2
    inv_freq = ROPE_THETA ** (-jnp.arange(half, dtype=F32) / half)
    ang = pos.astype(F32)[:, None] * inv_freq[None, :]
    cos = jnp.cos(ang)
    sin = jnp.sin(ang)
    cos = jnp.concatenate([cos, cos], axis=1)
    sin = jnp.concatenate([-sin, sin], axis=1)
    reps = LANES // HEAD_DIM
    return jnp.tile(cos, (1, reps)), jnp.tile(sin, (1, reps))


def _permute_qkv_weight(w_qkv):
    w3 = w_qkv.reshape(D_MODEL, 3, N_HEADS * HEAD_DIM)
    secs = []
    for part in range(3):
        for (h0, h1, _, _) in GROUPS:
            sec = w3[:, part, h0 * HEAD_DIM:h1 * HEAD_DIM]
            secs.append(jnp.pad(sec, ((0, 0), (0, SEC - sec.shape[1]))))
    return jnp.concatenate(secs, axis=1).astype(BF16)


def _permute_out_weight(w_o):
    secs = []
    for (h0, h1, _, _) in GROUPS:
        sec = w_o[h0 * HEAD_DIM:h1 * HEAD_DIM]
        secs.append(jnp.pad(sec, ((0, SEC - sec.shape[0]), (0, 0))))
    return jnp.stack(secs).astype(BF16)


def _sample_step_kernel(*refs, dec_s, aliased):
    y_ref, cos_ref, sin_ref = refs[:3]
    cache_refs = refs[3:3 + N_GROUPS]
    outs = refs[3 + N_GROUPS * (2 if aliased else 1):]
    o_ref, new_refs = outs[0], outs[1:]
    y = y_ref[0]
    cos = cos_ref[...]
    sin = sin_ref[...]
    lane = lax.broadcasted_iota(jnp.int32, (SUB, LANES), 1)
    first_half = (lane & (HEAD_DIM // 2)) == 0
    blocks_per_sec = SEC // LANES
    n_rot = 2 * N_GROUPS * blocks_per_sec
    blocks = []
    for c in range(3 * N_GROUPS * blocks_per_sec):
        blk = y[:, c * LANES:(c + 1) * LANES]
        if c < n_rot:
            swapped = jnp.where(first_half, pltpu.roll(blk, LANES - HEAD_DIM // 2, 1), pltpu.roll(blk, HEAD_DIM // 2, 1))
            blk = blk * cos + swapped * sin
        if c < n_rot // 2:
            blk = blk * (HEAD_DIM ** -0.5)
        blocks.append(blk)

    lo = lane < HEAD_DIM
    new_row = lax.broadcasted_iota(jnp.int32, (SUB, SUB), 1)
    qry_row = lax.broadcasted_iota(jnp.int32, (SUB, SUB), 0)
    pad_rows = jnp.zeros((LANES - SUB, LANES), F32)
    contract_lanes = (((1,), (1,)), ((), ()))
    o_blocks, glses = [], []
    for g in range(N_GROUPS):
        h0, h1, win, d = GROUPS[g]
        n_heads = h1 - h0
        cache_ref, new_ref = cache_refs[g], new_refs[g]
        lb = cache_ref.shape[-1]
        key = lax.broadcasted_iota(jnp.int32, (SUB, lb), 1)
        dist = lb + lax.broadcasted_iota(jnp.int32, (SUB, lb), 0) - key
        visible = ((dist & (d - 1)) == 0) & (dist <= win)
        dist_new = qry_row - new_row
        visible_new = (dist_new >= 0) & ((dist_new & (d - 1)) == 0) & (new_row < dec_s)
        tail_lane = lax.broadcasted_iota(jnp.int32, (HEAD_DIM, LANES), 1)
        lses = []
        for p in range(blocks_per_sec):
            heads = [h for h in (2 * p, 2 * p + 1) if h < n_heads]
            if not heads:
                o_blocks.append(jnp.zeros((SUB, LANES), F32))
                continue
            q2 = blocks[g * blocks_per_sec + p]
            k_new = blocks[(N_GROUPS + g) * blocks_per_sec + p]
            v_new = blocks[(2 * N_GROUPS + g) * blocks_per_sec + p]
            slabs = [[cache_ref[part, h] for h in heads] for part in range(2)]
            zero_slab = jnp.zeros((HEAD_DIM, lb), F32)
            kt2 = jnp.concatenate(slabs[0] + [zero_slab] * (2 - len(heads)), axis=0).astype(BF16)
            vt2 = jnp.concatenate(slabs[1] + [zero_slab] * (2 - len(heads)), axis=0).astype(BF16)
            q_both = jnp.concatenate([jnp.where(lo, q2, 0.0), jnp.where(lo, 0.0, q2)], axis=0).astype(BF16)
            s_both = jnp.dot(q_both, kt2, preferred_element_type=F32)
            s_new_both = lax.dot_general(q_both, k_new.astype(BF16), contract_lanes, preferred_element_type=F32)
            probs, probs_new = [], []
            for half in range(2):
                s = jnp.where(visible, s_both[half * SUB:(half + 1) * SUB], NEG_INF)
                s_new = jnp.where(visible_new, s_new_both[half * SUB:(half + 1) * SUB], NEG_INF)
                mx = jnp.maximum(jnp.max(s, axis=1, keepdims=True), jnp.max(s_new, axis=1, keepdims=True))
                pe = jnp.exp(s - mx)
                pe_new = jnp.exp(s_new - mx)
                l = jnp.sum(pe, axis=1, keepdims=True) + jnp.sum(pe_new, axis=1, keepdims=True)
                probs.append(pe / l)
                probs_new.append(pe_new / l)
                if half < len(heads):
                    lses.append(mx + jnp.log(l))
            pv = lax.dot_general(jnp.concatenate(probs, axis=0).astype(BF16), vt2, contract_lanes, preferred_element_type=F32)
            pv_new = jnp.dot(jnp.concatenate(probs_new, axis=0).astype(BF16), v_new.astype(BF16), preferred_element_type=F32)
            both = pv + pv_new
            o_blocks.append(jnp.where(lo, both[:SUB], both[SUB:]))
            for part, new in enumerate((k_new, v_new)):
                new_t = jnp.concatenate([new, pad_rows], axis=0).T
                for i, h in enumerate(heads):
                    shifted = pltpu.roll(slabs[part][i], lb - dec_s, 1)
                    tail = shifted[:, lb - LANES:]
                    feats = new_t[i * HEAD_DIM:(i + 1) * HEAD_DIM]
                    for jj in range(dec_s):
                        tail = jnp.where(tail_lane == LANES - dec_s + jj, feats[:, jj:jj + 1], tail)
                    if lb > LANES:
                        new_ref[part, h, :, :lb - LANES] = shifted[:, :lb - LANES]
                    new_ref[part, h, :, lb - LANES:] = tail
        lmax = functools.reduce(jnp.maximum, lses)
        lsum = functools.reduce(lambda a, b: a + b, [jnp.exp(l - lmax) for l in lses])
        glses.append(lmax + jnp.log(lsum) - math.log(n_heads))
    gmax = functools.reduce(jnp.maximum, glses)
    es = [jnp.exp(l - gmax) for l in glses]
    inv = float(N_GROUPS) / functools.reduce(lambda a, b: a + b, es)
    scaled = [o_blocks[g * blocks_per_sec + p] * (es[g] * inv) for g in range(N_GROUPS) for p in range(blocks_per_sec)]
    o_ref[0] = jnp.concatenate(scaled, axis=1)


def _sample_step(y_s, caches_t, new_caches_t, layer, cos_s, sin_s, dec_b, dec_s):
    n_cols = 3 * N_GROUPS * SEC
    y_pad = jnp.pad(y_s.reshape(dec_b, dec_s, n_cols), ((0, 0), (0, SUB - dec_s), (0, 0)))
    aliased = new_caches_t is not None
    cache_specs = []
    for (h0, h1, win, d), cache in zip(GROUPS, caches_t):
        assert cache.shape[-1] == win and dec_s <= LANES and win % LANES == 0
        cache_specs.append(pl.BlockSpec((None, None) + cache.shape[2:], lambda b: (layer, b, 0, 0, 0, 0)))
    in_specs = [
        pl.BlockSpec((1, SUB, n_cols), lambda b: (b, 0, 0)),
        pl.BlockSpec((SUB, LANES), lambda b: (0, 0)),
        pl.BlockSpec((SUB, LANES), lambda b: (0, 0)),
    ] + cache_specs
    args = [y_pad, cos_s, sin_s, *caches_t]
    aliases = {}
    if aliased:
        in_specs += [pl.BlockSpec(memory_space=pl.ANY)] * N_GROUPS
        aliases = {len(args) + g: 1 + g for g in range(N_GROUPS)}
        args += list(new_caches_t)
    outs = pl.pallas_call(
        functools.partial(_sample_step_kernel, dec_s=dec_s, aliased=aliased),
        grid=(dec_b,),
        in_specs=in_specs,
        out_specs=[pl.BlockSpec((1, SUB, N_GROUPS * SEC), lambda b: (b, 0, 0))] + cache_specs,
        out_shape=[jax.ShapeDtypeStruct((dec_b, SUB, N_GROUPS * SEC), F32)]
        + [jax.ShapeDtypeStruct(c.shape, F32) for c in caches_t],
        input_output_aliases=aliases,
        compiler_params=_params(52, "arbitrary"),
        name="sample_step",
    )(*args)
    return outs[0][:, :dec_s].reshape(dec_b * dec_s, N_GROUPS * SEC), outs[1:]


def _row2(a):
    return a.reshape(1, -1)


def _attn_layer(x_all, caches_t, new_caches_t, layer, w_qkv, w_o, g1, b1, cos_p, sin_p, dims):
    batch, seq, dec_b, dec_s = dims
    t_prompt = batch * seq
    t_sample = dec_b * dec_s
    w_perm = _permute_qkv_weight(w_qkv)
    wo_perm = _permute_out_weight(w_o)
    outs = _qkv_prompt(x_all, w_perm, cos_p, sin_p, batch, seq)
    qkv_groups, windows = outs[:N_GROUPS], outs[N_GROUPS:]
    os_, ls_ = [], []
    for g in range(N_GROUPS):
        o_g, l_g = _attn_group(qkv_groups[g], g, batch, seq)
        os_.append(o_g)
        ls_.append(l_g)
    x_next = _attn_out_prompt(os_, ls_, wo_perm, x_all, g1, b1, batch, seq)
    rows_p = []
    for g, (h0, h1, _, _) in enumerate(GROUPS):
        hg = h1 - h0
        halves = [windows[part * N_GROUPS + g][:, :, :hg * HEAD_DIM].reshape(batch, -1, hg, HEAD_DIM) for part in range(2)]
        rows_p.append(jnp.stack(halves, axis=2))
    y_s = _mm_rows(x_all, w_perm, t_prompt, t_sample)
    cos_s, sin_s = _rope_tables(PAST_LEN + jnp.arange(SUB, dtype=jnp.int32))
    o_s, new_caches_t = _sample_step(y_s, caches_t, new_caches_t, layer, cos_s, sin_s, dec_b, dec_s)
    x_all = _proj_ln_rows(o_s, wo_perm.reshape(N_GROUPS * SEC, D_MODEL), x_all, g1, b1, x_next, t_prompt)
    return x_all, rows_p, new_caches_t


def _sgu_layer(x_all, w_in, b_in, ln_g, ln_b, ws, bs, w_out, g1, b1, dims):
    batch, seq, dec_b, dec_s = dims
    t_prompt = batch * seq
    t_sample = dec_b * dec_s
    common = (w_in.astype(BF16), _row2(b_in), _row2(ln_g), _row2(ln_b))
    tail = (w_out.astype(BF16), g1, b1)
    bs_p = jnp.broadcast_to(bs[:, :, None], (SGU_GROUPS, SGU_CHUNK, SGU_CHUNK))
    x_next = _sgu(x_all, None, common + (ws, bs_p) + tail, 0, t_prompt, 256, SGU_CHUNK, False)[0]
    c = min(SGU_CHUNK, dec_s)
    reps = SGU_CHUNK // c
    ws_s = jnp.tile(ws[:, :c, :c], (1, reps, reps))
    bs_s = jnp.broadcast_to(jnp.tile(bs[:, :c], (1, reps))[:, :, None], (SGU_GROUPS, SGU_CHUNK, SGU_CHUNK))
    x_all, v_new = _sgu(x_all, x_next, common + (ws_s, bs_s) + tail, t_prompt, t_sample, t_sample, c, True)
    return x_all, v_new.reshape(dec_b, dec_s, D_MODEL)


def _moe_layer(x_all, w_router, b_router, w_gu, b_gu, w_dn, b_dn, ln_g, ln_b, layer):
    w_r = jnp.pad(w_router, ((0, 0), (0, LANES - N_EXPERTS)))
    b_r = jnp.pad(b_router, (0, LANES - N_EXPERTS)).reshape(1, LANES)
    n_layers = w_gu.shape[0]
    return _moe(
        x_all, w_r, b_r,
        w_gu, b_gu.reshape(n_layers, N_EXPERTS, 1, -1),
        w_dn, b_dn.reshape(n_layers, N_EXPERTS, 1, -1),
        ln_g, ln_b, layer)


def kernel(x_prompt, x_sample, cache_kv_w128, cache_kv_w512, cache_kv_w2048, attn_w_qkv, attn_w_o, sgu_w_in, sgu_b_in, sgu_ln_g, sgu_ln_b, sgu_w_s, sgu_b_s, sgu_w_out, moe_w_router, moe_b_router, moe_w_gu, moe_b_gu, moe_w_down, moe_b_down, ln1_g, ln1_b, ln2_g, ln2_b):
    batch, seq, _ = x_prompt.shape
    dec_b, dec_s, _ = x_sample.shape
    t_prompt = batch * seq
    t_sample = dec_b * dec_s
    caches = (cache_kv_w128, cache_kv_w512, cache_kv_w2048)
    x_all = jnp.concatenate([x_prompt.reshape(t_prompt, D_MODEL), x_sample.reshape(t_sample, D_MODEL)], axis=0)
    cos_p, sin_p = _rope_tables(jnp.arange(seq, dtype=jnp.int32))
    row2 = _row2
    dims = (batch, seq, dec_b, dec_s)

    kv_prompt = [[] for _ in GROUPS]
    caches_t = [c.transpose(0, 1, 3, 4, 5, 2) for c in caches]
    new_caches_t = None
    v_rows = []
    for i in range(DEPTH):
        j = i // 2
        g1, b1 = row2(ln1_g[i]), row2(ln1_b[i])
        if i % 2 == 0:
            x_all, rows_p, new_caches_t = _attn_layer(
                x_all, caches_t, new_caches_t, j, attn_w_qkv[j], attn_w_o[j], g1, b1, cos_p, sin_p, dims)
            for g in range(N_GROUPS):
                kv_prompt[g].append(rows_p[g])
        else:
            x_all, v_new = _sgu_layer(
                x_all, sgu_w_in[j], sgu_b_in[j], sgu_ln_g[j], sgu_ln_b[j], sgu_w_s[j], sgu_b_s[j], sgu_w_out[j],
                g1, b1, dims)
            v_rows.append(v_new)
        x_all = _moe_layer(
            x_all, moe_w_router[i], moe_b_router[i], moe_w_gu, moe_b_gu, moe_w_down, moe_b_down,
            row2(ln2_g[i]), row2(ln2_b[i]), i)
    y_prompt = x_all[:t_prompt].reshape(batch, seq, D_MODEL)
    y_sample = x_all[t_prompt:].reshape(dec_b, dec_s, D_MODEL)
    kv_sample = [c.transpose(0, 1, 5, 2, 3, 4) for c in new_caches_t]
    return (
        y_prompt, y_sample,
        jnp.stack(kv_prompt[0]), jnp.stack(kv_prompt[1]), jnp.stack(kv_prompt[2]),
        kv_sample[0], kv_sample[1], kv_sample[2],
        jnp.stack(v_rows),
    )
```

```python
import functools
import math

import jax
import jax.numpy as jnp
from jax import lax
from jax.experimental import pallas as pl
from jax.experimental.pallas import tpu as pltpu

F32 = jnp.float32
BF16 = jnp.bfloat16

D_MODEL = 1024
HEAD_DIM = 64
N_HEADS = D_MODEL // HEAD_DIM
GROUPS = ((0, 6, 128, 1), (6, 11, 512, 4), (11, 16, 2048, 16))
N_GROUPS = len(GROUPS)
Q_BLOCK = 128
SEC = 384
ROPE_THETA = 10000.0
NEG_INF = -1e30
PAST_LEN = 8192
SGU_GROUPS = 8
SGU_CHUNK = 128
N_EXPERTS = 32
TOP_K = 4
SWIGLU_LIMIT = 7.0
SWIGLU_ALPHA = 1.702
MOE_BLOCK = 512
MOE_TILE = 256
SUB = 8
DEPTH = 4
DEEPNORM_ALPHA = (2 * DEPTH) ** 0.25
LN_EPS = 1e-5
LANES = 128
MIB = 1024 * 1024


def _params(vmem_mib, *semantics):
    return pltpu.CompilerParams(dimension_semantics=semantics, vmem_limit_bytes=vmem_mib * MIB)


def _ln(x, g, b):
    mu = jnp.mean(x, axis=-1, keepdims=True)
    xc = x - mu
    var = jnp.mean(xc * xc, axis=-1, keepdims=True)
    return xc * lax.rsqrt(var + LN_EPS) * g + b


def _qkv_prompt_kernel(x_ref, w_ref, cos_ref, sin_ref, o1_ref, o2_ref, o3_ref, *rest, bm):
    win_refs = rest[:2 * N_GROUPS]
    y_s = rest[-1]
    y = jnp.dot(x_ref[...].astype(BF16), w_ref[...], preferred_element_type=F32)
    cos = cos_ref[...]
    sin = sin_ref[...]
    lane = lax.broadcasted_iota(jnp.int32, (bm, LANES), 1)
    first_half = (lane & (HEAD_DIM // 2)) == 0
    blocks_per_sec = SEC // LANES
    n_rot = 2 * N_GROUPS * blocks_per_sec
    for c in range(3 * N_GROUPS * blocks_per_sec):
        blk = y[:, c * LANES:(c + 1) * LANES]
        if c < n_rot:
            swapped = jnp.where(first_half, pltpu.roll(blk, LANES - HEAD_DIM // 2, 1), pltpu.roll(blk, HEAD_DIM // 2, 1))
            blk = blk * cos + swapped * sin
        if c < n_rot // 2:
            blk = blk * (HEAD_DIM ** -0.5)
        else:
            sec, p = divmod(c - n_rot // 2, blocks_per_sec)
            win_ref = win_refs[sec]
            keep = win_ref.shape[1]
            win_ref[0, :, p * LANES:(p + 1) * LANES] = blk[bm - keep:, :]
        y_s[c] = blk
    for g, o_ref in enumerate((o1_ref, o2_ref, o3_ref)):
        d = GROUPS[g][3]
        n = bm // d
        for r in range(d):
            for part in range(3):
                for p in range(blocks_per_sec):
                    c = (part * N_GROUPS + g) * blocks_per_sec + p
                    rows = y_s[c] if d == 1 else y_s[c, pl.ds(r, n, stride=d), :]
                    o_ref[0, r, :, part * SEC + p * LANES:part * SEC + (p + 1) * LANES] = rows.astype(BF16)


def _qkv_prompt(x_all, w_perm, cos, sin, batch, seq):
    bm = 512
    tiles = seq // bm
    out_shape = [jax.ShapeDtypeStruct((batch, d, seq // d, 3 * SEC), BF16) for (_, _, _, d) in GROUPS]
    out_specs = [pl.BlockSpec((1, d, bm // d, 3 * SEC), lambda b, i: (b, 0, i, 0)) for (_, _, _, d) in GROUPS]
    for _ in range(2):
        for (_, _, win, _) in GROUPS:
            keep = min(win, seq)
            rows = min(keep, bm)
            first = tiles - keep // rows
            out_shape.append(jax.ShapeDtypeStruct((batch, keep, SEC), F32))
            out_specs.append(pl.BlockSpec((1, rows, SEC), lambda b, i, first=first: (b, jnp.maximum(i - first, 0), 0)))
    n_cols = 3 * N_GROUPS * SEC
    return pl.pallas_call(
        functools.partial(_qkv_prompt_kernel, bm=bm),
        grid=(batch, tiles),
        in_specs=[
            pl.BlockSpec((bm, D_MODEL), lambda b, i: (b * tiles + i, 0)),
            pl.BlockSpec((D_MODEL, n_cols), lambda b, i: (0, 0)),
            pl.BlockSpec((bm, LANES), lambda b, i: (i, 0)),
            pl.BlockSpec((bm, LANES), lambda b, i: (i, 0)),
        ],
        out_specs=out_specs,
        out_shape=out_shape,
        scratch_shapes=[pltpu.VMEM((n_cols // LANES, bm, LANES), F32)],
        compiler_params=_params(48, "arbitrary", "arbitrary"),
        name="qkv_prompt",
    )(x_all, w_perm, cos, sin)


def _attn_group_kernel(qkv_ref, o_ref, lse_ref, *, d, m_len, n_heads):
    nb = m_len // Q_BLOCK
    kw = min(2 * Q_BLOCK, m_len)
    lane = lax.broadcasted_iota(jnp.int32, (1, LANES), 1)
    lo = lane < HEAD_DIM
    qi = lax.broadcasted_iota(jnp.int32, (Q_BLOCK, kw), 0)
    kj = lax.broadcasted_iota(jnp.int32, (Q_BLOCK, kw), 1)

    def block(i, carry):
        r = i // nb
        n = i % nb
        ks = pl.multiple_of(jnp.maximum(n - 1, 0) * Q_BLOCK, Q_BLOCK)
        qs = pl.multiple_of(n * Q_BLOCK, Q_BLOCK)
        delta = qs - ks + qi - kj
        valid = (delta >= 0) & (delta <= Q_BLOCK)
        if d == 1:
            dst = pl.ds(qs, Q_BLOCK)
        else:
            dst = pl.ds(r + qs * d, Q_BLOCK, stride=d)
        lses = []
        for p in range(SEC // LANES):
            cols = slice(p * LANES, (p + 1) * LANES)
            q2 = qkv_ref[0, r, pl.ds(qs, Q_BLOCK), cols]
            k2 = qkv_ref[0, r, pl.ds(ks, kw), SEC + p * LANES:SEC + (p + 1) * LANES]
            v2 = qkv_ref[0, r, pl.ds(ks, kw), 2 * SEC + p * LANES:2 * SEC + (p + 1) * LANES]
            n_half = min(2, n_heads - 2 * p)
            if n_half <= 0:
                o_ref[0, p, dst, :] = jnp.zeros((Q_BLOCK, LANES), F32)
                continue
            zero = jnp.zeros_like(q2)
            q_both = jnp.concatenate([jnp.where(lo, q2, zero), jnp.where(lo, zero, q2)][:n_half], axis=0)
            s_both = lax.dot_general(q_both, k2, (((1,), (1,)), ((), ())), preferred_element_type=F32)
            probs, inv_l = [], []
            for half in range(n_half):
                s = jnp.where(valid, s_both[half * Q_BLOCK:(half + 1) * Q_BLOCK], NEG_INF)
                mx = jnp.max(s, axis=1, keepdims=True)
                pe = jnp.exp(s - mx)
                l = jnp.sum(pe, axis=1, keepdims=True)
                probs.append(pe.astype(BF16))
                inv_l.append(1.0 / l)
                lses.append(mx + jnp.log(l))
            pv = jnp.dot(jnp.concatenate(probs, axis=0), v2, preferred_element_type=F32)
            o_pair = pv[:Q_BLOCK] * inv_l[0]
            if n_half == 2:
                o_pair = jnp.where(lo, o_pair, pv[Q_BLOCK:] * inv_l[1])
            else:
                o_pair = jnp.where(lo, o_pair, 0.0)
            o_ref[0, p, dst, :] = o_pair
        lmax = functools.reduce(jnp.maximum, lses)
        lsum = functools.reduce(lambda a, b: a + b, [jnp.exp(l - lmax) for l in lses])
        glse = lmax + jnp.log(lsum) - math.log(n_heads)
        lse_ref[0, dst, :] = jnp.broadcast_to(glse, (Q_BLOCK, LANES))
        return carry

    lax.fori_loop(0, d * nb, block, 0)


def _attn_group(qkv_g, g, batch, seq):
    h0, h1, _, d = GROUPS[g]
    m_len = seq // d
    return pl.pallas_call(
        functools.partial(_attn_group_kernel, d=d, m_len=m_len, n_heads=h1 - h0),
        grid=(batch,),
        in_specs=[pl.BlockSpec((1, d, m_len, 3 * SEC), lambda b: (b, 0, 0, 0))],
        out_specs=[
            pl.BlockSpec((1, SEC // LANES, seq, LANES), lambda b: (b, 0, 0, 0)),
            pl.BlockSpec((1, seq, LANES), lambda b: (b, 0, 0)),
        ],
        out_shape=[
            jax.ShapeDtypeStruct((batch, SEC // LANES, seq, LANES), F32),
            jax.ShapeDtypeStruct((batch, seq, LANES), F32),
        ],
        compiler_params=_params(40, "arbitrary"),
        name=f"attn_group{g}",
    )(qkv_g)


def _attn_out_kernel(o1_ref, o2_ref, o3_ref, l1_ref, l2_ref, l3_ref, wo_ref, x_ref, g_ref, b_ref, out_ref):
    ls = [l1_ref[0], l2_ref[0], l3_ref[0]]
    mx = jnp.maximum(jnp.maximum(ls[0], ls[1]), ls[2])
    es = [jnp.exp(l - mx) for l in ls]
    inv = float(N_GROUPS) / (es[0] + es[1] + es[2])
    acc = None
    for g, o_ref in enumerate((o1_ref, o2_ref, o3_ref)):
        w = es[g] * inv
        o_g = jnp.concatenate([o_ref[0, p] * w for p in range(SEC // LANES)], axis=1)
        part = jnp.dot(o_g.astype(BF16), wo_ref[g], preferred_element_type=F32)
        acc = part if acc is None else acc + part
    out_ref[...] = _ln(DEEPNORM_ALPHA * x_ref[...] + acc, g_ref[...], b_ref[...])


def _attn_out_prompt(os_, ls_, wo_perm, x_all, ln_g, ln_b, batch, seq):
    bm = 1024
    tiles = seq // bm
    t_all = x_all.shape[0]
    row = lambda b, i: (b * tiles + i, 0)
    const2 = lambda b, i: (0, 0)
    return pl.pallas_call(
        _attn_out_kernel,
        grid=(batch, tiles),
        in_specs=[pl.BlockSpec((1, SEC // LANES, bm, LANES), lambda b, i: (b, 0, i, 0))] * 3
        + [pl.BlockSpec((1, bm, LANES), lambda b, i: (b, i, 0))] * 3 + [
            pl.BlockSpec((N_GROUPS, SEC, D_MODEL), lambda b, i: (0, 0, 0)),
            pl.BlockSpec((bm, D_MODEL), row),
            pl.BlockSpec((1, D_MODEL), const2),
            pl.BlockSpec((1, D_MODEL), const2),
        ],
        out_specs=pl.BlockSpec((bm, D_MODEL), row),
        out_shape=jax.ShapeDtypeStruct((t_all, D_MODEL), F32),
        compiler_params=_params(40, "arbitrary", "arbitrary"),
        name="attn_out_prompt",
    )(*os_, *ls_, wo_perm, x_all, ln_g, ln_b)


def _mm_kernel(x_ref, w_ref, o_ref):
    o_ref[...] = jnp.dot(x_ref[...].astype(BF16), w_ref[...], preferred_element_type=F32)


def _mm_rows(x_all, w, row0, rows):
    n = w.shape[1]
    blk0 = row0 // rows
    return pl.pallas_call(
        _mm_kernel,
        grid=(1,),
        in_specs=[
            pl.BlockSpec((rows, D_MODEL), lambda i: (blk0, 0)),
            pl.BlockSpec((D_MODEL, n), lambda i: (0, 0)),
        ],
        out_specs=pl.BlockSpec((rows, n), lambda i: (0, 0)),
        out_shape=jax.ShapeDtypeStruct((rows, n), F32),
        compiler_params=_params(40, "arbitrary"),
        name="mm_rows",
    )(x_all, w)


def _proj_ln_kernel(a_ref, w_ref, x_ref, g_ref, b_ref, prev_ref, out_ref):
    del prev_ref
    acc = jnp.dot(a_ref[...].astype(BF16), w_ref[...], preferred_element_type=F32)
    out_ref[...] = _ln(DEEPNORM_ALPHA * x_ref[...] + acc, g_ref[...], b_ref[...])


def _proj_ln_rows(a, w, x_all, ln_g, ln_b, x_next, row0):
    rows = a.shape[0]
    blk0 = row0 // rows
    const2 = lambda i: (0, 0)
    return pl.pallas_call(
        _proj_ln_kernel,
        grid=(1,),
        in_specs=[
            pl.BlockSpec((rows, a.shape[1]), const2),
            pl.BlockSpec(w.shape, const2),
            pl.BlockSpec((rows, D_MODEL), lambda i: (blk0, 0)),
            pl.BlockSpec((1, D_MODEL), const2),
            pl.BlockSpec((1, D_MODEL), const2),
            pl.BlockSpec(memory_space=pl.ANY),
        ],
        out_specs=pl.BlockSpec((rows, D_MODEL), lambda i: (blk0, 0)),
        out_shape=jax.ShapeDtypeStruct(x_next.shape, F32),
        input_output_aliases={5: 0},
        compiler_params=_params(40, "arbitrary"),
        name="proj_ln_rows",
    )(a, w, x_all, ln_g, ln_b, x_next)


def _sgu_kernel(*refs, bm, chunk, emit_v, aliased):
    (x_ref, win_ref, bin_ref, lng_ref, lnb_ref, ws_ref, bs_ref, wout_ref, g1_ref, b1_ref) = refs[:10]
    rest = refs[10 + (1 if aliased else 0):]
    out_ref = rest[0]
    v_ref = rest[1] if emit_v else None
    ug_s = rest[-1]
    width = D_MODEL
    x = x_ref[...]
    z = jnp.dot(x.astype(BF16), win_ref[...], preferred_element_type=F32) + bin_ref[...]
    z = 0.5 * z * (1.0 + lax.erf(z * (2.0 ** -0.5)))
    v = _ln(z[:, width:], lng_ref[...], lnb_ref[...])
    if emit_v:
        v_ref[...] = v
    ii = lax.broadcasted_iota(jnp.int32, (SGU_CHUNK, SGU_CHUNK), 0)
    jj = lax.broadcasted_iota(jnp.int32, (SGU_CHUNK, SGU_CHUNK), 1)
    causal = (jj <= ii) & ((ii // chunk) == (jj // chunk))
    gw = width // SGU_GROUPS
    for gi in range(SGU_GROUPS):
        wsm = jnp.where(causal, ws_ref[gi], 0.0).astype(BF16)
        for c in range(bm // SGU_CHUNK):
            rows = slice(c * SGU_CHUNK, (c + 1) * SGU_CHUNK)
            cols = slice(gi * gw, (gi + 1) * gw)
            gate = jnp.dot(wsm, v[rows, cols].astype(BF16), preferred_element_type=F32) + bs_ref[gi]
            ug_s[rows, cols] = (z[rows, cols] * gate).astype(BF16)
    y = jnp.dot(ug_s[...], wout_ref[...], preferred_element_type=F32)
    out_ref[...] = _ln(DEEPNORM_ALPHA * x + y, g1_ref[...], b1_ref[...])


def _sgu(x_all, x_next, weights, row0, rows, bm, chunk, emit_v):
    w_in, b_in, ln_g, ln_b, ws, bs, w_out, g1, b1 = weights
    t_all = x_all.shape[0]
    blk0 = row0 // bm
    row = lambda i: (blk0 + i, 0)
    const2 = lambda i: (0, 0)
    const3 = lambda i: (0, 0, 0)
    aliased = x_next is not None
    in_specs = [
        pl.BlockSpec((bm, D_MODEL), row),
        pl.BlockSpec(w_in.shape, const2),
        pl.BlockSpec((1, 2 * D_MODEL), const2),
        pl.BlockSpec((1, D_MODEL), const2),
        pl.BlockSpec((1, D_MODEL), const2),
        pl.BlockSpec(ws.shape, const3),
        pl.BlockSpec(bs.shape, const3),
        pl.BlockSpec(w_out.shape, const2),
        pl.BlockSpec((1, D_MODEL), const2),
        pl.BlockSpec((1, D_MODEL), const2),
    ]
    args = [x_all, w_in, b_in, ln_g, ln_b, ws, bs, w_out, g1, b1]
    aliases = {}
    if aliased:
        in_specs.append(pl.BlockSpec(memory_space=pl.ANY))
        args.append(x_next)
        aliases = {10: 0}
    out_specs = [pl.BlockSpec((bm, D_MODEL), row)]
    out_shape = [jax.ShapeDtypeStruct((t_all, D_MODEL), F32)]
    if emit_v:
        out_specs.append(pl.BlockSpec((bm, D_MODEL), lambda i: (i, 0)))
        out_shape.append(jax.ShapeDtypeStruct((rows, D_MODEL), F32))
    return pl.pallas_call(
        functools.partial(_sgu_kernel, bm=bm, chunk=chunk, emit_v=emit_v, aliased=aliased),
        grid=(rows // bm,),
        in_specs=in_specs,
        out_specs=out_specs,
        out_shape=out_shape,
        scratch_shapes=[pltpu.VMEM((bm, D_MODEL), BF16)],
        input_output_aliases=aliases,
        compiler_params=_params(48, "arbitrary"),
        name="sgu",
    )(*args)


def _router_kernel(x_ref, wr_ref, br_ref, idx_ref, gate_ref, cnt_ref, *, bm, t_all):
    x = x_ref[...]
    w = wr_ref[...]
    x_hi = x.astype(BF16)
    w_hi = w.astype(BF16)
    x_lo = (x - x_hi.astype(F32)).astype(BF16)
    w_lo = (w - w_hi.astype(F32)).astype(BF16)
    logits = (jnp.dot(x_hi, w_hi, preferred_element_type=F32) + jnp.dot(x_lo, w_hi, preferred_element_type=F32)
              + jnp.dot(x_hi, w_lo, preferred_element_type=F32))
    logits = logits + br_ref[...]
    lane = lax.broadcasted_iota(jnp.int32, (bm, LANES), 1)
    lane_f = lane.astype(F32)
    row = pl.program_id(0) * bm + lax.broadcasted_iota(jnp.int32, (bm, LANES), 0)
    logits = jnp.where(row < t_all, logits, 0.0)
    logits = jnp.where(lane < N_EXPERTS, logits, -jnp.inf)
    vals, idxs = [], []
    for _ in range(TOP_K):
        m = jnp.max(logits, axis=1, keepdims=True)
        i = jnp.min(jnp.where(logits == m, lane_f, float(LANES)), axis=1, keepdims=True)
        vals.append(m)
        idxs.append(i)
        logits = jnp.where(lane_f == i, -jnp.inf, logits)
    es = [jnp.exp(v - vals[0]) for v in vals]
    inv = 1.0 / functools.reduce(lambda a, b: a + b, es)
    idx_out = jnp.zeros((bm, LANES), F32)
    gate_out = jnp.zeros((bm, LANES), F32)
    chosen = jnp.zeros((bm, LANES), F32)
    for k in range(TOP_K):
        idx_out = jnp.where(lane == k, idxs[k], idx_out)
        gate_out = jnp.where(lane == k, es[k] * inv, gate_out)
        chosen = chosen + jnp.where((lane_f == idxs[k]) & (row < t_all), 1.0, 0.0)
    idx_ref[...] = idx_out.astype(jnp.int32)
    gate_ref[...] = gate_out
    cnt_ref[0] = jnp.broadcast_to(jnp.sum(chosen, axis=0, keepdims=True), (SUB, LANES))


def _router(x_all, w_r, b_r):
    bm = MOE_TILE
    t_all = x_all.shape[0]
    n_tiles = pl.cdiv(t_all, bm)
    row = lambda i: (i, 0)
    const2 = lambda i: (0, 0)
    return pl.pallas_call(
        functools.partial(_router_kernel, bm=bm, t_all=t_all),
        grid=(n_tiles,),
        in_specs=[
            pl.BlockSpec((bm, D_MODEL), row),
            pl.BlockSpec((D_MODEL, LANES), const2),
            pl.BlockSpec((1, LANES), const2),
        ],
        out_specs=[pl.BlockSpec((bm, LANES), row), pl.BlockSpec((bm, LANES), row),
                   pl.BlockSpec((1, SUB, LANES), lambda i: (i, 0, 0))],
        out_shape=[
            jax.ShapeDtypeStruct((t_all, LANES), jnp.int32),
            jax.ShapeDtypeStruct((t_all, LANES), F32),
            jax.ShapeDtypeStruct((n_tiles, SUB, LANES), F32),
        ],
        compiler_params=_params(32, "arbitrary"),
        name="router",
    )(x_all, w_r, b_r)


def _local_slots(idx, row_valid):
    tile = idx.shape[0]
    lane = lax.broadcasted_iota(jnp.int32, (tile, LANES), 1)
    idx = jnp.where(row_valid, idx, -1)
    picks = [lane == idx[:, k:k + 1] for k in range(TOP_K)]
    chosen = functools.reduce(lambda a, b: a + b, [p.astype(F32) for p in picks])
    earlier = (lax.broadcasted_iota(jnp.int32, (tile, tile), 1)
               < lax.broadcasted_iota(jnp.int32, (tile, tile), 0)).astype(BF16)
    rank = jnp.dot(earlier, chosen.astype(BF16), preferred_element_type=F32)
    per_expert = jnp.broadcast_to(jnp.sum(chosen, axis=0, keepdims=True), (SUB, LANES))
    lower = (lax.broadcasted_iota(jnp.int32, (LANES, LANES), 0)
             < lax.broadcasted_iota(jnp.int32, (LANES, LANES), 1)).astype(BF16)
    base = jnp.dot(per_expert.astype(BF16), lower, preferred_element_type=F32)[0:1]
    place = base + rank + 1.0
    return [jnp.sum(jnp.where(p, place, 0.0), axis=1, keepdims=True) - 1.0 for p in picks]


def _segment_copies(cnt_ref, off_ref, j, hbm, buf, sem, to_hbm):
    local = 0
    for e in range(N_EXPERTS):
        cnt = cnt_ref[j * N_EXPERTS + e]
        off = off_ref[j * N_EXPERTS + e]
        for bit in range(MOE_TILE.bit_length()):
            size = 1 << bit

            @pl.when((cnt & size) != 0)
            def _(cnt=cnt, off=off, local=local, size=size):
                done = cnt & (size - 1)
                a = buf.at[pl.ds(pl.multiple_of((local + done) * SUB, SUB), size * SUB), :]
                b = hbm.at[pl.ds(pl.multiple_of((off + done) * SUB, SUB), size * SUB), :]
                (pltpu.make_async_copy(a, b, sem) if to_hbm else pltpu.make_async_copy(b, a, sem)).start()

        local = local + cnt


def _rows_to_tiles(ref, value):
    rows = value.shape[0]
    for s in range(D_MODEL // LANES):
        ref[pl.ds(s, rows, stride=SUB), :] = value[:, s * LANES:(s + 1) * LANES]


def _tiles_to_rows(ref, rows):
    return jnp.concatenate([ref[pl.ds(s, rows, stride=SUB), :] for s in range(D_MODEL // LANES)], axis=1)


def _dispatch_kernel(cnt_ref, off_ref, x_ref, idx_ref, xs_hbm, slot_ref, buf, sem, *, t_all, n_tiles):
    j = pl.program_id(0)
    par = j % 2
    tile = MOE_TILE
    row = j * tile + lax.broadcasted_iota(jnp.int32, (tile, LANES), 0)
    row_valid = row < t_all
    slots = _local_slots(idx_ref[...], row_valid)
    lane = lax.broadcasted_iota(jnp.int32, (tile, LANES), 1)
    slot_out = jnp.zeros((tile, LANES), F32)
    for k in range(TOP_K):
        slot_out = jnp.where(lane == k, slots[k], slot_out)
    slot_ref[...] = slot_out
    slots_t = slot_out.T
    pos = lax.broadcasted_iota(jnp.int32, (TOP_K * tile, tile), 0).astype(F32)
    hit = functools.reduce(jnp.logical_or, [pos == slots_t[k:k + 1, :] for k in range(TOP_K)])
    place = jnp.where(hit, 1.0, 0.0).astype(BF16)
    in_range = (j * tile + lax.broadcasted_iota(jnp.int32, (tile, D_MODEL), 0)) < t_all
    x = jnp.where(in_range, x_ref[...], 0.0).astype(BF16)
    ordered = jnp.dot(place, x, preferred_element_type=F32)
    _rows_to_tiles(buf.at[par], ordered)

    full = TOP_K * tile * SUB
    last = TOP_K * (t_all - (n_tiles - 1) * tile) * SUB

    @pl.when(j > 0)
    def _():
        pltpu.make_async_copy(buf.at[1 - par], xs_hbm.at[pl.ds(0, full), :], sem.at[0]).wait()

    _segment_copies(cnt_ref, off_ref, j, xs_hbm, buf.at[par], sem.at[0], True)

    @pl.when(j == n_tiles - 1)
    def _():
        pltpu.make_async_copy(buf.at[par, pl.ds(0, last), :], xs_hbm.at[pl.ds(0, last), :], sem.at[0]).wait()


def _dispatch(cnt, off, x_all, idx, n_slots):
    t_all = x_all.shape[0]
    n_tiles = pl.cdiv(t_all, MOE_TILE)
    row = lambda j, c, o: (j, 0)
    grid_spec = pltpu.PrefetchScalarGridSpec(
        num_scalar_prefetch=2,
        grid=(n_tiles,),
        in_specs=[pl.BlockSpec((MOE_TILE, D_MODEL), row), pl.BlockSpec((MOE_TILE, LANES), row)],
        out_specs=[pl.BlockSpec(memory_space=pl.ANY), pl.BlockSpec((MOE_TILE, LANES), row)],
        scratch_shapes=[pltpu.VMEM((2, TOP_K * MOE_TILE * SUB, LANES), F32), pltpu.SemaphoreType.DMA((1,))],
    )
    return pl.pallas_call(
        functools.partial(_dispatch_kernel, t_all=t_all, n_tiles=n_tiles),
        grid_spec=grid_spec,
        out_shape=[jax.ShapeDtypeStruct((n_slots * SUB, LANES), F32), jax.ShapeDtypeStruct((t_all, LANES), F32)],
        compiler_params=_params(40, "arbitrary"),
        name="moe_dispatch",
    )(cnt, off, x_all, idx)


def _expert_kernel(be_ref, nreal_ref, first_ref, par_ref, next_ref, xs_ref, wgu_hbm, bgu_ref, wdn_hbm, bdn_ref, ys_ref,
                   wgu_f32, wdn_f32, wgu_bf, wdn_bf, sem, *, layer):
    i = pl.program_id(0)
    ff = wdn_bf.shape[0]

    def copies(e, slot):
        return (pltpu.make_async_copy(wgu_hbm.at[layer, e], wgu_f32.at[slot], sem.at[0, slot]),
                pltpu.make_async_copy(wdn_hbm.at[layer, e], wdn_f32.at[slot], sem.at[1, slot]))

    @pl.when(i == 0)
    def _():
        for c in copies(be_ref[0], 0):
            c.start()

    @pl.when(i < nreal_ref[0])
    def _():
        @pl.when(first_ref[i] == 1)
        def _():
            slot = par_ref[i]
            for c in copies(be_ref[i], slot):
                c.wait()
            wgu_bf[...] = wgu_f32[slot].astype(BF16)
            wdn_bf[...] = wdn_f32[slot].astype(BF16)

            @pl.when(next_ref[i] >= 0)
            def _():
                for c in copies(next_ref[i], 1 - slot):
                    c.start()

        x = _tiles_to_rows(xs_ref, MOE_BLOCK).astype(BF16)
        h = jnp.dot(x, wgu_bf[...], preferred_element_type=F32) + bgu_ref[0]
        gate = jnp.minimum(h[:, :ff], SWIGLU_LIMIT)
        up = jnp.clip(h[:, ff:], -SWIGLU_LIMIT, SWIGLU_LIMIT)
        act = (up + 1.0) * gate * jax.nn.sigmoid(SWIGLU_ALPHA * gate)
        y = jnp.dot(act.astype(BF16), wdn_bf[...], preferred_element_type=F32) + bdn_ref[0]
        _rows_to_tiles(ys_ref, y)


def _experts(block_expert, n_real, run_first, run_parity, next_expert, xs, w_gu, b_gu, w_dn, b_dn, layer):
    n_blocks = xs.shape[0] // (MOE_BLOCK * SUB)
    ff = w_dn.shape[2]
    rows = lambda i, be, nr, *_: (jnp.minimum(i, nr[0] - 1), 0)
    by_expert = lambda i, be, *_: (layer, be[i], 0, 0)
    grid_spec = pltpu.PrefetchScalarGridSpec(
        num_scalar_prefetch=5,
        grid=(n_blocks,),
        in_specs=[
            pl.BlockSpec((MOE_BLOCK * SUB, LANES), rows),
            pl.BlockSpec(memory_space=pl.ANY),
            pl.BlockSpec((None, 1, 1, 2 * ff), by_expert),
            pl.BlockSpec(memory_space=pl.ANY),
            pl.BlockSpec((None, 1, 1, D_MODEL), by_expert),
        ],
        out_specs=pl.BlockSpec((MOE_BLOCK * SUB, LANES), rows),
        scratch_shapes=[
            pltpu.VMEM((2, D_MODEL, 2 * ff), F32),
            pltpu.VMEM((2, ff, D_MODEL), F32),
            pltpu.VMEM((D_MODEL, 2 * ff), BF16),
            pltpu.VMEM((ff, D_MODEL), BF16),
            pltpu.SemaphoreType.DMA((2, 2)),
        ],
    )
    return pl.pallas_call(
        functools.partial(_expert_kernel, layer=layer),
        grid_spec=grid_spec,
        out_shape=jax.ShapeDtypeStruct(xs.shape, F32),
        compiler_params=_params(56, "arbitrary"),
        name="experts",
    )(block_expert, n_real, run_first, run_parity, next_expert, xs, w_gu, b_gu, w_dn, b_dn)


def _combine_kernel(cnt_ref, off_ref, x_ref, slot_ref, gate_ref, g_ref, b_ref, ys_hbm, out_ref, buf, sem, *, t_all, n_tiles):
    j = pl.program_id(0)
    par = j % 2
    tile = MOE_TILE
    full = TOP_K * tile * SUB
    last = TOP_K * (t_all - (n_tiles - 1) * tile) * SUB

    @pl.when(j == 0)
    def _():
        _segment_copies(cnt_ref, off_ref, 0, ys_hbm, buf.at[0], sem.at[0], False)

    @pl.when(j + 1 < n_tiles)
    def _():
        _segment_copies(cnt_ref, off_ref, j + 1, ys_hbm, buf.at[1 - par], sem.at[1 - par], False)

    @pl.when(j < n_tiles - 1)
    def _():
        pltpu.make_async_copy(ys_hbm.at[pl.ds(0, full), :], buf.at[par], sem.at[par]).wait()

    @pl.when(j == n_tiles - 1)
    def _():
        pltpu.make_async_copy(ys_hbm.at[pl.ds(0, last), :], buf.at[par, pl.ds(0, last), :], sem.at[par]).wait()

    n_valid = TOP_K * jnp.minimum(tile, t_all - j * tile)
    ys = _tiles_to_rows(buf.at[par], TOP_K * tile)
    ys = jnp.where(lax.broadcasted_iota(jnp.int32, ys.shape, 0) < n_valid, ys, 0.0).astype(BF16)
    slots = slot_ref[...]
    gates = gate_ref[...]
    pos = lax.broadcasted_iota(jnp.int32, (tile, TOP_K * tile), 1).astype(F32)
    weights = jnp.zeros((tile, TOP_K * tile), F32)
    for k in range(TOP_K):
        weights = jnp.where(pos == slots[:, k:k + 1], gates[:, k:k + 1], weights)
    y = jnp.dot(weights.astype(BF16), ys, preferred_element_type=F32)
    out_ref[...] = _ln(DEEPNORM_ALPHA * x_ref[...] + y, g_ref[...], b_ref[...])


def _combine(cnt, off, x_all, slots, gates, ln_g, ln_b, ys):
    t_all = x_all.shape[0]
    n_tiles = pl.cdiv(t_all, MOE_TILE)
    row = lambda j, c, o: (j, 0)
    const2 = lambda j, c, o: (0, 0)
    grid_spec = pltpu.PrefetchScalarGridSpec(
        num_scalar_prefetch=2,
        grid=(n_tiles,),
        in_specs=[
            pl.BlockSpec((MOE_TILE, D_MODEL), row),
            pl.BlockSpec((MOE_TILE, LANES), row),
            pl.BlockSpec((MOE_TILE, LANES), row),
            pl.BlockSpec((1, D_MODEL), const2),
            pl.BlockSpec((1, D_MODEL), const2),
            pl.BlockSpec(memory_space=pl.ANY),
        ],
        out_specs=pl.BlockSpec((MOE_TILE, D_MODEL), row),
        scratch_shapes=[pltpu.VMEM((2, TOP_K * MOE_TILE * SUB, LANES), F32), pltpu.SemaphoreType.DMA((2,))],
    )
    return pl.pallas_call(
        functools.partial(_combine_kernel, t_all=t_all, n_tiles=n_tiles),
        grid_spec=grid_spec,
        out_shape=jax.ShapeDtypeStruct((t_all, D_MODEL), F32),
        compiler_params=_params(40, "arbitrary"),
        name="moe_combine",
    )(cnt, off, x_all, slots, gates, ln_g, ln_b, ys)


def _moe(x_all, w_r, b_r, w_gu, b_gu, w_dn, b_dn, ln_g, ln_b, layer):
    t_all = x_all.shape[0]
    idx, gates, tile_counts = _router(x_all, w_r, b_r)
    cnt = tile_counts[:, 0, :N_EXPERTS].astype(jnp.int32)
    counts = jnp.sum(cnt, axis=0)
    padded = (counts + MOE_BLOCK - 1) // MOE_BLOCK * MOE_BLOCK
    pad_end = jnp.cumsum(padded)
    pad_start = pad_end - padded
    off = pad_start[None, :] + jnp.cumsum(cnt, axis=0) - cnt
    n_blocks = -(-t_all * TOP_K // MOE_BLOCK) + N_EXPERTS
    block_start = jnp.arange(n_blocks, dtype=jnp.int32) * MOE_BLOCK
    block_expert = jnp.minimum(
        jnp.sum((pad_end[None, :] <= block_start[:, None]).astype(jnp.int32), axis=1), N_EXPERTS - 1)
    n_real = (pad_end[-1:] // MOE_BLOCK).astype(jnp.int32)
    block = jnp.arange(n_blocks, dtype=jnp.int32)
    prev_expert = jnp.concatenate([jnp.full((1,), -1, jnp.int32), block_expert[:-1]])
    run_first = ((block < n_real[0]) & (block_expert != prev_expert)).astype(jnp.int32)
    run_parity = (jnp.cumsum(run_first) - 1) % 2
    after_run = pad_end[block_expert] // MOE_BLOCK
    next_expert = jnp.where(after_run < n_real[0], block_expert[jnp.minimum(after_run, n_blocks - 1)], -1)
    cnt_flat, off_flat = cnt.reshape(-1), off.reshape(-1).astype(jnp.int32)
    xs, slots = _dispatch(cnt_flat, off_flat, x_all, idx, n_blocks * MOE_BLOCK)
    ys = _experts(block_expert, n_real, run_first, run_parity.astype(jnp.int32), next_expert.astype(jnp.int32),
                  xs, w_gu, b_gu, w_dn, b_dn, layer)
    return _combine(cnt_flat, off_flat, x_all, slots, gates, ln_g, ln_b, ys)


def _rope_tables(pos):
    half = HEAD_DIM // 2
    inv_freq = ROPE_THETA ** (-jnp.arange(half, dtype=F32) / half)
    ang = pos.astype(F32)[:, None] * inv_freq[None, :]
    cos = jnp.cos(ang)
    sin = jnp.sin(ang)
    cos = jnp.concatenate([cos, cos], axis=1)
    sin = jnp.concatenate([-sin, sin], axis=1)
    reps = LANES // HEAD_DIM
    return jnp.tile(cos, (1, reps)), jnp.tile(sin, (1, reps))


def _permute_qkv_weight(w_qkv):
    w3 = w_qkv.reshape(D_MODEL, 3, N_HEADS * HEAD_DIM)
    secs = []
    for part in range(3):
        for (h0, h1, _, _) in GROUPS:
            sec = w3[:, part, h0 * HEAD_DIM:h1 * HEAD_DIM]
            secs.append(jnp.pad(sec, ((0, 0), (0, SEC - sec.shape[1]))))
    return jnp.concatenate(secs, axis=1).astype(BF16)


def _permute_out_weight(w_o):
    secs = []
    for (h0, h1, _, _) in GROUPS:
        sec = w_o[h0 * HEAD_DIM:h1 * HEAD_DIM]
        secs.append(jnp.pad(sec, ((0, SEC - sec.shape[0]), (0, 0))))
    return jnp.stack(secs).astype(BF16)


def _sample_step_kernel(*refs, dec_s, aliased):
    y_ref, cos_ref, sin_ref = refs[:3]
    cache_refs = refs[3:3 + N_GROUPS]
    outs = refs[3 + N_GROUPS * (2 if aliased else 1):]
    o_ref, new_refs = outs[0], outs[1:]
    y = y_ref[0]
    cos = cos_ref[...]
    sin = sin_ref[...]
    lane = lax.broadcasted_iota(jnp.int32, (SUB, LANES), 1)
    first_half = (lane & (HEAD_DIM // 2)) == 0
    blocks_per_sec = SEC // LANES
    n_rot = 2 * N_GROUPS * blocks_per_sec
    blocks = []
    for c in range(3 * N_GROUPS * blocks_per_sec):
        blk = y[:, c * LANES:(c + 1) * LANES]
        if c < n_rot:
            swapped = jnp.where(first_half, pltpu.roll(blk, LANES - HEAD_DIM // 2, 1), pltpu.roll(blk, HEAD_DIM // 2, 1))
            blk = blk * cos + swapped * sin
        if c < n_rot // 2:
            blk = blk * (HEAD_DIM ** -0.5)
        blocks.append(blk)

    lo = lane < HEAD_DIM
    new_row = lax.broadcasted_iota(jnp.int32, (SUB, SUB), 1)
    qry_row = lax.broadcasted_iota(jnp.int32, (SUB, SUB), 0)
    pad_rows = jnp.zeros((LANES - SUB, LANES), F32)
    contract_lanes = (((1,), (1,)), ((), ()))
    o_blocks, glses = [], []
    for g in range(N_GROUPS):
        h0, h1, win, d = GROUPS[g]
        n_heads = h1 - h0
        cache_ref, new_ref = cache_refs[g], new_refs[g]
        lb = cache_ref.shape[-1]
        key = lax.broadcasted_iota(jnp.int32, (SUB, lb), 1)
        dist = lb + lax.broadcasted_iota(jnp.int32, (SUB, lb), 0) - key
        visible = ((dist & (d - 1)) == 0) & (dist <= win)
        dist_new = qry_row - new_row
        visible_new = (dist_new >= 0) & ((dist_new & (d - 1)) == 0) & (new_row < dec_s)
        tail_lane = lax.broadcasted_iota(jnp.int32, (HEAD_DIM, LANES), 1)
        lses = []
        for p in range(blocks_per_sec):
            heads = [h for h in (2 * p, 2 * p + 1) if h < n_heads]
            if not heads:
                o_blocks.append(jnp.zeros((SUB, LANES), F32))
                continue
            q2 = blocks[g * blocks_per_sec + p]
            k_new = blocks[(N_GROUPS + g) * blocks_per_sec + p]
            v_new = blocks[(2 * N_GROUPS + g) * blocks_per_sec + p]
            slabs = [[cache_ref[part, h] for h in heads] for part in range(2)]
            zero_slab = jnp.zeros((HEAD_DIM, lb), F32)
            kt2 = jnp.concatenate(slabs[0] + [zero_slab] * (2 - len(heads)), axis=0).astype(BF16)
            vt2 = jnp.concatenate(slabs[1] + [zero_slab] * (2 - len(heads)), axis=0).astype(BF16)
            q_both = jnp.concatenate([jnp.where(lo, q2, 0.0), jnp.where(lo, 0.0, q2)], axis=0).astype(BF16)
            s_both = jnp.dot(q_both, kt2, preferred_element_type=F32)
            s_new_both = lax.dot_general(q_both, k_new.astype(BF16), contract_lanes, preferred_element_type=F32)
            probs, probs_new = [], []
            for half in range(2):
                s = jnp.where(visible, s_both[half * SUB:(half + 1) * SUB], NEG_INF)
                s_new = jnp.where(visible_new, s_new_both[half * SUB:(half + 1) * SUB], NEG_INF)
                mx = jnp.maximum(jnp.max(s, axis=1, keepdims=True), jnp.max(s_new, axis=1, keepdims=True))
                pe = jnp.exp(s - mx)
                pe_new = jnp.exp(s_new - mx)
                l = jnp.sum(pe, axis=1, keepdims=True) + jnp.sum(pe_new, axis=1, keepdims=True)
                probs.append(pe / l)
                probs_new.append(pe_new / l)
                if half < len(heads):
                    lses.append(mx + jnp.log(l))
            pv = lax.dot_general(jnp.concatenate(probs, axis=0).astype(BF16), vt2, contract_lanes, preferred_element_type=F32)
            pv_new = jnp.dot(jnp.concatenate(probs_new, axis=0).astype(BF16), v_new.astype(BF16), preferred_element_type=F32)
            both = pv + pv_new
            o_blocks.append(jnp.where(lo, both[:SUB], both[SUB:]))
            for part, new in enumerate((k_new, v_new)):
                new_t = jnp.concatenate([new, pad_rows], axis=0).T
                for i, h in enumerate(heads):
                    shifted = pltpu.roll(slabs[part][i], lb - dec_s, 1)
                    tail = shifted[:, lb - LANES:]
                    feats = new_t[i * HEAD_DIM:(i + 1) * HEAD_DIM]
                    for jj in range(dec_s):
                        tail = jnp.where(tail_lane == LANES - dec_s + jj, feats[:, jj:jj + 1], tail)
                    if lb > LANES:
                        new_ref[part, h, :, :lb - LANES] = shifted[:, :lb - LANES]
                    new_ref[part, h, :, lb - LANES:] = tail
        lmax = functools.reduce(jnp.maximum, lses)
        lsum = functools.reduce(lambda a, b: a + b, [jnp.exp(l - lmax) for l in lses])
        glses.append(lmax + jnp.log(lsum) - math.log(n_heads))
    gmax = functools.reduce(jnp.maximum, glses)
    es = [jnp.exp(l - gmax) for l in glses]
    inv = float(N_GROUPS) / functools.reduce(lambda a, b: a + b, es)
    scaled = [o_blocks[g * blocks_per_sec + p] * (es[g] * inv) for g in range(N_GROUPS) for p in range(blocks_per_sec)]
    o_ref[0] = jnp.concatenate(scaled, axis=1)


def _sample_step(y_s, caches_t, new_caches_t, layer, cos_s, sin_s, dec_b, dec_s):
    n_cols = 3 * N_GROUPS * SEC
    y_pad = jnp.pad(y_s.reshape(dec_b, dec_s, n_cols), ((0, 0), (0, SUB - dec_s), (0, 0)))
    aliased = new_caches_t is not None
    cache_specs = []
    for (h0, h1, win, d), cache in zip(GROUPS, caches_t):
        assert cache.shape[-1] == win and dec_s <= LANES and win % LANES == 0
        cache_specs.append(pl.BlockSpec((None, None) + cache.shape[2:], lambda b: (layer, b, 0, 0, 0, 0)))
    in_specs = [
        pl.BlockSpec((1, SUB, n_cols), lambda b: (b, 0, 0)),
        pl.BlockSpec((SUB, LANES), lambda b: (0, 0)),
        pl.BlockSpec((SUB, LANES), lambda b: (0, 0)),
    ] + cache_specs
    args = [y_pad, cos_s, sin_s, *caches_t]
    aliases = {}
    if aliased:
        in_specs += [pl.BlockSpec(memory_space=pl.ANY)] * N_GROUPS
        aliases = {len(args) + g: 1 + g for g in range(N_GROUPS)}
        args += list(new_caches_t)
    outs = pl.pallas_call(
        functools.partial(_sample_step_kernel, dec_s=dec_s, aliased=aliased),
        grid=(dec_b,),
        in_specs=in_specs,
        out_specs=[pl.BlockSpec((1, SUB, N_GROUPS * SEC), lambda b: (b, 0, 0))] + cache_specs,
        out_shape=[jax.ShapeDtypeStruct((dec_b, SUB, N_GROUPS * SEC), F32)]
        + [jax.ShapeDtypeStruct(c.shape, F32) for c in caches_t],
        input_output_aliases=aliases,
        compiler_params=_params(52, "arbitrary"),
        name="sample_step",
    )(*args)
    return outs[0][:, :dec_s].reshape(dec_b * dec_s, N_GROUPS * SEC), outs[1:]


def _row2(a):
    return a.reshape(1, -1)


def _attn_layer(x_all, caches_t, new_caches_t, layer, w_qkv, w_o, g1, b1, cos_p, sin_p, dims):
    batch, seq, dec_b, dec_s = dims
    t_prompt = batch * seq
    t_sample = dec_b * dec_s
    w_perm = _permute_qkv_weight(w_qkv)
    wo_perm = _permute_out_weight(w_o)
    outs = _qkv_prompt(x_all, w_perm, cos_p, sin_p, batch, seq)
    qkv_groups, windows = outs[:N_GROUPS], outs[N_GROUPS:]
    os_, ls_ = [], []
    for g in range(N_GROUPS):
        o_g, l_g = _attn_group(qkv_groups[g], g, batch, seq)
        os_.append(o_g)
        ls_.append(l_g)
    x_next = _attn_out_prompt(os_, ls_, wo_perm, x_all, g1, b1, batch, seq)
    rows_p = []
    for g, (h0, h1, _, _) in enumerate(GROUPS):
        hg = h1 - h0
        halves = [windows[part * N_GROUPS + g][:, :, :hg * HEAD_DIM].reshape(batch, -1, hg, HEAD_DIM) for part in range(2)]
        rows_p.append(jnp.stack(halves, axis=2))
    y_s = _mm_rows(x_all, w_perm, t_prompt, t_sample)
    cos_s, sin_s = _rope_tables(PAST_LEN + jnp.arange(SUB, dtype=jnp.int32))
    o_s, new_caches_t = _sample_step(y_s, caches_t, new_caches_t, layer, cos_s, sin_s, dec_b, dec_s)
    x_all = _proj_ln_rows(o_s, wo_perm.reshape(N_GROUPS * SEC, D_MODEL), x_all, g1, b1, x_next, t_prompt)
    return x_all, rows_p, new_caches_t


def _sgu_layer(x_all, w_in, b_in, ln_g, ln_b, ws, bs, w_out, g1, b1, dims):
    batch, seq, dec_b, dec_s = dims
    t_prompt = batch * seq
    t_sample = dec_b * dec_s
    common = (w_in.astype(BF16), _row2(b_in), _row2(ln_g), _row2(ln_b))
    tail = (w_out.astype(BF16), g1, b1)
    bs_p = jnp.broadcast_to(bs[:, :, None], (SGU_GROUPS, SGU_CHUNK, SGU_CHUNK))
    x_next = _sgu(x_all, None, common + (ws, bs_p) + tail, 0, t_prompt, 512, SGU_CHUNK, False)[0]
    c = min(SGU_CHUNK, dec_s)
    reps = SGU_CHUNK // c
    ws_s = jnp.tile(ws[:, :c, :c], (1, reps, reps))
    bs_s = jnp.broadcast_to(jnp.tile(bs[:, :c], (1, reps))[:, :, None], (SGU_GROUPS, SGU_CHUNK, SGU_CHUNK))
    x_all, v_new = _sgu(x_all, x_next, common + (ws_s, bs_s) + tail, t_prompt, t_sample, t_sample, c, True)
    return x_all, v_new.reshape(dec_b, dec_s, D_MODEL)


def _moe_layer(x_all, w_router, b_router, w_gu, b_gu, w_dn, b_dn, ln_g, ln_b, layer):
    w_r = jnp.pad(w_router, ((0, 0), (0, LANES - N_EXPERTS)))
    b_r = jnp.pad(b_router, (0, LANES - N_EXPERTS)).reshape(1, LANES)
    n_layers = w_gu.shape[0]
    return _moe(
        x_all, w_r, b_r,
        w_gu, b_gu.reshape(n_layers, N_EXPERTS, 1, -1),
        w_dn, b_dn.reshape(n_layers, N_EXPERTS, 1, -1),
        ln_g, ln_b, layer)


def kernel(x_prompt, x_sample, cache_kv_w128, cache_kv_w512, cache_kv_w2048, attn_w_qkv, attn_w_o, sgu_w_in, sgu_b_in, sgu_ln_g, sgu_ln_b, sgu_w_s, sgu_b_s, sgu_w_out, moe_w_router, moe_b_router, moe_w_gu, moe_b_gu, moe_w_down, moe_b_down, ln1_g, ln1_b, ln2_g, ln2_b):
    batch, seq, _ = x_prompt.shape
    dec_b, dec_s, _ = x_sample.shape
    t_prompt = batch * seq
    t_sample = dec_b * dec_s
    caches = (cache_kv_w128, cache_kv_w512, cache_kv_w2048)
    x_all = jnp.concatenate([x_prompt.reshape(t_prompt, D_MODEL), x_sample.reshape(t_sample, D_MODEL)], axis=0)
    cos_p, sin_p = _rope_tables(jnp.arange(seq, dtype=jnp.int32))
    row2 = _row2
    dims = (batch, seq, dec_b, dec_s)

    kv_prompt = [[] for _ in GROUPS]
    caches_t = [c.transpose(0, 1, 3, 4, 5, 2) for c in caches]
    new_caches_t = None
    v_rows = []
    for i in range(DEPTH):
        j = i // 2
        g1, b1 = row2(ln1_g[i]), row2(ln1_b[i])
        if i % 2 == 0:
            x_all, rows_p, new_caches_t = _attn_layer(
                x_all, caches_t, new_caches_t, j, attn_w_qkv[j], attn_w_o[j], g1, b1, cos_p, sin_p, dims)
            for g in range(N_GROUPS):
                kv_prompt[g].append(rows_p[g])
        else:
            x_all, v_new = _sgu_layer(
                x_all, sgu_w_in[j], sgu_b_in[j], sgu_ln_g[j], sgu_ln_b[j], sgu_w_s[j], sgu_b_s[j], sgu_w_out[j],
                g1, b1, dims)
            v_rows.append(v_new)
        x_all = _moe_layer(
            x_all, moe_w_router[i], moe_b_router[i], moe_w_gu, moe_b_gu, moe_w_down, moe_b_down,
            row2(ln2_g[i]), row2(ln2_b[i]), i)
    y_prompt = x_all[:t_prompt].reshape(batch, seq, D_MODEL)
    y_sample = x_all[t_prompt:].reshape(dec_b, dec_s, D_MODEL)
    kv_sample = [c.transpose(0, 1, 5, 2, 3, 4) for c in new_caches_t]
    return (
        y_prompt, y_sample,
        jnp.stack(kv_prompt[0]), jnp.stack(kv_prompt[1]), jnp.stack(kv_prompt[2]),
        kv_sample[0], kv_sample[1], kv_sample[2],
        jnp.stack(v_rows),
    )
```

```python
import functools
import math

import jax
import jax.numpy as jnp
from jax import lax
from jax.experimental import pallas as pl
from jax.experimental.pallas import tpu as pltpu

F32 = jnp.float32
BF16 = jnp.bfloat16

D_MODEL = 1024
HEAD_DIM = 64
N_HEADS = D_MODEL // HEAD_DIM
GROUPS = ((0, 6, 128, 1), (6, 11, 512, 4), (11, 16, 2048, 16))
N_GROUPS = len(GROUPS)
Q_BLOCK = 128
SEC = 384
ROPE_THETA = 10000.0
NEG_INF = -1e30
PAST_LEN = 8192
SGU_GROUPS = 8
SGU_CHUNK = 128
N_EXPERTS = 32
TOP_K = 4
SWIGLU_LIMIT = 7.0
SWIGLU_ALPHA = 1.702
MOE_BLOCK = 512
MOE_TILE = 256
BIG_CHUNK_BIT = 6
SUB = 8
DEPTH = 4
DEEPNORM_ALPHA = (2 * DEPTH) ** 0.25
LN_EPS = 1e-5
LANES = 128
MIB = 1024 * 1024


def _params(vmem_mib, *semantics):
    return pltpu.CompilerParams(dimension_semantics=semantics, vmem_limit_bytes=vmem_mib * MIB)


def _ln(x, g, b):
    mu = jnp.mean(x, axis=-1, keepdims=True)
    xc = x - mu
    var = jnp.mean(xc * xc, axis=-1, keepdims=True)
    return xc * lax.rsqrt(var + LN_EPS) * g + b


def _qkv_prompt_kernel(x_ref, w_ref, cos_ref, sin_ref, o1_ref, o2_ref, o3_ref, *rest, bm):
    win_refs = rest[:2 * N_GROUPS]
    y_s = rest[-1]
    y = jnp.dot(x_ref[...].astype(BF16), w_ref[...], preferred_element_type=F32)
    cos = cos_ref[...]
    sin = sin_ref[...]
    lane = lax.broadcasted_iota(jnp.int32, (bm, LANES), 1)
    first_half = (lane & (HEAD_DIM // 2)) == 0
    blocks_per_sec = SEC // LANES
    n_rot = 2 * N_GROUPS * blocks_per_sec
    for c in range(3 * N_GROUPS * blocks_per_sec):
        blk = y[:, c * LANES:(c + 1) * LANES]
        if c < n_rot:
            swapped = jnp.where(first_half, pltpu.roll(blk, LANES - HEAD_DIM // 2, 1), pltpu.roll(blk, HEAD_DIM // 2, 1))
            blk = blk * cos + swapped * sin
        if c < n_rot // 2:
            blk = blk * (HEAD_DIM ** -0.5)
        else:
            sec, p = divmod(c - n_rot // 2, blocks_per_sec)
            win_ref = win_refs[sec]
            keep = win_ref.shape[1]
            win_ref[0, :, p * LANES:(p + 1) * LANES] = blk[bm - keep:, :]
        y_s[c] = blk
    for g, o_ref in enumerate((o1_ref, o2_ref, o3_ref)):
        d = GROUPS[g][3]
        n = bm // d
        for r in range(d):
            for part in range(3):
                for p in range(blocks_per_sec):
                    c = (part * N_GROUPS + g) * blocks_per_sec + p
                    rows = y_s[c] if d == 1 else y_s[c, pl.ds(r, n, stride=d), :]
                    o_ref[0, r, :, part * SEC + p * LANES:part * SEC + (p + 1) * LANES] = rows.astype(BF16)


def _qkv_prompt(x_all, w_perm, cos, sin, batch, seq):
    bm = 512
    tiles = seq // bm
    out_shape = [jax.ShapeDtypeStruct((batch, d, seq // d, 3 * SEC), BF16) for (_, _, _, d) in GROUPS]
    out_specs = [pl.BlockSpec((1, d, bm // d, 3 * SEC), lambda b, i: (b, 0, i, 0)) for (_, _, _, d) in GROUPS]
    for _ in range(2):
        for (_, _, win, _) in GROUPS:
            keep = min(win, seq)
            rows = min(keep, bm)
            first = tiles - keep // rows
            out_shape.append(jax.ShapeDtypeStruct((batch, keep, SEC), F32))
            out_specs.append(pl.BlockSpec((1, rows, SEC), lambda b, i, first=first: (b, jnp.maximum(i - first, 0), 0)))
    n_cols = 3 * N_GROUPS * SEC
    return pl.pallas_call(
        functools.partial(_qkv_prompt_kernel, bm=bm),
        grid=(batch, tiles),
        in_specs=[
            pl.BlockSpec((bm, D_MODEL), lambda b, i: (b * tiles + i, 0)),
            pl.BlockSpec((D_MODEL, n_cols), lambda b, i: (0, 0)),
            pl.BlockSpec((bm, LANES), lambda b, i: (i, 0)),
            pl.BlockSpec((bm, LANES), lambda b, i: (i, 0)),
        ],
        out_specs=out_specs,
        out_shape=out_shape,
        scratch_shapes=[pltpu.VMEM((n_cols // LANES, bm, LANES), F32)],
        compiler_params=_params(48, "arbitrary", "arbitrary"),
        name="qkv_prompt",
    )(x_all, w_perm, cos, sin)


def _attn_group_kernel(qkv_ref, o_ref, lse_ref, *, d, m_len, n_heads):
    nb = m_len // Q_BLOCK
    kw = min(2 * Q_BLOCK, m_len)
    lane = lax.broadcasted_iota(jnp.int32, (1, LANES), 1)
    lo = lane < HEAD_DIM
    qi = lax.broadcasted_iota(jnp.int32, (Q_BLOCK, kw), 0)
    kj = lax.broadcasted_iota(jnp.int32, (Q_BLOCK, kw), 1)

    def block(i, carry):
        r = i // nb
        n = i % nb
        ks = pl.multiple_of(jnp.maximum(n - 1, 0) * Q_BLOCK, Q_BLOCK)
        qs = pl.multiple_of(n * Q_BLOCK, Q_BLOCK)
        delta = qs - ks + qi - kj
        valid = (delta >= 0) & (delta <= Q_BLOCK)
        if d == 1:
            dst = pl.ds(qs, Q_BLOCK)
        else:
            dst = pl.ds(r + qs * d, Q_BLOCK, stride=d)
        lses = []
        for p in range(SEC // LANES):
            cols = slice(p * LANES, (p + 1) * LANES)
            q2 = qkv_ref[0, r, pl.ds(qs, Q_BLOCK), cols]
            k2 = qkv_ref[0, r, pl.ds(ks, kw), SEC + p * LANES:SEC + (p + 1) * LANES]
            v2 = qkv_ref[0, r, pl.ds(ks, kw), 2 * SEC + p * LANES:2 * SEC + (p + 1) * LANES]
            n_half = min(2, n_heads - 2 * p)
            if n_half <= 0:
                o_ref[0, p, dst, :] = jnp.zeros((Q_BLOCK, LANES), F32)
                continue
            zero = jnp.zeros_like(q2)
            q_both = jnp.concatenate([jnp.where(lo, q2, zero), jnp.where(lo, zero, q2)][:n_half], axis=0)
            s_both = lax.dot_general(q_both, k2, (((1,), (1,)), ((), ())), preferred_element_type=F32)
            probs, inv_l = [], []
            for half in range(n_half):
                s = jnp.where(valid, s_both[half * Q_BLOCK:(half + 1) * Q_BLOCK], NEG_INF)
                mx = jnp.max(s, axis=1, keepdims=True)
                pe = jnp.exp(s - mx)
                l = jnp.sum(pe, axis=1, keepdims=True)
                probs.append(pe.astype(BF16))
                inv_l.append(1.0 / l)
                lses.append(mx + jnp.log(l))
            pv = jnp.dot(jnp.concatenate(probs, axis=0), v2, preferred_element_type=F32)
            o_pair = pv[:Q_BLOCK] * inv_l[0]
            if n_half == 2:
                o_pair = jnp.where(lo, o_pair, pv[Q_BLOCK:] * inv_l[1])
            else:
                o_pair = jnp.where(lo, o_pair, 0.0)
            o_ref[0, p, dst, :] = o_pair
        lmax = functools.reduce(jnp.maximum, lses)
        lsum = functools.reduce(lambda a, b: a + b, [jnp.exp(l - lmax) for l in lses])
        glse = lmax + jnp.log(lsum) - math.log(n_heads)
        lse_ref[0, dst, :] = jnp.broadcast_to(glse, (Q_BLOCK, LANES))
        return carry

    lax.fori_loop(0, d * nb, block, 0)


def _attn_group(qkv_g, g, batch, seq):
    h0, h1, _, d = GROUPS[g]
    m_len = seq // d
    return pl.pallas_call(
        functools.partial(_attn_group_kernel, d=d, m_len=m_len, n_heads=h1 - h0),
        grid=(batch,),
        in_specs=[pl.BlockSpec((1, d, m_len, 3 * SEC), lambda b: (b, 0, 0, 0))],
        out_specs=[
            pl.BlockSpec((1, SEC // LANES, seq, LANES), lambda b: (b, 0, 0, 0)),
            pl.BlockSpec((1, seq, LANES), lambda b: (b, 0, 0)),
        ],
        out_shape=[
            jax.ShapeDtypeStruct((batch, SEC // LANES, seq, LANES), F32),
            jax.ShapeDtypeStruct((batch, seq, LANES), F32),
        ],
        compiler_params=_params(40, "arbitrary"),
        name=f"attn_group{g}",
    )(qkv_g)


def _attn_out_kernel(o1_ref, o2_ref, o3_ref, l1_ref, l2_ref, l3_ref, wo_ref, x_ref, g_ref, b_ref, out_ref):
    ls = [l1_ref[0], l2_ref[0], l3_ref[0]]
    mx = jnp.maximum(jnp.maximum(ls[0], ls[1]), ls[2])
    es = [jnp.exp(l - mx) for l in ls]
    inv = float(N_GROUPS) / (es[0] + es[1] + es[2])
    acc = None
    for g, o_ref in enumerate((o1_ref, o2_ref, o3_ref)):
        w = es[g] * inv
        o_g = jnp.concatenate([o_ref[0, p] * w for p in range(SEC // LANES)], axis=1)
        part = jnp.dot(o_g.astype(BF16), wo_ref[g], preferred_element_type=F32)
        acc = part if acc is None else acc + part
    out_ref[...] = _ln(DEEPNORM_ALPHA * x_ref[...] + acc, g_ref[...], b_ref[...])


def _attn_out_prompt(os_, ls_, wo_perm, x_src, ln_g, ln_b, batch, seq, t_all):
    bm = 1024
    tiles = seq // bm
    row = lambda b, i: (b * tiles + i, 0)
    const2 = lambda b, i: (0, 0)
    return pl.pallas_call(
        _attn_out_kernel,
        grid=(batch, tiles),
        in_specs=[pl.BlockSpec((1, SEC // LANES, bm, LANES), lambda b, i: (b, 0, i, 0))] * 3
        + [pl.BlockSpec((1, bm, LANES), lambda b, i: (b, i, 0))] * 3 + [
            pl.BlockSpec((N_GROUPS, SEC, D_MODEL), lambda b, i: (0, 0, 0)),
            pl.BlockSpec((bm, D_MODEL), row),
            pl.BlockSpec((1, D_MODEL), const2),
            pl.BlockSpec((1, D_MODEL), const2),
        ],
        out_specs=pl.BlockSpec((bm, D_MODEL), row),
        out_shape=jax.ShapeDtypeStruct((t_all, D_MODEL), F32),
        compiler_params=_params(40, "arbitrary", "arbitrary"),
        name="attn_out_prompt",
    )(*os_, *ls_, wo_perm, x_src, ln_g, ln_b)


def _mm_kernel(x_ref, w_ref, o_ref):
    o_ref[...] = jnp.dot(x_ref[...].astype(BF16), w_ref[...], preferred_element_type=F32)


def _mm_rows(x_all, w, row0, rows):
    n = w.shape[1]
    blk0 = row0 // rows
    return pl.pallas_call(
        _mm_kernel,
        grid=(1,),
        in_specs=[
            pl.BlockSpec((rows, D_MODEL), lambda i: (blk0, 0)),
            pl.BlockSpec((D_MODEL, n), lambda i: (0, 0)),
        ],
        out_specs=pl.BlockSpec((rows, n), lambda i: (0, 0)),
        out_shape=jax.ShapeDtypeStruct((rows, n), F32),
        compiler_params=_params(40, "arbitrary"),
        name="mm_rows",
    )(x_all, w)


def _proj_ln_kernel(a_ref, w_ref, x_ref, g_ref, b_ref, prev_ref, out_ref):
    del prev_ref
    acc = jnp.dot(a_ref[...].astype(BF16), w_ref[...], preferred_element_type=F32)
    out_ref[...] = _ln(DEEPNORM_ALPHA * x_ref[...] + acc, g_ref[...], b_ref[...])


def _proj_ln_rows(a, w, x_src, src_row0, ln_g, ln_b, x_next, row0):
    rows = a.shape[0]
    blk0 = row0 // rows
    src_blk0 = src_row0 // rows
    const2 = lambda i: (0, 0)
    return pl.pallas_call(
        _proj_ln_kernel,
        grid=(1,),
        in_specs=[
            pl.BlockSpec((rows, a.shape[1]), const2),
            pl.BlockSpec(w.shape, const2),
            pl.BlockSpec((rows, D_MODEL), lambda i: (src_blk0, 0)),
            pl.BlockSpec((1, D_MODEL), const2),
            pl.BlockSpec((1, D_MODEL), const2),
            pl.BlockSpec(memory_space=pl.ANY),
        ],
        out_specs=pl.BlockSpec((rows, D_MODEL), lambda i: (blk0, 0)),
        out_shape=jax.ShapeDtypeStruct(x_next.shape, F32),
        input_output_aliases={5: 0},
        compiler_params=_params(40, "arbitrary"),
        name="proj_ln_rows",
    )(a, w, x_src, ln_g, ln_b, x_next)


def _sgu_kernel(*refs, bm, chunk, emit_v, aliased):
    (x_ref, win_ref, bin_ref, lng_ref, lnb_ref, ws_ref, bs_ref, wout_ref, g1_ref, b1_ref) = refs[:10]
    rest = refs[10 + (1 if aliased else 0):]
    out_ref = rest[0]
    v_ref = rest[1] if emit_v else None
    ug_s = rest[-1]
    width = D_MODEL
    x = x_ref[...]
    z = jnp.dot(x.astype(BF16), win_ref[...], preferred_element_type=F32) + bin_ref[...]
    z = 0.5 * z * (1.0 + lax.erf(z * (2.0 ** -0.5)))
    v = _ln(z[:, width:], lng_ref[...], lnb_ref[...])
    if emit_v:
        v_ref[...] = v
    ii = lax.broadcasted_iota(jnp.int32, (SGU_CHUNK, SGU_CHUNK), 0)
    jj = lax.broadcasted_iota(jnp.int32, (SGU_CHUNK, SGU_CHUNK), 1)
    causal = (jj <= ii) & ((ii // chunk) == (jj // chunk))
    gw = width // SGU_GROUPS
    for gi in range(SGU_GROUPS):
        wsm = jnp.where(causal, ws_ref[gi], 0.0).astype(BF16)
        for c in range(bm // SGU_CHUNK):
            rows = slice(c * SGU_CHUNK, (c + 1) * SGU_CHUNK)
            cols = slice(gi * gw, (gi + 1) * gw)
            gate = jnp.dot(wsm, v[rows, cols].astype(BF16), preferred_element_type=F32) + bs_ref[gi]
            ug_s[rows, cols] = (z[rows, cols] * gate).astype(BF16)
    y = jnp.dot(ug_s[...], wout_ref[...], preferred_element_type=F32)
    out_ref[...] = _ln(DEEPNORM_ALPHA * x + y, g1_ref[...], b1_ref[...])


def _sgu(x_all, x_next, weights, row0, rows, bm, chunk, emit_v):
    w_in, b_in, ln_g, ln_b, ws, bs, w_out, g1, b1 = weights
    t_all = x_all.shape[0]
    blk0 = row0 // bm
    row = lambda i: (blk0 + i, 0)
    const2 = lambda i: (0, 0)
    const3 = lambda i: (0, 0, 0)
    aliased = x_next is not None
    in_specs = [
        pl.BlockSpec((bm, D_MODEL), row),
        pl.BlockSpec(w_in.shape, const2),
        pl.BlockSpec((1, 2 * D_MODEL), const2),
        pl.BlockSpec((1, D_MODEL), const2),
        pl.BlockSpec((1, D_MODEL), const2),
        pl.BlockSpec(ws.shape, const3),
        pl.BlockSpec(bs.shape, const3),
        pl.BlockSpec(w_out.shape, const2),
        pl.BlockSpec((1, D_MODEL), const2),
        pl.BlockSpec((1, D_MODEL), const2),
    ]
    args = [x_all, w_in, b_in, ln_g, ln_b, ws, bs, w_out, g1, b1]
    aliases = {}
    if aliased:
        in_specs.append(pl.BlockSpec(memory_space=pl.ANY))
        args.append(x_next)
        aliases = {10: 0}
    out_specs = [pl.BlockSpec((bm, D_MODEL), row)]
    out_shape = [jax.ShapeDtypeStruct((t_all, D_MODEL), F32)]
    if emit_v:
        out_specs.append(pl.BlockSpec((bm, D_MODEL), lambda i: (i, 0)))
        out_shape.append(jax.ShapeDtypeStruct((rows, D_MODEL), F32))
    return pl.pallas_call(
        functools.partial(_sgu_kernel, bm=bm, chunk=chunk, emit_v=emit_v, aliased=aliased),
        grid=(rows // bm,),
        in_specs=in_specs,
        out_specs=out_specs,
        out_shape=out_shape,
        scratch_shapes=[pltpu.VMEM((bm, D_MODEL), BF16)],
        input_output_aliases=aliases,
        compiler_params=_params(48, "arbitrary"),
        name="sgu",
    )(*args)


def _router_kernel(x_ref, wr_ref, br_ref, idx_ref, gate_ref, cnt_ref, *, bm, t_all):
    x = x_ref[...]
    w = wr_ref[...]
    x_hi = x.astype(BF16)
    w_hi = w.astype(BF16)
    x_lo = (x - x_hi.astype(F32)).astype(BF16)
    w_lo = (w - w_hi.astype(F32)).astype(BF16)
    logits = (jnp.dot(x_hi, w_hi, preferred_element_type=F32) + jnp.dot(x_lo, w_hi, preferred_element_type=F32)
              + jnp.dot(x_hi, w_lo, preferred_element_type=F32))
    logits = logits + br_ref[...]
    lane = lax.broadcasted_iota(jnp.int32, (bm, LANES), 1)
    lane_f = lane.astype(F32)
    row = pl.program_id(0) * bm + lax.broadcasted_iota(jnp.int32, (bm, LANES), 0)
    logits = jnp.where(row < t_all, logits, 0.0)
    logits = jnp.where(lane < N_EXPERTS, logits, -jnp.inf)
    vals, idxs = [], []
    for _ in range(TOP_K):
        m = jnp.max(logits, axis=1, keepdims=True)
        i = jnp.min(jnp.where(logits == m, lane_f, float(LANES)), axis=1, keepdims=True)
        vals.append(m)
        idxs.append(i)
        logits = jnp.where(lane_f == i, -jnp.inf, logits)
    es = [jnp.exp(v - vals[0]) for v in vals]
    inv = 1.0 / functools.reduce(lambda a, b: a + b, es)
    idx_out = jnp.zeros((bm, LANES), F32)
    gate_out = jnp.zeros((bm, LANES), F32)
    chosen = jnp.zeros((bm, LANES), F32)
    for k in range(TOP_K):
        idx_out = jnp.where(lane == k, idxs[k], idx_out)
        gate_out = jnp.where(lane == k, es[k] * inv, gate_out)
        chosen = chosen + jnp.where((lane_f == idxs[k]) & (row < t_all), 1.0, 0.0)
    idx_ref[...] = idx_out.astype(jnp.int32)
    gate_ref[...] = gate_out
    cnt_ref[0] = jnp.broadcast_to(jnp.sum(chosen, axis=0, keepdims=True), (SUB, LANES))


def _router(x_all, w_r, b_r):
    bm = MOE_TILE
    t_all = x_all.shape[0]
    n_tiles = pl.cdiv(t_all, bm)
    row = lambda i: (i, 0)
    const2 = lambda i: (0, 0)
    return pl.pallas_call(
        functools.partial(_router_kernel, bm=bm, t_all=t_all),
        grid=(n_tiles,),
        in_specs=[
            pl.BlockSpec((bm, D_MODEL), row),
            pl.BlockSpec((D_MODEL, LANES), const2),
            pl.BlockSpec((1, LANES), const2),
        ],
        out_specs=[pl.BlockSpec((bm, LANES), row), pl.BlockSpec((bm, LANES), row),
                   pl.BlockSpec((1, SUB, LANES), lambda i: (i, 0, 0))],
        out_shape=[
            jax.ShapeDtypeStruct((t_all, LANES), jnp.int32),
            jax.ShapeDtypeStruct((t_all, LANES), F32),
            jax.ShapeDtypeStruct((n_tiles, SUB, LANES), F32),
        ],
        compiler_params=_params(32, "arbitrary"),
        name="router",
    )(x_all, w_r, b_r)


def _local_slots(idx, row_valid):
    tile = idx.shape[0]
    lane = lax.broadcasted_iota(jnp.int32, (tile, LANES), 1)
    idx = jnp.where(row_valid, idx, -1)
    picks = [lane == idx[:, k:k + 1] for k in range(TOP_K)]
    chosen = functools.reduce(lambda a, b: a + b, [p.astype(F32) for p in picks])
    earlier = (lax.broadcasted_iota(jnp.int32, (tile, tile), 1)
               < lax.broadcasted_iota(jnp.int32, (tile, tile), 0)).astype(BF16)
    rank = jnp.dot(earlier, chosen.astype(BF16), preferred_element_type=F32)
    per_expert = jnp.broadcast_to(jnp.sum(chosen, axis=0, keepdims=True), (SUB, LANES))
    lower = (lax.broadcasted_iota(jnp.int32, (LANES, LANES), 0)
             < lax.broadcasted_iota(jnp.int32, (LANES, LANES), 1)).astype(BF16)
    base = jnp.dot(per_expert.astype(BF16), lower, preferred_element_type=F32)[0:1]
    place = base + rank + 1.0
    return [jnp.sum(jnp.where(p, place, 0.0), axis=1, keepdims=True) - 1.0 for p in picks]


def _segment_copies(cnt_ref, off_ref, j, hbm, buf, sem, to_hbm):
    local = 0
    for e in range(N_EXPERTS):
        cnt = cnt_ref[j * N_EXPERTS + e]
        off = off_ref[j * N_EXPERTS + e]
        def chunks(bits, cnt=cnt, off=off, local=local):
            for bit in bits:
                size = 1 << bit

                @pl.when((cnt & size) != 0)
                def _(size=size):
                    done = cnt & (size - 1)
                    a = buf.at[pl.ds(pl.multiple_of((local + done) * SUB, SUB), size * SUB), :]
                    b = hbm.at[pl.ds(pl.multiple_of((off + done) * SUB, SUB), size * SUB), :]
                    (pltpu.make_async_copy(a, b, sem) if to_hbm else pltpu.make_async_copy(b, a, sem)).start()

        n_bits = MOE_TILE.bit_length()
        chunks(range(BIG_CHUNK_BIT))
        pl.when(cnt >= (1 << BIG_CHUNK_BIT))(functools.partial(chunks, range(BIG_CHUNK_BIT, n_bits)))
        local = local + cnt


def _rows_to_tiles(ref, value):
    rows = value.shape[0]
    for s in range(D_MODEL // LANES):
        ref[pl.ds(s, rows, stride=SUB), :] = value[:, s * LANES:(s + 1) * LANES]


def _tiles_to_rows(ref, rows):
    return jnp.concatenate([ref[pl.ds(s, rows, stride=SUB), :] for s in range(D_MODEL // LANES)], axis=1)


def _dispatch_kernel(cnt_ref, off_ref, x_ref, idx_ref, xs_hbm, slot_ref, buf, sem, *, t_all, n_tiles):
    j = pl.program_id(0)
    par = j % 2
    tile = MOE_TILE
    row = j * tile + lax.broadcasted_iota(jnp.int32, (tile, LANES), 0)
    row_valid = row < t_all
    slots = _local_slots(idx_ref[...], row_valid)
    lane = lax.broadcasted_iota(jnp.int32, (tile, LANES), 1)
    slot_out = jnp.zeros((tile, LANES), F32)
    for k in range(TOP_K):
        slot_out = jnp.where(lane == k, slots[k], slot_out)
    slot_ref[...] = slot_out
    slots_t = slot_out.T
    pos = lax.broadcasted_iota(jnp.int32, (TOP_K * tile, tile), 0).astype(F32)
    hit = functools.reduce(jnp.logical_or, [pos == slots_t[k:k + 1, :] for k in range(TOP_K)])
    place = jnp.where(hit, 1.0, 0.0).astype(BF16)
    in_range = (j * tile + lax.broadcasted_iota(jnp.int32, (tile, D_MODEL), 0)) < t_all
    x = jnp.where(in_range, x_ref[...], 0.0).astype(BF16)
    ordered = jnp.dot(place, x, preferred_element_type=F32)
    _rows_to_tiles(buf.at[par], ordered)

    full = TOP_K * tile * SUB
    last = TOP_K * (t_all - (n_tiles - 1) * tile) * SUB

    @pl.when(j > 0)
    def _():
        pltpu.make_async_copy(buf.at[1 - par], xs_hbm.at[pl.ds(0, full), :], sem.at[0]).wait()

    _segment_copies(cnt_ref, off_ref, j, xs_hbm, buf.at[par], sem.at[0], True)

    @pl.when(j == n_tiles - 1)
    def _():
        pltpu.make_async_copy(buf.at[par, pl.ds(0, last), :], xs_hbm.at[pl.ds(0, last), :], sem.at[0]).wait()


def _dispatch(cnt, off, x_all, idx, n_slots):
    t_all = x_all.shape[0]
    n_tiles = pl.cdiv(t_all, MOE_TILE)
    row = lambda j, c, o: (j, 0)
    grid_spec = pltpu.PrefetchScalarGridSpec(
        num_scalar_prefetch=2,
        grid=(n_tiles,),
        in_specs=[pl.BlockSpec((MOE_TILE, D_MODEL), row), pl.BlockSpec((MOE_TILE, LANES), row)],
        out_specs=[pl.BlockSpec(memory_space=pl.ANY), pl.BlockSpec((MOE_TILE, LANES), row)],
        scratch_shapes=[pltpu.VMEM((2, TOP_K * MOE_TILE * SUB, LANES), F32), pltpu.SemaphoreType.DMA((1,))],
    )
    return pl.pallas_call(
        functools.partial(_dispatch_kernel, t_all=t_all, n_tiles=n_tiles),
        grid_spec=grid_spec,
        out_shape=[jax.ShapeDtypeStruct((n_slots * SUB, LANES), F32), jax.ShapeDtypeStruct((t_all, LANES), F32)],
        compiler_params=_params(40, "arbitrary"),
        name="moe_dispatch",
    )(cnt, off, x_all, idx)


def _expert_kernel(be_ref, nreal_ref, first_ref, par_ref, next_ref, xs_ref, wgu_hbm, bgu_ref, wdn_hbm, bdn_ref, ys_ref,
                   wgu_f32, wdn_f32, wgu_bf, wdn_bf, sem, *, layer):
    i = pl.program_id(0)
    ff = wdn_bf.shape[0]

    def copies(e, slot):
        return (pltpu.make_async_copy(wgu_hbm.at[layer, e], wgu_f32.at[slot], sem.at[0, slot]),
                pltpu.make_async_copy(wdn_hbm.at[layer, e], wdn_f32.at[slot], sem.at[1, slot]))

    @pl.when(i == 0)
    def _():
        for c in copies(be_ref[0], 0):
            c.start()

    @pl.when(i < nreal_ref[0])
    def _():
        @pl.when(first_ref[i] == 1)
        def _():
            slot = par_ref[i]
            for c in copies(be_ref[i], slot):
                c.wait()
            wgu_bf[...] = wgu_f32[slot].astype(BF16)
            wdn_bf[...] = wdn_f32[slot].astype(BF16)

            @pl.when(next_ref[i] >= 0)
            def _():
                for c in copies(next_ref[i], 1 - slot):
                    c.start()

        x = _tiles_to_rows(xs_ref, MOE_BLOCK).astype(BF16)
        h = jnp.dot(x, wgu_bf[...], preferred_element_type=F32) + bgu_ref[0]
        gate = jnp.minimum(h[:, :ff], SWIGLU_LIMIT)
        up = jnp.clip(h[:, ff:], -SWIGLU_LIMIT, SWIGLU_LIMIT)
        act = (up + 1.0) * gate * jax.nn.sigmoid(SWIGLU_ALPHA * gate)
        y = jnp.dot(act.astype(BF16), wdn_bf[...], preferred_element_type=F32) + bdn_ref[0]
        _rows_to_tiles(ys_ref, y)


def _experts(block_expert, n_real, run_first, run_parity, next_expert, xs, w_gu, b_gu, w_dn, b_dn, layer):
    n_blocks = xs.shape[0] // (MOE_BLOCK * SUB)
    ff = w_dn.shape[2]
    rows = lambda i, be, nr, *_: (jnp.minimum(i, nr[0] - 1), 0)
    by_expert = lambda i, be, *_: (layer, be[i], 0, 0)
    grid_spec = pltpu.PrefetchScalarGridSpec(
        num_scalar_prefetch=5,
        grid=(n_blocks,),
        in_specs=[
            pl.BlockSpec((MOE_BLOCK * SUB, LANES), rows),
            pl.BlockSpec(memory_space=pl.ANY),
            pl.BlockSpec((None, 1, 1, 2 * ff), by_expert),
            pl.BlockSpec(memory_space=pl.ANY),
            pl.BlockSpec((None, 1, 1, D_MODEL), by_expert),
        ],
        out_specs=pl.BlockSpec((MOE_BLOCK * SUB, LANES), rows),
        scratch_shapes=[
            pltpu.VMEM((2, D_MODEL, 2 * ff), F32),
            pltpu.VMEM((2, ff, D_MODEL), F32),
            pltpu.VMEM((D_MODEL, 2 * ff), BF16),
            pltpu.VMEM((ff, D_MODEL), BF16),
            pltpu.SemaphoreType.DMA((2, 2)),
        ],
    )
    return pl.pallas_call(
        functools.partial(_expert_kernel, layer=layer),
        grid_spec=grid_spec,
        out_shape=jax.ShapeDtypeStruct(xs.shape, F32),
        compiler_params=_params(56, "arbitrary"),
        name="experts",
    )(block_expert, n_real, run_first, run_parity, next_expert, xs, w_gu, b_gu, w_dn, b_dn)


def _combine_kernel(cnt_ref, off_ref, x_ref, slot_ref, gate_ref, g_ref, b_ref, ys_hbm, out_ref, buf, sem, *, t_all, n_tiles):
    j = pl.program_id(0)
    par = j % 2
    tile = MOE_TILE
    full = TOP_K * tile * SUB
    last = TOP_K * (t_all - (n_tiles - 1) * tile) * SUB

    @pl.when(j == 0)
    def _():
        _segment_copies(cnt_ref, off_ref, 0, ys_hbm, buf.at[0], sem.at[0], False)

    @pl.when(j + 1 < n_tiles)
    def _():
        _segment_copies(cnt_ref, off_ref, j + 1, ys_hbm, buf.at[1 - par], sem.at[1 - par], False)

    @pl.when(j < n_tiles - 1)
    def _():
        pltpu.make_async_copy(ys_hbm.at[pl.ds(0, full), :], buf.at[par], sem.at[par]).wait()

    @pl.when(j == n_tiles - 1)
    def _():
        pltpu.make_async_copy(ys_hbm.at[pl.ds(0, last), :], buf.at[par, pl.ds(0, last), :], sem.at[par]).wait()

    n_valid = TOP_K * jnp.minimum(tile, t_all - j * tile)
    ys = _tiles_to_rows(buf.at[par], TOP_K * tile)
    ys = jnp.where(lax.broadcasted_iota(jnp.int32, ys.shape, 0) < n_valid, ys, 0.0).astype(BF16)
    slots = slot_ref[...]
    gates = gate_ref[...]
    pos = lax.broadcasted_iota(jnp.int32, (tile, TOP_K * tile), 1).astype(F32)
    weights = jnp.zeros((tile, TOP_K * tile), F32)
    for k in range(TOP_K):
        weights = jnp.where(pos == slots[:, k:k + 1], gates[:, k:k + 1], weights)
    y = jnp.dot(weights.astype(BF16), ys, preferred_element_type=F32)
    out_ref[...] = _ln(DEEPNORM_ALPHA * x_ref[...] + y, g_ref[...], b_ref[...])


def _combine(cnt, off, x_all, slots, gates, ln_g, ln_b, ys):
    t_all = x_all.shape[0]
    n_tiles = pl.cdiv(t_all, MOE_TILE)
    row = lambda j, c, o: (j, 0)
    const2 = lambda j, c, o: (0, 0)
    grid_spec = pltpu.PrefetchScalarGridSpec(
        num_scalar_prefetch=2,
        grid=(n_tiles,),
        in_specs=[
            pl.BlockSpec((MOE_TILE, D_MODEL), row),
            pl.BlockSpec((MOE_TILE, LANES), row),
            pl.BlockSpec((MOE_TILE, LANES), row),
            pl.BlockSpec((1, D_MODEL), const2),
            pl.BlockSpec((1, D_MODEL), const2),
            pl.BlockSpec(memory_space=pl.ANY),
        ],
        out_specs=pl.BlockSpec((MOE_TILE, D_MODEL), row),
        scratch_shapes=[pltpu.VMEM((2, TOP_K * MOE_TILE * SUB, LANES), F32), pltpu.SemaphoreType.DMA((2,))],
    )
    return pl.pallas_call(
        functools.partial(_combine_kernel, t_all=t_all, n_tiles=n_tiles),
        grid_spec=grid_spec,
        out_shape=jax.ShapeDtypeStruct((t_all, D_MODEL), F32),
        compiler_params=_params(40, "arbitrary"),
        name="moe_combine",
    )(cnt, off, x_all, slots, gates, ln_g, ln_b, ys)


def _moe(x_all, w_r, b_r, w_gu, b_gu, w_dn, b_dn, ln_g, ln_b, layer):
    t_all = x_all.shape[0]
    idx, gates, tile_counts = _router(x_all, w_r, b_r)
    cnt = tile_counts[:, 0, :N_EXPERTS].astype(jnp.int32)
    counts = jnp.sum(cnt, axis=0)
    padded = (counts + MOE_BLOCK - 1) // MOE_BLOCK * MOE_BLOCK
    pad_end = jnp.cumsum(padded)
    pad_start = pad_end - padded
    off = pad_start[None, :] + jnp.cumsum(cnt, axis=0) - cnt
    n_blocks = -(-t_all * TOP_K // MOE_BLOCK) + N_EXPERTS
    block_start = jnp.arange(n_blocks, dtype=jnp.int32) * MOE_BLOCK
    block_expert = jnp.minimum(
        jnp.sum((pad_end[None, :] <= block_start[:, None]).astype(jnp.int32), axis=1), N_EXPERTS - 1)
    n_real = (pad_end[-1:] // MOE_BLOCK).astype(jnp.int32)
    block = jnp.arange(n_blocks, dtype=jnp.int32)
    prev_expert = jnp.concatenate([jnp.full((1,), -1, jnp.int32), block_expert[:-1]])
    run_first = ((block < n_real[0]) & (block_expert != prev_expert)).astype(jnp.int32)
    run_parity = (jnp.cumsum(run_first) - 1) % 2
    after_run = pad_end[block_expert] // MOE_BLOCK
    next_expert = jnp.where(after_run < n_real[0], block_expert[jnp.minimum(after_run, n_blocks - 1)], -1)
    cnt_flat, off_flat = cnt.reshape(-1), off.reshape(-1).astype(jnp.int32)
    xs, slots = _dispatch(cnt_flat, off_flat, x_all, idx, n_blocks * MOE_BLOCK)
    ys = _experts(block_expert, n_real, run_first, run_parity.astype(jnp.int32), next_expert.astype(jnp.int32),
                  xs, w_gu, b_gu, w_dn, b_dn, layer)
    return _combine(cnt_flat, off_flat, x_all, slots, gates, ln_g, ln_b, ys)


def _rope_tables(pos):
    half = HEAD_DIM // 2
    inv_freq = ROPE_THETA ** (-jnp.arange(half, dtype=F32) / half)
    ang = pos.astype(F32)[:, None] * inv_freq[None, :]
    cos = jnp.cos(ang)
    sin = jnp.sin(ang)
    cos = jnp.concatenate([cos, cos], axis=1)
    sin = jnp.concatenate([-sin, sin], axis=1)
    reps = LANES // HEAD_DIM
    return jnp.tile(cos, (1, reps)), jnp.tile(sin, (1, reps))


def _permute_qkv_weight(w_qkv):
    w3 = w_qkv.reshape(D_MODEL, 3, N_HEADS * HEAD_DIM)
    secs = []
    for part in range(3):
        for (h0, h1, _, _) in GROUPS:
            sec = w3[:, part, h0 * HEAD_DIM:h1 * HEAD_DIM]
            secs.append(jnp.pad(sec, ((0, 0), (0, SEC - sec.shape[1]))))
    return jnp.concatenate(secs, axis=1).astype(BF16)


def _permute_out_weight(w_o):
    secs = []
    for (h0, h1, _, _) in GROUPS:
        sec = w_o[h0 * HEAD_DIM:h1 * HEAD_DIM]
        secs.append(jnp.pad(sec, ((0, SEC - sec.shape[0]), (0, 0))))
    return jnp.stack(secs).astype(BF16)


def _sample_step_kernel(*refs, dec_s, aliased):
    y_ref, cos_ref, sin_ref = refs[:3]
    cache_refs = refs[3:3 + N_GROUPS]
    outs = refs[3 + N_GROUPS * (2 if aliased else 1):]
    o_ref, new_refs = outs[0], outs[1:]
    y = y_ref[0]
    cos = cos_ref[...]
    sin = sin_ref[...]
    lane = lax.broadcasted_iota(jnp.int32, (SUB, LANES), 1)
    first_half = (lane & (HEAD_DIM // 2)) == 0
    blocks_per_sec = SEC // LANES
    n_rot = 2 * N_GROUPS * blocks_per_sec
    blocks = []
    for c in range(3 * N_GROUPS * blocks_per_sec):
        blk = y[:, c * LANES:(c + 1) * LANES]
        if c < n_rot:
            swapped = jnp.where(first_half, pltpu.roll(blk, LANES - HEAD_DIM // 2, 1), pltpu.roll(blk, HEAD_DIM // 2, 1))
            blk = blk * cos + swapped * sin
        if c < n_rot // 2:
            blk = blk * (HEAD_DIM ** -0.5)
        blocks.append(blk)

    lo = lane < HEAD_DIM
    new_row = lax.broadcasted_iota(jnp.int32, (SUB, SUB), 1)
    qry_row = lax.broadcasted_iota(jnp.int32, (SUB, SUB), 0)
    pad_rows = jnp.zeros((LANES - SUB, LANES), F32)
    contract_lanes = (((1,), (1,)), ((), ()))
    o_blocks, glses = [], []
    for g in range(N_GROUPS):
        h0, h1, win, d = GROUPS[g]
        n_heads = h1 - h0
        cache_ref, new_ref = cache_refs[g], new_refs[g]
        lb = cache_ref.shape[-1]
        key = lax.broadcasted_iota(jnp.int32, (SUB, lb), 1)
        dist = lb + lax.broadcasted_iota(jnp.int32, (SUB, lb), 0) - key
        visible = ((dist & (d - 1)) == 0) & (dist <= win)
        dist_new = qry_row - new_row
        visible_new = (dist_new >= 0) & ((dist_new & (d - 1)) == 0) & (new_row < dec_s)
        tail_lane = lax.broadcasted_iota(jnp.int32, (HEAD_DIM, LANES), 1)
        lses = []
        for p in range(blocks_per_sec):
            heads = [h for h in (2 * p, 2 * p + 1) if h < n_heads]
            if not heads:
                o_blocks.append(jnp.zeros((SUB, LANES), F32))
                continue
            q2 = blocks[g * blocks_per_sec + p]
            k_new = blocks[(N_GROUPS + g) * blocks_per_sec + p]
            v_new = blocks[(2 * N_GROUPS + g) * blocks_per_sec + p]
            slabs = [[cache_ref[part, h] for h in heads] for part in range(2)]
            zero_slab = jnp.zeros((HEAD_DIM, lb), F32)
            kt2 = jnp.concatenate(slabs[0] + [zero_slab] * (2 - len(heads)), axis=0).astype(BF16)
            vt2 = jnp.concatenate(slabs[1] + [zero_slab] * (2 - len(heads)), axis=0).astype(BF16)
            q_both = jnp.concatenate([jnp.where(lo, q2, 0.0), jnp.where(lo, 0.0, q2)], axis=0).astype(BF16)
            s_both = jnp.dot(q_both, kt2, preferred_element_type=F32)
            s_new_both = lax.dot_general(q_both, k_new.astype(BF16), contract_lanes, preferred_element_type=F32)
            probs, probs_new = [], []
            for half in range(2):
                s = jnp.where(visible, s_both[half * SUB:(half + 1) * SUB], NEG_INF)
                s_new = jnp.where(visible_new, s_new_both[half * SUB:(half + 1) * SUB], NEG_INF)
                mx = jnp.maximum(jnp.max(s, axis=1, keepdims=True), jnp.max(s_new, axis=1, keepdims=True))
                pe = jnp.exp(s - mx)
                pe_new = jnp.exp(s_new - mx)
                l = jnp.sum(pe, axis=1, keepdims=True) + jnp.sum(pe_new, axis=1, keepdims=True)
                probs.append(pe / l)
                probs_new.append(pe_new / l)
                if half < len(heads):
                    lses.append(mx + jnp.log(l))
            pv = lax.dot_general(jnp.concatenate(probs, axis=0).astype(BF16), vt2, contract_lanes, preferred_element_type=F32)
            pv_new = jnp.dot(jnp.concatenate(probs_new, axis=0).astype(BF16), v_new.astype(BF16), preferred_element_type=F32)
            both = pv + pv_new
            o_blocks.append(jnp.where(lo, both[:SUB], both[SUB:]))
            for part, new in enumerate((k_new, v_new)):
                new_t = jnp.concatenate([new, pad_rows], axis=0).T
                for i, h in enumerate(heads):
                    shifted = pltpu.roll(slabs[part][i], lb - dec_s, 1)
                    tail = shifted[:, lb - LANES:]
                    feats = new_t[i * HEAD_DIM:(i + 1) * HEAD_DIM]
                    for jj in range(dec_s):
                        tail = jnp.where(tail_lane == LANES - dec_s + jj, feats[:, jj:jj + 1], tail)
                    if lb > LANES:
                        new_ref[part, h, :, :lb - LANES] = shifted[:, :lb - LANES]
                    new_ref[part, h, :, lb - LANES:] = tail
        lmax = functools.reduce(jnp.maximum, lses)
        lsum = functools.reduce(lambda a, b: a + b, [jnp.exp(l - lmax) for l in lses])
        glses.append(lmax + jnp.log(lsum) - math.log(n_heads))
    gmax = functools.reduce(jnp.maximum, glses)
    es = [jnp.exp(l - gmax) for l in glses]
    inv = float(N_GROUPS) / functools.reduce(lambda a, b: a + b, es)
    scaled = [o_blocks[g * blocks_per_sec + p] * (es[g] * inv) for g in range(N_GROUPS) for p in range(blocks_per_sec)]
    o_ref[0] = jnp.concatenate(scaled, axis=1)


def _sample_step(y_s, caches_t, new_caches_t, layer, cos_s, sin_s, dec_b, dec_s):
    n_cols = 3 * N_GROUPS * SEC
    y_pad = jnp.pad(y_s.reshape(dec_b, dec_s, n_cols), ((0, 0), (0, SUB - dec_s), (0, 0)))
    aliased = new_caches_t is not None
    cache_specs = []
    for (h0, h1, win, d), cache in zip(GROUPS, caches_t):
        assert cache.shape[-1] == win and dec_s <= LANES and win % LANES == 0
        cache_specs.append(pl.BlockSpec((None, None) + cache.shape[2:], lambda b: (layer, b, 0, 0, 0, 0)))
    in_specs = [
        pl.BlockSpec((1, SUB, n_cols), lambda b: (b, 0, 0)),
        pl.BlockSpec((SUB, LANES), lambda b: (0, 0)),
        pl.BlockSpec((SUB, LANES), lambda b: (0, 0)),
    ] + cache_specs
    args = [y_pad, cos_s, sin_s, *caches_t]
    aliases = {}
    if aliased:
        in_specs += [pl.BlockSpec(memory_space=pl.ANY)] * N_GROUPS
        aliases = {len(args) + g: 1 + g for g in range(N_GROUPS)}
        args += list(new_caches_t)
    outs = pl.pallas_call(
        functools.partial(_sample_step_kernel, dec_s=dec_s, aliased=aliased),
        grid=(dec_b,),
        in_specs=in_specs,
        out_specs=[pl.BlockSpec((1, SUB, N_GROUPS * SEC), lambda b: (b, 0, 0))] + cache_specs,
        out_shape=[jax.ShapeDtypeStruct((dec_b, SUB, N_GROUPS * SEC), F32)]
        + [jax.ShapeDtypeStruct(c.shape, F32) for c in caches_t],
        input_output_aliases=aliases,
        compiler_params=_params(52, "arbitrary"),
        name="sample_step",
    )(*args)
    return outs[0][:, :dec_s].reshape(dec_b * dec_s, N_GROUPS * SEC), outs[1:]


def _row2(a):
    return a.reshape(1, -1)


def _attn_layer(x_p, x_s, s_row0, caches_t, new_caches_t, layer, w_qkv, w_o, g1, b1, cos_p, sin_p, dims):
    batch, seq, dec_b, dec_s = dims
    t_prompt = batch * seq
    t_sample = dec_b * dec_s
    w_perm = _permute_qkv_weight(w_qkv)
    wo_perm = _permute_out_weight(w_o)
    outs = _qkv_prompt(x_p, w_perm, cos_p, sin_p, batch, seq)
    qkv_groups, windows = outs[:N_GROUPS], outs[N_GROUPS:]
    os_, ls_ = [], []
    for g in range(N_GROUPS):
        o_g, l_g = _attn_group(qkv_groups[g], g, batch, seq)
        os_.append(o_g)
        ls_.append(l_g)
    x_next = _attn_out_prompt(os_, ls_, wo_perm, x_p, g1, b1, batch, seq, t_prompt + t_sample)
    rows_p = []
    for g, (h0, h1, _, _) in enumerate(GROUPS):
        hg = h1 - h0
        halves = [windows[part * N_GROUPS + g][:, :, :hg * HEAD_DIM].reshape(batch, -1, hg, HEAD_DIM) for part in range(2)]
        rows_p.append(jnp.stack(halves, axis=2))
    y_s = _mm_rows(x_s, w_perm, s_row0, t_sample)
    cos_s, sin_s = _rope_tables(PAST_LEN + jnp.arange(SUB, dtype=jnp.int32))
    o_s, new_caches_t = _sample_step(y_s, caches_t, new_caches_t, layer, cos_s, sin_s, dec_b, dec_s)
    x_all = _proj_ln_rows(o_s, wo_perm.reshape(N_GROUPS * SEC, D_MODEL), x_s, s_row0, g1, b1, x_next, t_prompt)
    return x_all, rows_p, new_caches_t


def _sgu_layer(x_all, w_in, b_in, ln_g, ln_b, ws, bs, w_out, g1, b1, dims):
    batch, seq, dec_b, dec_s = dims
    t_prompt = batch * seq
    t_sample = dec_b * dec_s
    common = (w_in.astype(BF16), _row2(b_in), _row2(ln_g), _row2(ln_b))
    tail = (w_out.astype(BF16), g1, b1)
    bs_p = jnp.broadcast_to(bs[:, :, None], (SGU_GROUPS, SGU_CHUNK, SGU_CHUNK))
    x_next = _sgu(x_all, None, common + (ws, bs_p) + tail, 0, t_prompt, 512, SGU_CHUNK, False)[0]
    c = min(SGU_CHUNK, dec_s)
    reps = SGU_CHUNK // c
    ws_s = jnp.tile(ws[:, :c, :c], (1, reps, reps))
    bs_s = jnp.broadcast_to(jnp.tile(bs[:, :c], (1, reps))[:, :, None], (SGU_GROUPS, SGU_CHUNK, SGU_CHUNK))
    x_all, v_new = _sgu(x_all, x_next, common + (ws_s, bs_s) + tail, t_prompt, t_sample, t_sample, c, True)
    return x_all, v_new.reshape(dec_b, dec_s, D_MODEL)


def _moe_layer(x_all, w_router, b_router, w_gu, b_gu, w_dn, b_dn, ln_g, ln_b, layer):
    w_r = jnp.pad(w_router, ((0, 0), (0, LANES - N_EXPERTS)))
    b_r = jnp.pad(b_router, (0, LANES - N_EXPERTS)).reshape(1, LANES)
    n_layers = w_gu.shape[0]
    return _moe(
        x_all, w_r, b_r,
        w_gu, b_gu.reshape(n_layers, N_EXPERTS, 1, -1),
        w_dn, b_dn.reshape(n_layers, N_EXPERTS, 1, -1),
        ln_g, ln_b, layer)


def kernel(x_prompt, x_sample, cache_kv_w128, cache_kv_w512, cache_kv_w2048, attn_w_qkv, attn_w_o, sgu_w_in, sgu_b_in, sgu_ln_g, sgu_ln_b, sgu_w_s, sgu_b_s, sgu_w_out, moe_w_router, moe_b_router, moe_w_gu, moe_b_gu, moe_w_down, moe_b_down, ln1_g, ln1_b, ln2_g, ln2_b):
    batch, seq, _ = x_prompt.shape
    dec_b, dec_s, _ = x_sample.shape
    t_prompt = batch * seq
    t_sample = dec_b * dec_s
    caches = (cache_kv_w128, cache_kv_w512, cache_kv_w2048)
    x_all = None
    cos_p, sin_p = _rope_tables(jnp.arange(seq, dtype=jnp.int32))
    row2 = _row2
    dims = (batch, seq, dec_b, dec_s)

    kv_prompt = [[] for _ in GROUPS]
    caches_t = [c.transpose(0, 1, 3, 4, 5, 2) for c in caches]
    new_caches_t = None
    v_rows = []
    for i in range(DEPTH):
        j = i // 2
        g1, b1 = row2(ln1_g[i]), row2(ln1_b[i])
        if i % 2 == 0:
            if x_all is None:
                sources = (x_prompt.reshape(t_prompt, D_MODEL), x_sample.reshape(t_sample, D_MODEL), 0)
            else:
                sources = (x_all, x_all, t_prompt)
            x_all, rows_p, new_caches_t = _attn_layer(
                *sources, caches_t, new_caches_t, j, attn_w_qkv[j], attn_w_o[j], g1, b1, cos_p, sin_p, dims)
            for g in range(N_GROUPS):
                kv_prompt[g].append(rows_p[g])
        else:
            x_all, v_new = _sgu_layer(
                x_all, sgu_w_in[j], sgu_b_in[j], sgu_ln_g[j], sgu_ln_b[j], sgu_w_s[j], sgu_b_s[j], sgu_w_out[j],
                g1, b1, dims)
            v_rows.append(v_new)
        x_all = _moe_layer(
            x_all, moe_w_router[i], moe_b_router[i], moe_w_gu, moe_b_gu, moe_w_down, moe_b_down,
            row2(ln2_g[i]), row2(ln2_b[i]), i)
    y_prompt = x_all[:t_prompt].reshape(batch, seq, D_MODEL)
    y_sample = x_all[t_prompt:].reshape(dec_b, dec_s, D_MODEL)
    kv_sample = [c.transpose(0, 1, 5, 2, 3, 4) for c in new_caches_t]
    return (
        y_prompt, y_sample,
        jnp.stack(kv_prompt[0]), jnp.stack(kv_prompt[1]), jnp.stack(kv_prompt[2]),
        kv_sample[0], kv_sample[1], kv_sample[2],
        jnp.stack(v_rows),
    )
```

```python
import functools
import math

import jax
import jax.numpy as jnp
from jax import lax
from jax.experimental import pallas as pl
from jax.experimental.pallas import tpu as pltpu

F32 = jnp.float32
BF16 = jnp.bfloat16

D_MODEL = 1024
HEAD_DIM = 64
N_HEADS = D_MODEL // HEAD_DIM
GROUPS = ((0, 6, 128, 1), (6, 11, 512, 4), (11, 16, 2048, 16))
N_GROUPS = len(GROUPS)
Q_BLOCK = 128
SEC = 384
ROPE_THETA = 10000.0
NEG_INF = -1e30
PAST_LEN = 8192
SGU_GROUPS = 8
SGU_CHUNK = 128
N_EXPERTS = 32
TOP_K = 4
SWIGLU_LIMIT = 7.0
SWIGLU_ALPHA = 1.702
MOE_BLOCK = 512
MOE_TILE = 256
BIG_CHUNK_BIT = 6
SUB = 8
DEPTH = 4
DEEPNORM_ALPHA = (2 * DEPTH) ** 0.25
LN_EPS = 1e-5
LANES = 128
MIB = 1024 * 1024


def _params(vmem_mib, *semantics):
    return pltpu.CompilerParams(dimension_semantics=semantics, vmem_limit_bytes=vmem_mib * MIB)


def _ln(x, g, b):
    mu = jnp.mean(x, axis=-1, keepdims=True)
    xc = x - mu
    var = jnp.mean(xc * xc, axis=-1, keepdims=True)
    return xc * lax.rsqrt(var + LN_EPS) * g + b


def _qkv_prompt_kernel(x_ref, w_ref, cos_ref, sin_ref, o1_ref, o2_ref, o3_ref, *rest, bm):
    win_refs = rest[:2 * N_GROUPS]
    y_s = rest[-1]
    y = jnp.dot(x_ref[...].astype(BF16), w_ref[...], preferred_element_type=F32)
    cos = cos_ref[...]
    sin = sin_ref[...]
    lane = lax.broadcasted_iota(jnp.int32, (bm, LANES), 1)
    first_half = (lane & (HEAD_DIM // 2)) == 0
    blocks_per_sec = SEC // LANES
    n_rot = 2 * N_GROUPS * blocks_per_sec
    for c in range(3 * N_GROUPS * blocks_per_sec):
        blk = y[:, c * LANES:(c + 1) * LANES]
        if c < n_rot:
            swapped = jnp.where(first_half, pltpu.roll(blk, LANES - HEAD_DIM // 2, 1), pltpu.roll(blk, HEAD_DIM // 2, 1))
            blk = blk * cos + swapped * sin
        if c < n_rot // 2:
            blk = blk * (HEAD_DIM ** -0.5)
        else:
            sec, p = divmod(c - n_rot // 2, blocks_per_sec)
            win_ref = win_refs[sec]
            keep = win_ref.shape[1]
            win_ref[0, :, p * LANES:(p + 1) * LANES] = blk[bm - keep:, :]
        y_s[c] = blk
    for g, o_ref in enumerate((o1_ref, o2_ref, o3_ref)):
        d = GROUPS[g][3]
        n = bm // d
        for r in range(d):
            for part in range(3):
                for p in range(blocks_per_sec):
                    c = (part * N_GROUPS + g) * blocks_per_sec + p
                    rows = y_s[c] if d == 1 else y_s[c, pl.ds(r, n, stride=d), :]
                    o_ref[0, r, :, part * SEC + p * LANES:part * SEC + (p + 1) * LANES] = rows.astype(BF16)


def _qkv_prompt(x_all, w_perm, cos, sin, batch, seq):
    bm = 512
    tiles = seq // bm
    out_shape = [jax.ShapeDtypeStruct((batch, d, seq // d, 3 * SEC), BF16) for (_, _, _, d) in GROUPS]
    out_specs = [pl.BlockSpec((1, d, bm // d, 3 * SEC), lambda b, i: (b, 0, i, 0)) for (_, _, _, d) in GROUPS]
    for _ in range(2):
        for (_, _, win, _) in GROUPS:
            keep = min(win, seq)
            rows = min(keep, bm)
            first = tiles - keep // rows
            out_shape.append(jax.ShapeDtypeStruct((batch, keep, SEC), F32))
            out_specs.append(pl.BlockSpec((1, rows, SEC), lambda b, i, first=first: (b, jnp.maximum(i - first, 0), 0)))
    n_cols = 3 * N_GROUPS * SEC
    return pl.pallas_call(
        functools.partial(_qkv_prompt_kernel, bm=bm),
        grid=(batch, tiles),
        in_specs=[
            pl.BlockSpec((bm, D_MODEL), lambda b, i: (b * tiles + i, 0)),
            pl.BlockSpec((D_MODEL, n_cols), lambda b, i: (0, 0)),
            pl.BlockSpec((bm, LANES), lambda b, i: (i, 0)),
            pl.BlockSpec((bm, LANES), lambda b, i: (i, 0)),
        ],
        out_specs=out_specs,
        out_shape=out_shape,
        scratch_shapes=[pltpu.VMEM((n_cols // LANES, bm, LANES), F32)],
        compiler_params=_params(48, "arbitrary", "arbitrary"),
        name="qkv_prompt",
    )(x_all, w_perm, cos, sin)


def _attn_group_kernel(qkv_ref, o_ref, lse_ref, *, d, m_len, n_heads):
    nb = m_len // Q_BLOCK
    kw = min(2 * Q_BLOCK, m_len)
    lane = lax.broadcasted_iota(jnp.int32, (1, LANES), 1)
    lo = lane < HEAD_DIM
    qi = lax.broadcasted_iota(jnp.int32, (Q_BLOCK, kw), 0)
    kj = lax.broadcasted_iota(jnp.int32, (Q_BLOCK, kw), 1)

    def block(i, carry):
        r = i // nb
        n = i % nb
        ks = pl.multiple_of(jnp.maximum(n - 1, 0) * Q_BLOCK, Q_BLOCK)
        qs = pl.multiple_of(n * Q_BLOCK, Q_BLOCK)
        delta = qs - ks + qi - kj
        valid = (delta >= 0) & (delta <= Q_BLOCK)
        if d == 1:
            dst = pl.ds(qs, Q_BLOCK)
        else:
            dst = pl.ds(r + qs * d, Q_BLOCK, stride=d)
        lses = []
        for p in range(SEC // LANES):
            cols = slice(p * LANES, (p + 1) * LANES)
            q2 = qkv_ref[0, r, pl.ds(qs, Q_BLOCK), cols]
            k2 = qkv_ref[0, r, pl.ds(ks, kw), SEC + p * LANES:SEC + (p + 1) * LANES]
            v2 = qkv_ref[0, r, pl.ds(ks, kw), 2 * SEC + p * LANES:2 * SEC + (p + 1) * LANES]
            n_half = min(2, n_heads - 2 * p)
            if n_half <= 0:
                o_ref[0, p, dst, :] = jnp.zeros((Q_BLOCK, LANES), F32)
                continue
            zero = jnp.zeros_like(q2)
            q_both = jnp.concatenate([jnp.where(lo, q2, zero), jnp.where(lo, zero, q2)][:n_half], axis=0)
            s_both = lax.dot_general(q_both, k2, (((1,), (1,)), ((), ())), preferred_element_type=F32)
            probs, inv_l = [], []
            for half in range(n_half):
                s = jnp.where(valid, s_both[half * Q_BLOCK:(half + 1) * Q_BLOCK], NEG_INF)
                mx = jnp.max(s, axis=1, keepdims=True)
                pe = jnp.exp(s - mx)
                l = jnp.sum(pe, axis=1, keepdims=True)
                probs.append(pe.astype(BF16))
                inv_l.append(1.0 / l)
                lses.append(mx + jnp.log(l))
            pv = jnp.dot(jnp.concatenate(probs, axis=0), v2, preferred_element_type=F32)
            o_pair = pv[:Q_BLOCK] * inv_l[0]
            if n_half == 2:
                o_pair = jnp.where(lo, o_pair, pv[Q_BLOCK:] * inv_l[1])
            else:
                o_pair = jnp.where(lo, o_pair, 0.0)
            o_ref[0, p, dst, :] = o_pair
        lmax = functools.reduce(jnp.maximum, lses)
        lsum = functools.reduce(lambda a, b: a + b, [jnp.exp(l - lmax) for l in lses])
        glse = lmax + jnp.log(lsum) - math.log(n_heads)
        lse_ref[0, dst, :] = jnp.broadcast_to(glse, (Q_BLOCK, LANES))
        return carry

    lax.fori_loop(0, d * nb, block, 0)


def _attn_group(qkv_g, g, batch, seq):
    h0, h1, _, d = GROUPS[g]
    m_len = seq // d
    return pl.pallas_call(
        functools.partial(_attn_group_kernel, d=d, m_len=m_len, n_heads=h1 - h0),
        grid=(batch,),
        in_specs=[pl.BlockSpec((1, d, m_len, 3 * SEC), lambda b: (b, 0, 0, 0))],
        out_specs=[
            pl.BlockSpec((1, SEC // LANES, seq, LANES), lambda b: (b, 0, 0, 0)),
            pl.BlockSpec((1, seq, LANES), lambda b: (b, 0, 0)),
        ],
        out_shape=[
            jax.ShapeDtypeStruct((batch, SEC // LANES, seq, LANES), F32),
            jax.ShapeDtypeStruct((batch, seq, LANES), F32),
        ],
        compiler_params=_params(40, "arbitrary"),
        name=f"attn_group{g}",
    )(qkv_g)


def _attn_out_kernel(o1_ref, o2_ref, o3_ref, l1_ref, l2_ref, l3_ref, wo_ref, x_ref, g_ref, b_ref, out_ref):
    ls = [l1_ref[0], l2_ref[0], l3_ref[0]]
    mx = jnp.maximum(jnp.maximum(ls[0], ls[1]), ls[2])
    es = [jnp.exp(l - mx) for l in ls]
    inv = float(N_GROUPS) / (es[0] + es[1] + es[2])
    acc = None
    for g, o_ref in enumerate((o1_ref, o2_ref, o3_ref)):
        w = es[g] * inv
        o_g = jnp.concatenate([o_ref[0, p] * w for p in range(SEC // LANES)], axis=1)
        part = jnp.dot(o_g.astype(BF16), wo_ref[g], preferred_element_type=F32)
        acc = part if acc is None else acc + part
    out_ref[...] = _ln(DEEPNORM_ALPHA * x_ref[...] + acc, g_ref[...], b_ref[...])


def _attn_out_prompt(os_, ls_, wo_perm, x_src, ln_g, ln_b, batch, seq, t_all):
    bm = 1024
    tiles = seq // bm
    row = lambda b, i: (b * tiles + i, 0)
    const2 = lambda b, i: (0, 0)
    return pl.pallas_call(
        _attn_out_kernel,
        grid=(batch, tiles),
        in_specs=[pl.BlockSpec((1, SEC // LANES, bm, LANES), lambda b, i: (b, 0, i, 0))] * 3
        + [pl.BlockSpec((1, bm, LANES), lambda b, i: (b, i, 0))] * 3 + [
            pl.BlockSpec((N_GROUPS, SEC, D_MODEL), lambda b, i: (0, 0, 0)),
            pl.BlockSpec((bm, D_MODEL), row),
            pl.BlockSpec((1, D_MODEL), const2),
            pl.BlockSpec((1, D_MODEL), const2),
        ],
        out_specs=pl.BlockSpec((bm, D_MODEL), row),
        out_shape=jax.ShapeDtypeStruct((t_all, D_MODEL), F32),
        compiler_params=_params(40, "arbitrary", "arbitrary"),
        name="attn_out_prompt",
    )(*os_, *ls_, wo_perm, x_src, ln_g, ln_b)


def _mm_kernel(x_ref, w_ref, o_ref):
    o_ref[...] = jnp.dot(x_ref[...].astype(BF16), w_ref[...], preferred_element_type=F32)


def _mm_rows(x_all, w, row0, rows):
    n = w.shape[1]
    blk0 = row0 // rows
    return pl.pallas_call(
        _mm_kernel,
        grid=(1,),
        in_specs=[
            pl.BlockSpec((rows, D_MODEL), lambda i: (blk0, 0)),
            pl.BlockSpec((D_MODEL, n), lambda i: (0, 0)),
        ],
        out_specs=pl.BlockSpec((rows, n), lambda i: (0, 0)),
        out_shape=jax.ShapeDtypeStruct((rows, n), F32),
        compiler_params=_params(40, "arbitrary"),
        name="mm_rows",
    )(x_all, w)


def _proj_ln_kernel(a_ref, w_ref, x_ref, g_ref, b_ref, prev_ref, out_ref):
    del prev_ref
    acc = jnp.dot(a_ref[...].astype(BF16), w_ref[...], preferred_element_type=F32)
    out_ref[...] = _ln(DEEPNORM_ALPHA * x_ref[...] + acc, g_ref[...], b_ref[...])


def _proj_ln_rows(a, w, x_src, src_row0, ln_g, ln_b, x_next, row0):
    rows = a.shape[0]
    blk0 = row0 // rows
    src_blk0 = src_row0 // rows
    const2 = lambda i: (0, 0)
    return pl.pallas_call(
        _proj_ln_kernel,
        grid=(1,),
        in_specs=[
            pl.BlockSpec((rows, a.shape[1]), const2),
            pl.BlockSpec(w.shape, const2),
            pl.BlockSpec((rows, D_MODEL), lambda i: (src_blk0, 0)),
            pl.BlockSpec((1, D_MODEL), const2),
            pl.BlockSpec((1, D_MODEL), const2),
            pl.BlockSpec(memory_space=pl.ANY),
        ],
        out_specs=pl.BlockSpec((rows, D_MODEL), lambda i: (blk0, 0)),
        out_shape=jax.ShapeDtypeStruct(x_next.shape, F32),
        input_output_aliases={5: 0},
        compiler_params=_params(40, "arbitrary"),
        name="proj_ln_rows",
    )(a, w, x_src, ln_g, ln_b, x_next)


def _sgu_kernel(*refs, bm, chunk, emit_v, aliased):
    (x_ref, win_ref, bin_ref, lng_ref, lnb_ref, ws_ref, bs_ref, wout_ref, g1_ref, b1_ref) = refs[:10]
    rest = refs[10 + (1 if aliased else 0):]
    out_ref = rest[0]
    v_ref = rest[1] if emit_v else None
    ug_s = rest[-1]
    width = D_MODEL
    x = x_ref[...]
    z = jnp.dot(x.astype(BF16), win_ref[...], preferred_element_type=F32) + bin_ref[...]
    z = 0.5 * z * (1.0 + lax.erf(z * (2.0 ** -0.5)))
    v = _ln(z[:, width:], lng_ref[...], lnb_ref[...])
    if emit_v:
        v_ref[...] = v
    ii = lax.broadcasted_iota(jnp.int32, (SGU_CHUNK, SGU_CHUNK), 0)
    jj = lax.broadcasted_iota(jnp.int32, (SGU_CHUNK, SGU_CHUNK), 1)
    causal = (jj <= ii) & ((ii // chunk) == (jj // chunk))
    gw = width // SGU_GROUPS
    for gi in range(SGU_GROUPS):
        wsm = jnp.where(causal, ws_ref[gi], 0.0).astype(BF16)
        for c in range(bm // SGU_CHUNK):
            rows = slice(c * SGU_CHUNK, (c + 1) * SGU_CHUNK)
            cols = slice(gi * gw, (gi + 1) * gw)
            gate = jnp.dot(wsm, v[rows, cols].astype(BF16), preferred_element_type=F32) + bs_ref[gi]
            ug_s[rows, cols] = (z[rows, cols] * gate).astype(BF16)
    y = jnp.dot(ug_s[...], wout_ref[...], preferred_element_type=F32)
    out_ref[...] = _ln(DEEPNORM_ALPHA * x + y, g1_ref[...], b1_ref[...])


def _sgu(x_all, x_next, weights, row0, rows, bm, chunk, emit_v):
    w_in, b_in, ln_g, ln_b, ws, bs, w_out, g1, b1 = weights
    t_all = x_all.shape[0]
    blk0 = row0 // bm
    row = lambda i: (blk0 + i, 0)
    const2 = lambda i: (0, 0)
    const3 = lambda i: (0, 0, 0)
    aliased = x_next is not None
    in_specs = [
        pl.BlockSpec((bm, D_MODEL), row),
        pl.BlockSpec(w_in.shape, const2),
        pl.BlockSpec((1, 2 * D_MODEL), const2),
        pl.BlockSpec((1, D_MODEL), const2),
        pl.BlockSpec((1, D_MODEL), const2),
        pl.BlockSpec(ws.shape, const3),
        pl.BlockSpec(bs.shape, const3),
        pl.BlockSpec(w_out.shape, const2),
        pl.BlockSpec((1, D_MODEL), const2),
        pl.BlockSpec((1, D_MODEL), const2),
    ]
    args = [x_all, w_in, b_in, ln_g, ln_b, ws, bs, w_out, g1, b1]
    aliases = {}
    if aliased:
        in_specs.append(pl.BlockSpec(memory_space=pl.ANY))
        args.append(x_next)
        aliases = {10: 0}
    out_specs = [pl.BlockSpec((bm, D_MODEL), row)]
    out_shape = [jax.ShapeDtypeStruct((t_all, D_MODEL), F32)]
    if emit_v:
        out_specs.append(pl.BlockSpec((bm, D_MODEL), lambda i: (i, 0)))
        out_shape.append(jax.ShapeDtypeStruct((rows, D_MODEL), F32))
    return pl.pallas_call(
        functools.partial(_sgu_kernel, bm=bm, chunk=chunk, emit_v=emit_v, aliased=aliased),
        grid=(rows // bm,),
        in_specs=in_specs,
        out_specs=out_specs,
        out_shape=out_shape,
        scratch_shapes=[pltpu.VMEM((bm, D_MODEL), BF16)],
        input_output_aliases=aliases,
        compiler_params=_params(48, "arbitrary"),
        name="sgu",
    )(*args)


def _router_kernel(x_ref, wr_ref, br_ref, idx_ref, gate_ref, cnt_ref, *, bm, t_all):
    x = x_ref[...]
    w = wr_ref[...]
    x_hi = x.astype(BF16)
    w_hi = w.astype(BF16)
    x_lo = (x - x_hi.astype(F32)).astype(BF16)
    w_lo = (w - w_hi.astype(F32)).astype(BF16)
    contract = (((1,), (1,)), ((), ()))
    both = lax.dot_general(jnp.concatenate([w_hi, w_lo], axis=0), x_hi, contract, preferred_element_type=F32)
    logits = (both[:N_EXPERTS] + lax.dot_general(w_hi, x_lo, contract, preferred_element_type=F32)
              + both[N_EXPERTS:])
    logits = logits + jnp.concatenate([br_ref[...]] * (bm // LANES), axis=1)
    expert = lax.broadcasted_iota(jnp.int32, (N_EXPERTS, bm), 0).astype(F32)
    token = pl.program_id(0) * bm + lax.broadcasted_iota(jnp.int32, (N_EXPERTS, bm), 1)
    logits = jnp.where(token < t_all, logits, 0.0)
    vals, idxs = [], []
    chosen = jnp.zeros((N_EXPERTS, bm), F32)
    for _ in range(TOP_K):
        m = jnp.max(logits, axis=0, keepdims=True)
        i = jnp.min(jnp.where(logits == m, expert, float(N_EXPERTS)), axis=0, keepdims=True)
        vals.append(m)
        idxs.append(i)
        hit = expert == i
        chosen = chosen + jnp.where(hit & (token < t_all), 1.0, 0.0)
        logits = jnp.where(hit, -jnp.inf, logits)
    es = [jnp.exp(v - vals[0]) for v in vals]
    inv = 1.0 / functools.reduce(lambda a, b: a + b, es)
    sub = lax.broadcasted_iota(jnp.int32, (2 * TOP_K, bm), 0)
    head = jnp.zeros((2 * TOP_K, bm), F32)
    for k in range(TOP_K):
        head = jnp.where(sub == k, idxs[k], head)
        head = jnp.where(sub == TOP_K + k, es[k] * inv, head)
    pad = jnp.zeros((LANES - 2 * TOP_K - N_EXPERTS, bm), F32)
    packed = jnp.concatenate([head, chosen, pad], axis=0).T
    lane = lax.broadcasted_iota(jnp.int32, (bm, LANES), 1)
    idx_ref[...] = jnp.where(lane < TOP_K, packed, 0.0).astype(jnp.int32)
    gate_ref[...] = jnp.where(lane < TOP_K, pltpu.roll(packed, LANES - TOP_K, 1), 0.0)
    marks = jnp.where(lane < N_EXPERTS, pltpu.roll(packed, LANES - 2 * TOP_K, 1), 0.0)
    counts = jnp.sum(marks, axis=0, keepdims=True)
    cnt_ref[0] = jnp.broadcast_to(counts, (SUB, LANES))


def _router(x_all, w_r, b_r):
    bm = MOE_TILE
    t_all = x_all.shape[0]
    n_tiles = pl.cdiv(t_all, bm)
    row = lambda i: (i, 0)
    const2 = lambda i: (0, 0)
    return pl.pallas_call(
        functools.partial(_router_kernel, bm=bm, t_all=t_all),
        grid=(n_tiles,),
        in_specs=[
            pl.BlockSpec((bm, D_MODEL), row),
            pl.BlockSpec((N_EXPERTS, D_MODEL), const2),
            pl.BlockSpec((N_EXPERTS, LANES), const2),
        ],
        out_specs=[pl.BlockSpec((bm, LANES), row), pl.BlockSpec((bm, LANES), row),
                   pl.BlockSpec((1, SUB, LANES), lambda i: (i, 0, 0))],
        out_shape=[
            jax.ShapeDtypeStruct((t_all, LANES), jnp.int32),
            jax.ShapeDtypeStruct((t_all, LANES), F32),
            jax.ShapeDtypeStruct((n_tiles, SUB, LANES), F32),
        ],
        compiler_params=_params(32, "arbitrary"),
        name="router",
    )(x_all, w_r, b_r)


def _local_slots(idx, row_valid):
    tile = idx.shape[0]
    lane = lax.broadcasted_iota(jnp.int32, (tile, LANES), 1)
    idx = jnp.where(row_valid, idx, -1)
    picks = [lane == idx[:, k:k + 1] for k in range(TOP_K)]
    chosen = functools.reduce(lambda a, b: a + b, [p.astype(F32) for p in picks])
    earlier = (lax.broadcasted_iota(jnp.int32, (tile, tile), 1)
               < lax.broadcasted_iota(jnp.int32, (tile, tile), 0)).astype(BF16)
    rank = jnp.dot(earlier, chosen.astype(BF16), preferred_element_type=F32)
    per_expert = jnp.broadcast_to(jnp.sum(chosen, axis=0, keepdims=True), (SUB, LANES))
    lower = (lax.broadcasted_iota(jnp.int32, (LANES, LANES), 0)
             < lax.broadcasted_iota(jnp.int32, (LANES, LANES), 1)).astype(BF16)
    base = jnp.dot(per_expert.astype(BF16), lower, preferred_element_type=F32)[0:1]
    place = base + rank + 1.0
    return [jnp.sum(jnp.where(p, place, 0.0), axis=1, keepdims=True) - 1.0 for p in picks]


def _segment_copies(cnt_ref, off_ref, j, hbm, buf, sem, to_hbm):
    local = 0
    for e in range(N_EXPERTS):
        cnt = cnt_ref[j * N_EXPERTS + e]
        off = off_ref[j * N_EXPERTS + e]
        def chunks(bits, cnt=cnt, off=off, local=local):
            for bit in bits:
                size = 1 << bit

                @pl.when((cnt & size) != 0)
                def _(size=size):
                    done = cnt & (size - 1)
                    a = buf.at[pl.ds(pl.multiple_of((local + done) * SUB, SUB), size * SUB), :]
                    b = hbm.at[pl.ds(pl.multiple_of((off + done) * SUB, SUB), size * SUB), :]
                    (pltpu.make_async_copy(a, b, sem) if to_hbm else pltpu.make_async_copy(b, a, sem)).start()

        n_bits = MOE_TILE.bit_length()
        chunks(range(BIG_CHUNK_BIT))
        pl.when(cnt >= (1 << BIG_CHUNK_BIT))(functools.partial(chunks, range(BIG_CHUNK_BIT, n_bits)))
        local = local + cnt


def _rows_to_tiles(ref, value):
    rows = value.shape[0]
    for s in range(D_MODEL // LANES):
        ref[pl.ds(s, rows, stride=SUB), :] = value[:, s * LANES:(s + 1) * LANES]


def _tiles_to_rows(ref, rows):
    return jnp.concatenate([ref[pl.ds(s, rows, stride=SUB), :] for s in range(D_MODEL // LANES)], axis=1)


def _dispatch_kernel(cnt_ref, off_ref, x_ref, idx_ref, xs_hbm, slot_ref, buf, sem, *, t_all, n_tiles):
    j = pl.program_id(0)
    par = j % 2
    tile = MOE_TILE
    row = j * tile + lax.broadcasted_iota(jnp.int32, (tile, LANES), 0)
    row_valid = row < t_all
    slots = _local_slots(idx_ref[...], row_valid)
    lane = lax.broadcasted_iota(jnp.int32, (tile, LANES), 1)
    slot_out = jnp.zeros((tile, LANES), F32)
    for k in range(TOP_K):
        slot_out = jnp.where(lane == k, slots[k], slot_out)
    slot_ref[...] = slot_out
    slots_t = slot_out.T
    pos = lax.broadcasted_iota(jnp.int32, (TOP_K * tile, tile), 0).astype(F32)
    hit = functools.reduce(jnp.logical_or, [pos == slots_t[k:k + 1, :] for k in range(TOP_K)])
    place = jnp.where(hit, 1.0, 0.0).astype(BF16)
    in_range = (j * tile + lax.broadcasted_iota(jnp.int32, (tile, D_MODEL), 0)) < t_all
    x = jnp.where(in_range, x_ref[...], 0.0).astype(BF16)
    ordered = jnp.dot(place, x, preferred_element_type=F32)
    _rows_to_tiles(buf.at[par], ordered)

    full = TOP_K * tile * SUB
    last = TOP_K * (t_all - (n_tiles - 1) * tile) * SUB

    @pl.when(j > 0)
    def _():
        pltpu.make_async_copy(buf.at[1 - par], xs_hbm.at[pl.ds(0, full), :], sem.at[0]).wait()

    _segment_copies(cnt_ref, off_ref, j, xs_hbm, buf.at[par], sem.at[0], True)

    @pl.when(j == n_tiles - 1)
    def _():
        pltpu.make_async_copy(buf.at[par, pl.ds(0, last), :], xs_hbm.at[pl.ds(0, last), :], sem.at[0]).wait()


def _dispatch(cnt, off, x_all, idx, n_slots):
    t_all = x_all.shape[0]
    n_tiles = pl.cdiv(t_all, MOE_TILE)
    row = lambda j, c, o: (j, 0)
    grid_spec = pltpu.PrefetchScalarGridSpec(
        num_scalar_prefetch=2,
        grid=(n_tiles,),
        in_specs=[pl.BlockSpec((MOE_TILE, D_MODEL), row), pl.BlockSpec((MOE_TILE, LANES), row)],
        out_specs=[pl.BlockSpec(memory_space=pl.ANY), pl.BlockSpec((MOE_TILE, LANES), row)],
        scratch_shapes=[pltpu.VMEM((2, TOP_K * MOE_TILE * SUB, LANES), F32), pltpu.SemaphoreType.DMA((1,))],
    )
    return pl.pallas_call(
        functools.partial(_dispatch_kernel, t_all=t_all, n_tiles=n_tiles),
        grid_spec=grid_spec,
        out_shape=[jax.ShapeDtypeStruct((n_slots * SUB, LANES), F32), jax.ShapeDtypeStruct((t_all, LANES), F32)],
        compiler_params=_params(40, "arbitrary"),
        name="moe_dispatch",
    )(cnt, off, x_all, idx)


def _expert_kernel(be_ref, nreal_ref, first_ref, par_ref, next_ref, xs_ref, wgu_hbm, bgu_ref, wdn_hbm, bdn_ref, ys_ref,
                   wgu_f32, wdn_f32, wgu_bf, wdn_bf, sem, *, layer):
    i = pl.program_id(0)
    ff = wdn_bf.shape[0]

    def copies(e, slot):
        return (pltpu.make_async_copy(wgu_hbm.at[layer, e], wgu_f32.at[slot], sem.at[0, slot]),
                pltpu.make_async_copy(wdn_hbm.at[layer, e], wdn_f32.at[slot], sem.at[1, slot]))

    @pl.when(i == 0)
    def _():
        for c in copies(be_ref[0], 0):
            c.start()

    @pl.when(i < nreal_ref[0])
    def _():
        @pl.when(first_ref[i] == 1)
        def _():
            slot = par_ref[i]
            for c in copies(be_ref[i], slot):
                c.wait()
            wgu_bf[...] = wgu_f32[slot].astype(BF16)
            wdn_bf[...] = wdn_f32[slot].astype(BF16)

            @pl.when(next_ref[i] >= 0)
            def _():
                for c in copies(next_ref[i], 1 - slot):
                    c.start()

        x = _tiles_to_rows(xs_ref, MOE_BLOCK).astype(BF16)
        h = jnp.dot(x, wgu_bf[...], preferred_element_type=F32) + bgu_ref[0]
        gate = jnp.minimum(h[:, :ff], SWIGLU_LIMIT)
        up = jnp.clip(h[:, ff:], -SWIGLU_LIMIT, SWIGLU_LIMIT)
        act = (up + 1.0) * gate * jax.nn.sigmoid(SWIGLU_ALPHA * gate)
        y = jnp.dot(act.astype(BF16), wdn_bf[...], preferred_element_type=F32) + bdn_ref[0]
        _rows_to_tiles(ys_ref, y)


def _experts(block_expert, n_real, run_first, run_parity, next_expert, xs, w_gu, b_gu, w_dn, b_dn, layer):
    n_blocks = xs.shape[0] // (MOE_BLOCK * SUB)
    ff = w_dn.shape[2]
    rows = lambda i, be, nr, *_: (jnp.minimum(i, nr[0] - 1), 0)
    by_expert = lambda i, be, *_: (layer, be[i], 0, 0)
    grid_spec = pltpu.PrefetchScalarGridSpec(
        num_scalar_prefetch=5,
        grid=(n_blocks,),
        in_specs=[
            pl.BlockSpec((MOE_BLOCK * SUB, LANES), rows),
            pl.BlockSpec(memory_space=pl.ANY),
            pl.BlockSpec((None, 1, 1, 2 * ff), by_expert),
            pl.BlockSpec(memory_space=pl.ANY),
            pl.BlockSpec((None, 1, 1, D_MODEL), by_expert),
        ],
        out_specs=pl.BlockSpec((MOE_BLOCK * SUB, LANES), rows),
        scratch_shapes=[
            pltpu.VMEM((2, D_MODEL, 2 * ff), F32),
            pltpu.VMEM((2, ff, D_MODEL), F32),
            pltpu.VMEM((D_MODEL, 2 * ff), BF16),
            pltpu.VMEM((ff, D_MODEL), BF16),
            pltpu.SemaphoreType.DMA((2, 2)),
        ],
    )
    return pl.pallas_call(
        functools.partial(_expert_kernel, layer=layer),
        grid_spec=grid_spec,
        out_shape=jax.ShapeDtypeStruct(xs.shape, F32),
        compiler_params=_params(56, "arbitrary"),
        name="experts",
    )(block_expert, n_real, run_first, run_parity, next_expert, xs, w_gu, b_gu, w_dn, b_dn)


def _combine_kernel(cnt_ref, off_ref, x_ref, slot_ref, gate_ref, g_ref, b_ref, ys_hbm, out_ref, buf, sem, *, t_all, n_tiles):
    j = pl.program_id(0)
    par = j % 2
    tile = MOE_TILE
    full = TOP_K * tile * SUB
    last = TOP_K * (t_all - (n_tiles - 1) * tile) * SUB

    @pl.when(j == 0)
    def _():
        _segment_copies(cnt_ref, off_ref, 0, ys_hbm, buf.at[0], sem.at[0], False)

    @pl.when(j + 1 < n_tiles)
    def _():
        _segment_copies(cnt_ref, off_ref, j + 1, ys_hbm, buf.at[1 - par], sem.at[1 - par], False)

    @pl.when(j < n_tiles - 1)
    def _():
        pltpu.make_async_copy(ys_hbm.at[pl.ds(0, full), :], buf.at[par], sem.at[par]).wait()

    @pl.when(j == n_tiles - 1)
    def _():
        pltpu.make_async_copy(ys_hbm.at[pl.ds(0, last), :], buf.at[par, pl.ds(0, last), :], sem.at[par]).wait()

    n_valid = TOP_K * jnp.minimum(tile, t_all - j * tile)
    ys = _tiles_to_rows(buf.at[par], TOP_K * tile)
    ys = jnp.where(lax.broadcasted_iota(jnp.int32, ys.shape, 0) < n_valid, ys, 0.0).astype(BF16)
    slots = slot_ref[...]
    gates = gate_ref[...]
    pos = lax.broadcasted_iota(jnp.int32, (tile, TOP_K * tile), 1).astype(F32)
    weights = jnp.zeros((tile, TOP_K * tile), F32)
    for k in range(TOP_K):
        weights = jnp.where(pos == slots[:, k:k + 1], gates[:, k:k + 1], weights)
    y = jnp.dot(weights.astype(BF16), ys, preferred_element_type=F32)
    out_ref[...] = _ln(DEEPNORM_ALPHA * x_ref[...] + y, g_ref[...], b_ref[...])


def _combine(cnt, off, x_all, slots, gates, ln_g, ln_b, ys):
    t_all = x_all.shape[0]
    n_tiles = pl.cdiv(t_all, MOE_TILE)
    row = lambda j, c, o: (j, 0)
    const2 = lambda j, c, o: (0, 0)
    grid_spec = pltpu.PrefetchScalarGridSpec(
        num_scalar_prefetch=2,
        grid=(n_tiles,),
        in_specs=[
            pl.BlockSpec((MOE_TILE, D_MODEL), row),
            pl.BlockSpec((MOE_TILE, LANES), row),
            pl.BlockSpec((MOE_TILE, LANES), row),
            pl.BlockSpec((1, D_MODEL), const2),
            pl.BlockSpec((1, D_MODEL), const2),
            pl.BlockSpec(memory_space=pl.ANY),
        ],
        out_specs=pl.BlockSpec((MOE_TILE, D_MODEL), row),
        scratch_shapes=[pltpu.VMEM((2, TOP_K * MOE_TILE * SUB, LANES), F32), pltpu.SemaphoreType.DMA((2,))],
    )
    return pl.pallas_call(
        functools.partial(_combine_kernel, t_all=t_all, n_tiles=n_tiles),
        grid_spec=grid_spec,
        out_shape=jax.ShapeDtypeStruct((t_all, D_MODEL), F32),
        compiler_params=_params(40, "arbitrary"),
        name="moe_combine",
    )(cnt, off, x_all, slots, gates, ln_g, ln_b, ys)


def _moe(x_all, w_r, b_r, w_gu, b_gu, w_dn, b_dn, ln_g, ln_b, layer):
    t_all = x_all.shape[0]
    idx, gates, tile_counts = _router(x_all, w_r, b_r)
    cnt = tile_counts[:, 0, :N_EXPERTS].astype(jnp.int32)
    counts = jnp.sum(cnt, axis=0)
    padded = (counts + MOE_BLOCK - 1) // MOE_BLOCK * MOE_BLOCK
    pad_end = jnp.cumsum(padded)
    pad_start = pad_end - padded
    off = pad_start[None, :] + jnp.cumsum(cnt, axis=0) - cnt
    n_blocks = -(-t_all * TOP_K // MOE_BLOCK) + N_EXPERTS
    block_start = jnp.arange(n_blocks, dtype=jnp.int32) * MOE_BLOCK
    block_expert = jnp.minimum(
        jnp.sum((pad_end[None, :] <= block_start[:, None]).astype(jnp.int32), axis=1), N_EXPERTS - 1)
    n_real = (pad_end[-1:] // MOE_BLOCK).astype(jnp.int32)
    block = jnp.arange(n_blocks, dtype=jnp.int32)
    prev_expert = jnp.concatenate([jnp.full((1,), -1, jnp.int32), block_expert[:-1]])
    run_first = ((block < n_real[0]) & (block_expert != prev_expert)).astype(jnp.int32)
    run_parity = (jnp.cumsum(run_first) - 1) % 2
    after_run = pad_end[block_expert] // MOE_BLOCK
    next_expert = jnp.where(after_run < n_real[0], block_expert[jnp.minimum(after_run, n_blocks - 1)], -1)
    cnt_flat, off_flat = cnt.reshape(-1), off.reshape(-1).astype(jnp.int32)
    xs, slots = _dispatch(cnt_flat, off_flat, x_all, idx, n_blocks * MOE_BLOCK)
    ys = _experts(block_expert, n_real, run_first, run_parity.astype(jnp.int32), next_expert.astype(jnp.int32),
                  xs, w_gu, b_gu, w_dn, b_dn, layer)
    return _combine(cnt_flat, off_flat, x_all, slots, gates, ln_g, ln_b, ys)


def _rope_tables(pos):
    half = HEAD_DIM // 2
    inv_freq = ROPE_THETA ** (-jnp.arange(half, dtype=F32) / half)
    ang = pos.astype(F32)[:, None] * inv_freq[None, :]
    cos = jnp.cos(ang)
    sin = jnp.sin(ang)
    cos = jnp.concatenate([cos, cos], axis=1)
    sin = jnp.concatenate([-sin, sin], axis=1)
    reps = LANES // HEAD_DIM
    return jnp.tile(cos, (1, reps)), jnp.tile(sin, (1, reps))


def _permute_qkv_weight(w_qkv):
    w3 = w_qkv.reshape(D_MODEL, 3, N_HEADS * HEAD_DIM)
    secs = []
    for part in range(3):
        for (h0, h1, _, _) in GROUPS:
            sec = w3[:, part, h0 * HEAD_DIM:h1 * HEAD_DIM]
            secs.append(jnp.pad(sec, ((0, 0), (0, SEC - sec.shape[1]))))
    return jnp.concatenate(secs, axis=1).astype(BF16)


def _permute_out_weight(w_o):
    secs = []
    for (h0, h1, _, _) in GROUPS:
        sec = w_o[h0 * HEAD_DIM:h1 * HEAD_DIM]
        secs.append(jnp.pad(sec, ((0, SEC - sec.shape[0]), (0, 0))))
    return jnp.stack(secs).astype(BF16)


def _sample_step_kernel(*refs, dec_s, aliased):
    y_ref, cos_ref, sin_ref = refs[:3]
    cache_refs = refs[3:3 + N_GROUPS]
    outs = refs[3 + N_GROUPS * (2 if aliased else 1):]
    o_ref, new_refs = outs[0], outs[1:]
    y = y_ref[0]
    cos = cos_ref[...]
    sin = sin_ref[...]
    lane = lax.broadcasted_iota(jnp.int32, (SUB, LANES), 1)
    first_half = (lane & (HEAD_DIM // 2)) == 0
    blocks_per_sec = SEC // LANES
    n_rot = 2 * N_GROUPS * blocks_per_sec
    blocks = []
    for c in range(3 * N_GROUPS * blocks_per_sec):
        blk = y[:, c * LANES:(c + 1) * LANES]
        if c < n_rot:
            swapped = jnp.where(first_half, pltpu.roll(blk, LANES - HEAD_DIM // 2, 1), pltpu.roll(blk, HEAD_DIM // 2, 1))
            blk = blk * cos + swapped * sin
        if c < n_rot // 2:
            blk = blk * (HEAD_DIM ** -0.5)
        blocks.append(blk)

    lo = lane < HEAD_DIM
    new_row = lax.broadcasted_iota(jnp.int32, (SUB, SUB), 1)
    qry_row = lax.broadcasted_iota(jnp.int32, (SUB, SUB), 0)
    pad_rows = jnp.zeros((LANES - SUB, LANES), F32)
    contract_lanes = (((1,), (1,)), ((), ()))
    o_blocks, glses = [], []
    for g in range(N_GROUPS):
        h0, h1, win, d = GROUPS[g]
        n_heads = h1 - h0
        cache_ref, new_ref = cache_refs[g], new_refs[g]
        lb = cache_ref.shape[-1]
        key = lax.broadcasted_iota(jnp.int32, (SUB, lb), 1)
        dist = lb + lax.broadcasted_iota(jnp.int32, (SUB, lb), 0) - key
        visible = ((dist & (d - 1)) == 0) & (dist <= win)
        dist_new = qry_row - new_row
        visible_new = (dist_new >= 0) & ((dist_new & (d - 1)) == 0) & (new_row < dec_s)
        tail_lane = lax.broadcasted_iota(jnp.int32, (HEAD_DIM, LANES), 1)
        lses = []
        for p in range(blocks_per_sec):
            heads = [h for h in (2 * p, 2 * p + 1) if h < n_heads]
            if not heads:
                o_blocks.append(jnp.zeros((SUB, LANES), F32))
                continue
            q2 = blocks[g * blocks_per_sec + p]
            k_new = blocks[(N_GROUPS + g) * blocks_per_sec + p]
            v_new = blocks[(2 * N_GROUPS + g) * blocks_per_sec + p]
            slabs = [[cache_ref[part, h] for h in heads] for part in range(2)]
            zero_slab = jnp.zeros((HEAD_DIM, lb), F32)
            kt2 = jnp.concatenate(slabs[0] + [zero_slab] * (2 - len(heads)), axis=0).astype(BF16)
            vt2 = jnp.concatenate(slabs[1] + [zero_slab] * (2 - len(heads)), axis=0).astype(BF16)
            q_both = jnp.concatenate([jnp.where(lo, q2, 0.0), jnp.where(lo, 0.0, q2)], axis=0).astype(BF16)
            s_both = jnp.dot(q_both, kt2, preferred_element_type=F32)
            s_new_both = lax.dot_general(q_both, k_new.astype(BF16), contract_lanes, preferred_element_type=F32)
            probs, probs_new = [], []
            for half in range(2):
                s = jnp.where(visible, s_both[half * SUB:(half + 1) * SUB], NEG_INF)
                s_new = jnp.where(visible_new, s_new_both[half * SUB:(half + 1) * SUB], NEG_INF)
                mx = jnp.maximum(jnp.max(s, axis=1, keepdims=True), jnp.max(s_new, axis=1, keepdims=True))
                pe = jnp.exp(s - mx)
                pe_new = jnp.exp(s_new - mx)
                l = jnp.sum(pe, axis=1, keepdims=True) + jnp.sum(pe_new, axis=1, keepdims=True)
                probs.append(pe / l)
                probs_new.append(pe_new / l)
                if half < len(heads):
                    lses.append(mx + jnp.log(l))
            pv = lax.dot_general(jnp.concatenate(probs, axis=0).astype(BF16), vt2, contract_lanes, preferred_element_type=F32)
            pv_new = jnp.dot(jnp.concatenate(probs_new, axis=0).astype(BF16), v_new.astype(BF16), preferred_element_type=F32)
            both = pv + pv_new
            o_blocks.append(jnp.where(lo, both[:SUB], both[SUB:]))
            for part, new in enumerate((k_new, v_new)):
                new_t = jnp.concatenate([new, pad_rows], axis=0).T
                for i, h in enumerate(heads):
                    shifted = pltpu.roll(slabs[part][i], lb - dec_s, 1)
                    tail = shifted[:, lb - LANES:]
                    feats = new_t[i * HEAD_DIM:(i + 1) * HEAD_DIM]
                    for jj in range(dec_s):
                        tail = jnp.where(tail_lane == LANES - dec_s + jj, feats[:, jj:jj + 1], tail)
                    if lb > LANES:
                        new_ref[part, h, :, :lb - LANES] = shifted[:, :lb - LANES]
                    new_ref[part, h, :, lb - LANES:] = tail
        lmax = functools.reduce(jnp.maximum, lses)
        lsum = functools.reduce(lambda a, b: a + b, [jnp.exp(l - lmax) for l in lses])
        glses.append(lmax + jnp.log(lsum) - math.log(n_heads))
    gmax = functools.reduce(jnp.maximum, glses)
    es = [jnp.exp(l - gmax) for l in glses]
    inv = float(N_GROUPS) / functools.reduce(lambda a, b: a + b, es)
    scaled = [o_blocks[g * blocks_per_sec + p] * (es[g] * inv) for g in range(N_GROUPS) for p in range(blocks_per_sec)]
    o_ref[0] = jnp.concatenate(scaled, axis=1)


def _sample_step(y_s, caches_t, new_caches_t, layer, cos_s, sin_s, dec_b, dec_s):
    n_cols = 3 * N_GROUPS * SEC
    y_pad = jnp.pad(y_s.reshape(dec_b, dec_s, n_cols), ((0, 0), (0, SUB - dec_s), (0, 0)))
    aliased = new_caches_t is not None
    cache_specs = []
    for (h0, h1, win, d), cache in zip(GROUPS, caches_t):
        assert cache.shape[-1] == win and dec_s <= LANES and win % LANES == 0
        cache_specs.append(pl.BlockSpec((None, None) + cache.shape[2:], lambda b: (layer, b, 0, 0, 0, 0)))
    in_specs = [
        pl.BlockSpec((1, SUB, n_cols), lambda b: (b, 0, 0)),
        pl.BlockSpec((SUB, LANES), lambda b: (0, 0)),
        pl.BlockSpec((SUB, LANES), lambda b: (0, 0)),
    ] + cache_specs
    args = [y_pad, cos_s, sin_s, *caches_t]
    aliases = {}
    if aliased:
        in_specs += [pl.BlockSpec(memory_space=pl.ANY)] * N_GROUPS
        aliases = {len(args) + g: 1 + g for g in range(N_GROUPS)}
        args += list(new_caches_t)
    outs = pl.pallas_call(
        functools.partial(_sample_step_kernel, dec_s=dec_s, aliased=aliased),
        grid=(dec_b,),
        in_specs=in_specs,
        out_specs=[pl.BlockSpec((1, SUB, N_GROUPS * SEC), lambda b: (b, 0, 0))] + cache_specs,
        out_shape=[jax.ShapeDtypeStruct((dec_b, SUB, N_GROUPS * SEC), F32)]
        + [jax.ShapeDtypeStruct(c.shape, F32) for c in caches_t],
        input_output_aliases=aliases,
        compiler_params=_params(52, "arbitrary"),
        name="sample_step",
    )(*args)
    return outs[0][:, :dec_s].reshape(dec_b * dec_s, N_GROUPS * SEC), outs[1:]


def _row2(a):
    return a.reshape(1, -1)


def _attn_layer(x_p, x_s, s_row0, caches_t, new_caches_t, layer, w_qkv, w_o, g1, b1, cos_p, sin_p, dims):
    batch, seq, dec_b, dec_s = dims
    t_prompt = batch * seq
    t_sample = dec_b * dec_s
    w_perm = _permute_qkv_weight(w_qkv)
    wo_perm = _permute_out_weight(w_o)
    outs = _qkv_prompt(x_p, w_perm, cos_p, sin_p, batch, seq)
    qkv_groups, windows = outs[:N_GROUPS], outs[N_GROUPS:]
    os_, ls_ = [], []
    for g in range(N_GROUPS):
        o_g, l_g = _attn_group(qkv_groups[g], g, batch, seq)
        os_.append(o_g)
        ls_.append(l_g)
    x_next = _attn_out_prompt(os_, ls_, wo_perm, x_p, g1, b1, batch, seq, t_prompt + t_sample)
    rows_p = []
    for g, (h0, h1, _, _) in enumerate(GROUPS):
        hg = h1 - h0
        halves = [windows[part * N_GROUPS + g][:, :, :hg * HEAD_DIM].reshape(batch, -1, hg, HEAD_DIM) for part in range(2)]
        rows_p.append(jnp.stack(halves, axis=2))
    y_s = _mm_rows(x_s, w_perm, s_row0, t_sample)
    cos_s, sin_s = _rope_tables(PAST_LEN + jnp.arange(SUB, dtype=jnp.int32))
    o_s, new_caches_t = _sample_step(y_s, caches_t, new_caches_t, layer, cos_s, sin_s, dec_b, dec_s)
    x_all = _proj_ln_rows(o_s, wo_perm.reshape(N_GROUPS * SEC, D_MODEL), x_s, s_row0, g1, b1, x_next, t_prompt)
    return x_all, rows_p, new_caches_t


def _sgu_layer(x_all, w_in, b_in, ln_g, ln_b, ws, bs, w_out, g1, b1, dims):
    batch, seq, dec_b, dec_s = dims
    t_prompt = batch * seq
    t_sample = dec_b * dec_s
    common = (w_in.astype(BF16), _row2(b_in), _row2(ln_g), _row2(ln_b))
    tail = (w_out.astype(BF16), g1, b1)
    bs_p = jnp.broadcast_to(bs[:, :, None], (SGU_GROUPS, SGU_CHUNK, SGU_CHUNK))
    x_next = _sgu(x_all, None, common + (ws, bs_p) + tail, 0, t_prompt, 512, SGU_CHUNK, False)[0]
    c = min(SGU_CHUNK, dec_s)
    reps = SGU_CHUNK // c
    ws_s = jnp.tile(ws[:, :c, :c], (1, reps, reps))
    bs_s = jnp.broadcast_to(jnp.tile(bs[:, :c], (1, reps))[:, :, None], (SGU_GROUPS, SGU_CHUNK, SGU_CHUNK))
    x_all, v_new = _sgu(x_all, x_next, common + (ws_s, bs_s) + tail, t_prompt, t_sample, t_sample, c, True)
    return x_all, v_new.reshape(dec_b, dec_s, D_MODEL)


def _moe_layer(x_all, w_router, b_router, w_gu, b_gu, w_dn, b_dn, ln_g, ln_b, layer):
    w_r = w_router.T
    b_r = jnp.broadcast_to(b_router[:, None], (N_EXPERTS, LANES))
    n_layers = w_gu.shape[0]
    return _moe(
        x_all, w_r, b_r,
        w_gu, b_gu.reshape(n_layers, N_EXPERTS, 1, -1),
        w_dn, b_dn.reshape(n_layers, N_EXPERTS, 1, -1),
        ln_g, ln_b, layer)


def kernel(x_prompt, x_sample, cache_kv_w128, cache_kv_w512, cache_kv_w2048, attn_w_qkv, attn_w_o, sgu_w_in, sgu_b_in, sgu_ln_g, sgu_ln_b, sgu_w_s, sgu_b_s, sgu_w_out, moe_w_router, moe_b_router, moe_w_gu, moe_b_gu, moe_w_down, moe_b_down, ln1_g, ln1_b, ln2_g, ln2_b):
    batch, seq, _ = x_prompt.shape
    dec_b, dec_s, _ = x_sample.shape
    t_prompt = batch * seq
    t_sample = dec_b * dec_s
    caches = (cache_kv_w128, cache_kv_w512, cache_kv_w2048)
    x_all = None
    cos_p, sin_p = _rope_tables(jnp.arange(seq, dtype=jnp.int32))
    row2 = _row2
    dims = (batch, seq, dec_b, dec_s)

    kv_prompt = [[] for _ in GROUPS]
    caches_t = [c.transpose(0, 1, 3, 4, 5, 2) for c in caches]
    new_caches_t = None
    v_rows = []
    for i in range(DEPTH):
        j = i // 2
        g1, b1 = row2(ln1_g[i]), row2(ln1_b[i])
        if i % 2 == 0:
            if x_all is None:
                sources = (x_prompt.reshape(t_prompt, D_MODEL), x_sample.reshape(t_sample, D_MODEL), 0)
            else:
                sources = (x_all, x_all, t_prompt)
            x_all, rows_p, new_caches_t = _attn_layer(
                *sources, caches_t, new_caches_t, j, attn_w_qkv[j], attn_w_o[j], g1, b1, cos_p, sin_p, dims)
            for g in range(N_GROUPS):
                kv_prompt[g].append(rows_p[g])
        else:
            x_all, v_new = _sgu_layer(
                x_all, sgu_w_in[j], sgu_b_in[j], sgu_ln_g[j], sgu_ln_b[j], sgu_w_s[j], sgu_b_s[j], sgu_w_out[j],
                g1, b1, dims)
            v_rows.append(v_new)
        x_all = _moe_layer(
            x_all, moe_w_router[i], moe_b_router[i], moe_w_gu, moe_b_gu, moe_w_down, moe_b_down,
            row2(ln2_g[i]), row2(ln2_b[i]), i)
    y_prompt = x_all[:t_prompt].reshape(batch, seq, D_MODEL)
    y_sample = x_all[t_prompt:].reshape(dec_b, dec_s, D_MODEL)
    kv_sample = [c.transpose(0, 1, 5, 2, 3, 4) for c in new_caches_t]
    return (
        y_prompt, y_sample,
        jnp.stack(kv_prompt[0]), jnp.stack(kv_prompt[1]), jnp.stack(kv_prompt[2]),
        kv_sample[0], kv_sample[1], kv_sample[2],
        jnp.stack(v_rows),
    )
```

```python
import functools
import math

import jax
import jax.numpy as jnp
from jax import lax
from jax.experimental import pallas as pl
from jax.experimental.pallas import tpu as pltpu

F32 = jnp.float32
BF16 = jnp.bfloat16

D_MODEL = 1024
HEAD_DIM = 64
N_HEADS = D_MODEL // HEAD_DIM
GROUPS = ((0, 6, 128, 1), (6, 11, 512, 4), (11, 16, 2048, 16))
N_GROUPS = len(GROUPS)
Q_BLOCK = 128
SEC = 384
ROPE_THETA = 10000.0
NEG_INF = -1e30
PAST_LEN = 8192
SGU_GROUPS = 8
SGU_CHUNK = 128
N_EXPERTS = 32
TOP_K = 4
SWIGLU_LIMIT = 7.0
SWIGLU_ALPHA = 1.702
MOE_BLOCK = 512
MOE_TILE = 512
EXACT_BF16_INT = 256
BIG_CHUNK_BIT = 6
SUB = 8
DEPTH = 4
DEEPNORM_ALPHA = (2 * DEPTH) ** 0.25
LN_EPS = 1e-5
LANES = 128
MIB = 1024 * 1024


def _params(vmem_mib, *semantics):
    return pltpu.CompilerParams(dimension_semantics=semantics, vmem_limit_bytes=vmem_mib * MIB)


def _ln(x, g, b):
    mu = jnp.mean(x, axis=-1, keepdims=True)
    xc = x - mu
    var = jnp.mean(xc * xc, axis=-1, keepdims=True)
    return xc * lax.rsqrt(var + LN_EPS) * g + b


def _qkv_prompt_kernel(x_ref, w_ref, cos_ref, sin_ref, o1_ref, o2_ref, o3_ref, *rest, bm):
    win_refs = rest[:2 * N_GROUPS]
    y_s = rest[-1]
    y = jnp.dot(x_ref[...].astype(BF16), w_ref[...], preferred_element_type=F32)
    cos = cos_ref[...]
    sin = sin_ref[...]
    lane = lax.broadcasted_iota(jnp.int32, (bm, LANES), 1)
    first_half = (lane & (HEAD_DIM // 2)) == 0
    blocks_per_sec = SEC // LANES
    n_rot = 2 * N_GROUPS * blocks_per_sec
    for c in range(3 * N_GROUPS * blocks_per_sec):
        blk = y[:, c * LANES:(c + 1) * LANES]
        if c < n_rot:
            swapped = jnp.where(first_half, pltpu.roll(blk, LANES - HEAD_DIM // 2, 1), pltpu.roll(blk, HEAD_DIM // 2, 1))
            blk = blk * cos + swapped * sin
        if c < n_rot // 2:
            blk = blk * (HEAD_DIM ** -0.5)
        else:
            sec, p = divmod(c - n_rot // 2, blocks_per_sec)
            win_ref = win_refs[sec]
            keep = win_ref.shape[1]
            win_ref[0, :, p * LANES:(p + 1) * LANES] = blk[bm - keep:, :]
        y_s[c] = blk
    for g, o_ref in enumerate((o1_ref, o2_ref, o3_ref)):
        d = GROUPS[g][3]
        n = bm // d
        for r in range(d):
            for part in range(3):
                for p in range(blocks_per_sec):
                    c = (part * N_GROUPS + g) * blocks_per_sec + p
                    rows = y_s[c] if d == 1 else y_s[c, pl.ds(r, n, stride=d), :]
                    o_ref[0, r, :, part * SEC + p * LANES:part * SEC + (p + 1) * LANES] = rows.astype(BF16)


def _qkv_prompt(x_all, w_perm, cos, sin, batch, seq):
    bm = 512
    tiles = seq // bm
    out_shape = [jax.ShapeDtypeStruct((batch, d, seq // d, 3 * SEC), BF16) for (_, _, _, d) in GROUPS]
    out_specs = [pl.BlockSpec((1, d, bm // d, 3 * SEC), lambda b, i: (b, 0, i, 0)) for (_, _, _, d) in GROUPS]
    for _ in range(2):
        for (_, _, win, _) in GROUPS:
            keep = min(win, seq)
            rows = min(keep, bm)
            first = tiles - keep // rows
            out_shape.append(jax.ShapeDtypeStruct((batch, keep, SEC), F32))
            out_specs.append(pl.BlockSpec((1, rows, SEC), lambda b, i, first=first: (b, jnp.maximum(i - first, 0), 0)))
    n_cols = 3 * N_GROUPS * SEC
    return pl.pallas_call(
        functools.partial(_qkv_prompt_kernel, bm=bm),
        grid=(batch, tiles),
        in_specs=[
            pl.BlockSpec((bm, D_MODEL), lambda b, i: (b * tiles + i, 0)),
            pl.BlockSpec((D_MODEL, n_cols), lambda b, i: (0, 0)),
            pl.BlockSpec((bm, LANES), lambda b, i: (i, 0)),
            pl.BlockSpec((bm, LANES), lambda b, i: (i, 0)),
        ],
        out_specs=out_specs,
        out_shape=out_shape,
        scratch_shapes=[pltpu.VMEM((n_cols // LANES, bm, LANES), F32)],
        compiler_params=_params(48, "arbitrary", "arbitrary"),
        name="qkv_prompt",
    )(x_all, w_perm, cos, sin)


def _attn_group_kernel(qkv_ref, o_ref, lse_ref, *, d, m_len, n_heads):
    nb = m_len // Q_BLOCK
    kw = min(2 * Q_BLOCK, m_len)
    lane = lax.broadcasted_iota(jnp.int32, (1, LANES), 1)
    lo = lane < HEAD_DIM
    qi = lax.broadcasted_iota(jnp.int32, (Q_BLOCK, kw), 0)
    kj = lax.broadcasted_iota(jnp.int32, (Q_BLOCK, kw), 1)

    def block(i, carry):
        r = i // nb
        n = i % nb
        ks = pl.multiple_of(jnp.maximum(n - 1, 0) * Q_BLOCK, Q_BLOCK)
        qs = pl.multiple_of(n * Q_BLOCK, Q_BLOCK)
        delta = qs - ks + qi - kj
        valid = (delta >= 0) & (delta <= Q_BLOCK)
        if d == 1:
            dst = pl.ds(qs, Q_BLOCK)
        else:
            dst = pl.ds(r + qs * d, Q_BLOCK, stride=d)
        lses = []
        for p in range(SEC // LANES):
            cols = slice(p * LANES, (p + 1) * LANES)
            q2 = qkv_ref[0, r, pl.ds(qs, Q_BLOCK), cols]
            k2 = qkv_ref[0, r, pl.ds(ks, kw), SEC + p * LANES:SEC + (p + 1) * LANES]
            v2 = qkv_ref[0, r, pl.ds(ks, kw), 2 * SEC + p * LANES:2 * SEC + (p + 1) * LANES]
            n_half = min(2, n_heads - 2 * p)
            if n_half <= 0:
                o_ref[0, p, dst, :] = jnp.zeros((Q_BLOCK, LANES), F32)
                continue
            zero = jnp.zeros_like(q2)
            q_both = jnp.concatenate([jnp.where(lo, q2, zero), jnp.where(lo, zero, q2)][:n_half], axis=0)
            s_both = lax.dot_general(q_both, k2, (((1,), (1,)), ((), ())), preferred_element_type=F32)
            probs, inv_l = [], []
            for half in range(n_half):
                s = jnp.where(valid, s_both[half * Q_BLOCK:(half + 1) * Q_BLOCK], NEG_INF)
                mx = jnp.max(s, axis=1, keepdims=True)
                pe = jnp.exp(s - mx)
                l = jnp.sum(pe, axis=1, keepdims=True)
                probs.append(pe.astype(BF16))
                inv_l.append(1.0 / l)
                lses.append(mx + jnp.log(l))
            pv = jnp.dot(jnp.concatenate(probs, axis=0), v2, preferred_element_type=F32)
            o_pair = pv[:Q_BLOCK] * inv_l[0]
            if n_half == 2:
                o_pair = jnp.where(lo, o_pair, pv[Q_BLOCK:] * inv_l[1])
            else:
                o_pair = jnp.where(lo, o_pair, 0.0)
            o_ref[0, p, dst, :] = o_pair
        lmax = functools.reduce(jnp.maximum, lses)
        lsum = functools.reduce(lambda a, b: a + b, [jnp.exp(l - lmax) for l in lses])
        glse = lmax + jnp.log(lsum) - math.log(n_heads)
        lse_ref[0, dst, :] = jnp.broadcast_to(glse, (Q_BLOCK, LANES))
        return carry

    lax.fori_loop(0, d * nb, block, 0)


def _attn_group(qkv_g, g, batch, seq):
    h0, h1, _, d = GROUPS[g]
    m_len = seq // d
    return pl.pallas_call(
        functools.partial(_attn_group_kernel, d=d, m_len=m_len, n_heads=h1 - h0),
        grid=(batch,),
        in_specs=[pl.BlockSpec((1, d, m_len, 3 * SEC), lambda b: (b, 0, 0, 0))],
        out_specs=[
            pl.BlockSpec((1, SEC // LANES, seq, LANES), lambda b: (b, 0, 0, 0)),
            pl.BlockSpec((1, seq, LANES), lambda b: (b, 0, 0)),
        ],
        out_shape=[
            jax.ShapeDtypeStruct((batch, SEC // LANES, seq, LANES), F32),
            jax.ShapeDtypeStruct((batch, seq, LANES), F32),
        ],
        compiler_params=_params(40, "arbitrary"),
        name=f"attn_group{g}",
    )(qkv_g)


def _attn_out_kernel(o1_ref, o2_ref, o3_ref, l1_ref, l2_ref, l3_ref, wo_ref, x_ref, g_ref, b_ref, out_ref):
    ls = [l1_ref[0], l2_ref[0], l3_ref[0]]
    mx = jnp.maximum(jnp.maximum(ls[0], ls[1]), ls[2])
    es = [jnp.exp(l - mx) for l in ls]
    inv = float(N_GROUPS) / (es[0] + es[1] + es[2])
    acc = None
    for g, o_ref in enumerate((o1_ref, o2_ref, o3_ref)):
        w = es[g] * inv
        o_g = jnp.concatenate([o_ref[0, p] * w for p in range(SEC // LANES)], axis=1)
        part = jnp.dot(o_g.astype(BF16), wo_ref[g], preferred_element_type=F32)
        acc = part if acc is None else acc + part
    out_ref[...] = _ln(DEEPNORM_ALPHA * x_ref[...] + acc, g_ref[...], b_ref[...])


def _attn_out_prompt(os_, ls_, wo_perm, x_src, ln_g, ln_b, batch, seq, t_all):
    bm = 1024
    tiles = seq // bm
    row = lambda b, i: (b * tiles + i, 0)
    const2 = lambda b, i: (0, 0)
    return pl.pallas_call(
        _attn_out_kernel,
        grid=(batch, tiles),
        in_specs=[pl.BlockSpec((1, SEC // LANES, bm, LANES), lambda b, i: (b, 0, i, 0))] * 3
        + [pl.BlockSpec((1, bm, LANES), lambda b, i: (b, i, 0))] * 3 + [
            pl.BlockSpec((N_GROUPS, SEC, D_MODEL), lambda b, i: (0, 0, 0)),
            pl.BlockSpec((bm, D_MODEL), row),
            pl.BlockSpec((1, D_MODEL), const2),
            pl.BlockSpec((1, D_MODEL), const2),
        ],
        out_specs=pl.BlockSpec((bm, D_MODEL), row),
        out_shape=jax.ShapeDtypeStruct((t_all, D_MODEL), F32),
        compiler_params=_params(40, "arbitrary", "arbitrary"),
        name="attn_out_prompt",
    )(*os_, *ls_, wo_perm, x_src, ln_g, ln_b)


def _mm_kernel(x_ref, w_ref, o_ref):
    o_ref[...] = jnp.dot(x_ref[...].astype(BF16), w_ref[...], preferred_element_type=F32)


def _mm_rows(x_all, w, row0, rows):
    n = w.shape[1]
    blk0 = row0 // rows
    return pl.pallas_call(
        _mm_kernel,
        grid=(1,),
        in_specs=[
            pl.BlockSpec((rows, D_MODEL), lambda i: (blk0, 0)),
            pl.BlockSpec((D_MODEL, n), lambda i: (0, 0)),
        ],
        out_specs=pl.BlockSpec((rows, n), lambda i: (0, 0)),
        out_shape=jax.ShapeDtypeStruct((rows, n), F32),
        compiler_params=_params(40, "arbitrary"),
        name="mm_rows",
    )(x_all, w)


def _proj_ln_kernel(a_ref, w_ref, x_ref, g_ref, b_ref, prev_ref, out_ref):
    del prev_ref
    acc = jnp.dot(a_ref[...].astype(BF16), w_ref[...], preferred_element_type=F32)
    out_ref[...] = _ln(DEEPNORM_ALPHA * x_ref[...] + acc, g_ref[...], b_ref[...])


def _proj_ln_rows(a, w, x_src, src_row0, ln_g, ln_b, x_next, row0):
    rows = a.shape[0]
    blk0 = row0 // rows
    src_blk0 = src_row0 // rows
    const2 = lambda i: (0, 0)
    return pl.pallas_call(
        _proj_ln_kernel,
        grid=(1,),
        in_specs=[
            pl.BlockSpec((rows, a.shape[1]), const2),
            pl.BlockSpec(w.shape, const2),
            pl.BlockSpec((rows, D_MODEL), lambda i: (src_blk0, 0)),
            pl.BlockSpec((1, D_MODEL), const2),
            pl.BlockSpec((1, D_MODEL), const2),
            pl.BlockSpec(memory_space=pl.ANY),
        ],
        out_specs=pl.BlockSpec((rows, D_MODEL), lambda i: (blk0, 0)),
        out_shape=jax.ShapeDtypeStruct(x_next.shape, F32),
        input_output_aliases={5: 0},
        compiler_params=_params(40, "arbitrary"),
        name="proj_ln_rows",
    )(a, w, x_src, ln_g, ln_b, x_next)


def _sgu_kernel(*refs, bm, chunk, emit_v, aliased):
    (x_ref, win_ref, bin_ref, lng_ref, lnb_ref, ws_ref, bs_ref, wout_ref, g1_ref, b1_ref) = refs[:10]
    rest = refs[10 + (1 if aliased else 0):]
    out_ref = rest[0]
    v_ref = rest[1] if emit_v else None
    ug_s = rest[-1]
    width = D_MODEL
    x = x_ref[...]
    z = jnp.dot(x.astype(BF16), win_ref[...], preferred_element_type=F32) + bin_ref[...]
    z = 0.5 * z * (1.0 + lax.erf(z * (2.0 ** -0.5)))
    v = _ln(z[:, width:], lng_ref[...], lnb_ref[...])
    if emit_v:
        v_ref[...] = v
    ii = lax.broadcasted_iota(jnp.int32, (SGU_CHUNK, SGU_CHUNK), 0)
    jj = lax.broadcasted_iota(jnp.int32, (SGU_CHUNK, SGU_CHUNK), 1)
    causal = (jj <= ii) & ((ii // chunk) == (jj // chunk))
    gw = width // SGU_GROUPS
    for gi in range(SGU_GROUPS):
        wsm = jnp.where(causal, ws_ref[gi], 0.0).astype(BF16)
        for c in range(bm // SGU_CHUNK):
            rows = slice(c * SGU_CHUNK, (c + 1) * SGU_CHUNK)
            cols = slice(gi * gw, (gi + 1) * gw)
            gate = jnp.dot(wsm, v[rows, cols].astype(BF16), preferred_element_type=F32) + bs_ref[gi]
            ug_s[rows, cols] = (z[rows, cols] * gate).astype(BF16)
    y = jnp.dot(ug_s[...], wout_ref[...], preferred_element_type=F32)
    out_ref[...] = _ln(DEEPNORM_ALPHA * x + y, g1_ref[...], b1_ref[...])


def _sgu(x_all, x_next, weights, row0, rows, bm, chunk, emit_v):
    w_in, b_in, ln_g, ln_b, ws, bs, w_out, g1, b1 = weights
    t_all = x_all.shape[0]
    blk0 = row0 // bm
    row = lambda i: (blk0 + i, 0)
    const2 = lambda i: (0, 0)
    const3 = lambda i: (0, 0, 0)
    aliased = x_next is not None
    in_specs = [
        pl.BlockSpec((bm, D_MODEL), row),
        pl.BlockSpec(w_in.shape, const2),
        pl.BlockSpec((1, 2 * D_MODEL), const2),
        pl.BlockSpec((1, D_MODEL), const2),
        pl.BlockSpec((1, D_MODEL), const2),
        pl.BlockSpec(ws.shape, const3),
        pl.BlockSpec(bs.shape, const3),
        pl.BlockSpec(w_out.shape, const2),
        pl.BlockSpec((1, D_MODEL), const2),
        pl.BlockSpec((1, D_MODEL), const2),
    ]
    args = [x_all, w_in, b_in, ln_g, ln_b, ws, bs, w_out, g1, b1]
    aliases = {}
    if aliased:
        in_specs.append(pl.BlockSpec(memory_space=pl.ANY))
        args.append(x_next)
        aliases = {10: 0}
    out_specs = [pl.BlockSpec((bm, D_MODEL), row)]
    out_shape = [jax.ShapeDtypeStruct((t_all, D_MODEL), F32)]
    if emit_v:
        out_specs.append(pl.BlockSpec((bm, D_MODEL), lambda i: (i, 0)))
        out_shape.append(jax.ShapeDtypeStruct((rows, D_MODEL), F32))
    return pl.pallas_call(
        functools.partial(_sgu_kernel, bm=bm, chunk=chunk, emit_v=emit_v, aliased=aliased),
        grid=(rows // bm,),
        in_specs=in_specs,
        out_specs=out_specs,
        out_shape=out_shape,
        scratch_shapes=[pltpu.VMEM((bm, D_MODEL), BF16)],
        input_output_aliases=aliases,
        compiler_params=_params(48, "arbitrary"),
        name="sgu",
    )(*args)


def _router_kernel(x_ref, wr_ref, br_ref, idx_ref, gate_ref, cnt_ref, *, bm, t_all):
    x = x_ref[...]
    w = wr_ref[...]
    x_hi = x.astype(BF16)
    w_hi = w.astype(BF16)
    x_lo = (x - x_hi.astype(F32)).astype(BF16)
    w_lo = (w - w_hi.astype(F32)).astype(BF16)
    contract = (((1,), (1,)), ((), ()))
    both = lax.dot_general(jnp.concatenate([w_hi, w_lo], axis=0), x_hi, contract, preferred_element_type=F32)
    logits = (both[:N_EXPERTS] + lax.dot_general(w_hi, x_lo, contract, preferred_element_type=F32)
              + both[N_EXPERTS:])
    logits = logits + jnp.concatenate([br_ref[...]] * (bm // LANES), axis=1)
    expert = lax.broadcasted_iota(jnp.int32, (N_EXPERTS, bm), 0).astype(F32)
    token = pl.program_id(0) * bm + lax.broadcasted_iota(jnp.int32, (N_EXPERTS, bm), 1)
    logits = jnp.where(token < t_all, logits, 0.0)
    vals, idxs = [], []
    chosen = jnp.zeros((N_EXPERTS, bm), F32)
    for _ in range(TOP_K):
        m = jnp.max(logits, axis=0, keepdims=True)
        i = jnp.min(jnp.where(logits == m, expert, float(N_EXPERTS)), axis=0, keepdims=True)
        vals.append(m)
        idxs.append(i)
        hit = expert == i
        chosen = chosen + jnp.where(hit & (token < t_all), 1.0, 0.0)
        logits = jnp.where(hit, -jnp.inf, logits)
    es = [jnp.exp(v - vals[0]) for v in vals]
    inv = 1.0 / functools.reduce(lambda a, b: a + b, es)
    sub = lax.broadcasted_iota(jnp.int32, (2 * TOP_K, bm), 0)
    head = jnp.zeros((2 * TOP_K, bm), F32)
    for k in range(TOP_K):
        head = jnp.where(sub == k, idxs[k], head)
        head = jnp.where(sub == TOP_K + k, es[k] * inv, head)
    pad = jnp.zeros((LANES - 2 * TOP_K - N_EXPERTS, bm), F32)
    packed = jnp.concatenate([head, chosen, pad], axis=0).T
    lane = lax.broadcasted_iota(jnp.int32, (bm, LANES), 1)
    idx_ref[...] = jnp.where(lane < TOP_K, packed, 0.0).astype(jnp.int32)
    gate_ref[...] = jnp.where(lane < TOP_K, pltpu.roll(packed, LANES - TOP_K, 1), 0.0)
    marks = jnp.where(lane < N_EXPERTS, pltpu.roll(packed, LANES - 2 * TOP_K, 1), 0.0)
    counts = jnp.sum(marks, axis=0, keepdims=True)
    cnt_ref[0] = jnp.broadcast_to(counts, (SUB, LANES))


def _router(x_all, w_r, b_r):
    bm = MOE_TILE
    t_all = x_all.shape[0]
    n_tiles = pl.cdiv(t_all, bm)
    row = lambda i: (i, 0)
    const2 = lambda i: (0, 0)
    return pl.pallas_call(
        functools.partial(_router_kernel, bm=bm, t_all=t_all),
        grid=(n_tiles,),
        in_specs=[
            pl.BlockSpec((bm, D_MODEL), row),
            pl.BlockSpec((N_EXPERTS, D_MODEL), const2),
            pl.BlockSpec((N_EXPERTS, LANES), const2),
        ],
        out_specs=[pl.BlockSpec((bm, LANES), row), pl.BlockSpec((bm, LANES), row),
                   pl.BlockSpec((1, SUB, LANES), lambda i: (i, 0, 0))],
        out_shape=[
            jax.ShapeDtypeStruct((t_all, LANES), jnp.int32),
            jax.ShapeDtypeStruct((t_all, LANES), F32),
            jax.ShapeDtypeStruct((n_tiles, SUB, LANES), F32),
        ],
        compiler_params=_params(32, "arbitrary"),
        name="router",
    )(x_all, w_r, b_r)


def _local_slots(idx, row_valid):
    tile = idx.shape[0]
    lane = lax.broadcasted_iota(jnp.int32, (tile, LANES), 1)
    idx = jnp.where(row_valid, idx, -1)
    picks = [lane == idx[:, k:k + 1] for k in range(TOP_K)]
    chosen = functools.reduce(lambda a, b: a + b, [p.astype(F32) for p in picks])
    earlier = (lax.broadcasted_iota(jnp.int32, (tile, tile), 1)
               < lax.broadcasted_iota(jnp.int32, (tile, tile), 0)).astype(BF16)
    rank = jnp.dot(earlier, chosen.astype(BF16), preferred_element_type=F32)
    lower = (lax.broadcasted_iota(jnp.int32, (LANES, LANES), 0)
             < lax.broadcasted_iota(jnp.int32, (LANES, LANES), 1)).astype(BF16)
    base = jnp.zeros((1, LANES), F32)
    for r0 in range(0, tile, EXACT_BF16_INT):
        per_expert = jnp.sum(chosen[r0:r0 + EXACT_BF16_INT], axis=0, keepdims=True)
        per_expert = jnp.broadcast_to(per_expert, (SUB, LANES)).astype(BF16)
        base = base + jnp.dot(per_expert, lower, preferred_element_type=F32)[0:1]
    place = base + rank + 1.0
    return [jnp.sum(jnp.where(p, place, 0.0), axis=1, keepdims=True) - 1.0 for p in picks]


def _segment_copies(cnt_ref, off_ref, j, hbm, buf, sem, to_hbm):
    local = 0
    for e in range(N_EXPERTS):
        cnt = cnt_ref[j * N_EXPERTS + e]
        off = off_ref[j * N_EXPERTS + e]
        def chunks(bits, cnt=cnt, off=off, local=local):
            for bit in bits:
                size = 1 << bit

                @pl.when((cnt & size) != 0)
                def _(size=size):
                    done = cnt & (size - 1)
                    a = buf.at[pl.ds(pl.multiple_of((local + done) * SUB, SUB), size * SUB), :]
                    b = hbm.at[pl.ds(pl.multiple_of((off + done) * SUB, SUB), size * SUB), :]
                    (pltpu.make_async_copy(a, b, sem) if to_hbm else pltpu.make_async_copy(b, a, sem)).start()

        n_bits = MOE_TILE.bit_length()
        chunks(range(BIG_CHUNK_BIT))
        pl.when(cnt >= (1 << BIG_CHUNK_BIT))(functools.partial(chunks, range(BIG_CHUNK_BIT, n_bits)))
        local = local + cnt


def _rows_to_tiles(ref, value):
    rows = value.shape[0]
    for s in range(D_MODEL // LANES):
        ref[pl.ds(s, rows, stride=SUB), :] = value[:, s * LANES:(s + 1) * LANES]


def _tiles_to_rows(ref, rows):
    return jnp.concatenate([ref[pl.ds(s, rows, stride=SUB), :] for s in range(D_MODEL // LANES)], axis=1)


def _dispatch_kernel(cnt_ref, off_ref, x_ref, idx_ref, xs_hbm, slot_ref, buf, sem, *, t_all, n_tiles):
    j = pl.program_id(0)
    par = j % 2
    tile = MOE_TILE
    row = j * tile + lax.broadcasted_iota(jnp.int32, (tile, LANES), 0)
    row_valid = row < t_all
    slots = _local_slots(idx_ref[...], row_valid)
    lane = lax.broadcasted_iota(jnp.int32, (tile, LANES), 1)
    slot_out = jnp.zeros((tile, LANES), F32)
    for k in range(TOP_K):
        slot_out = jnp.where(lane == k, slots[k], slot_out)
    slot_ref[...] = slot_out
    slots_t = slot_out.T
    pos = lax.broadcasted_iota(jnp.int32, (TOP_K * tile, tile), 0).astype(F32)
    hit = functools.reduce(jnp.logical_or, [pos == slots_t[k:k + 1, :] for k in range(TOP_K)])
    place = jnp.where(hit, 1.0, 0.0).astype(BF16)
    in_range = (j * tile + lax.broadcasted_iota(jnp.int32, (tile, D_MODEL), 0)) < t_all
    x = jnp.where(in_range, x_ref[...], 0.0).astype(BF16)
    ordered = jnp.dot(place, x, preferred_element_type=F32)
    _rows_to_tiles(buf.at[par], ordered)

    full = TOP_K * tile * SUB
    last = TOP_K * (t_all - (n_tiles - 1) * tile) * SUB

    @pl.when(j > 0)
    def _():
        pltpu.make_async_copy(buf.at[1 - par], xs_hbm.at[pl.ds(0, full), :], sem.at[0]).wait()

    _segment_copies(cnt_ref, off_ref, j, xs_hbm, buf.at[par], sem.at[0], True)

    @pl.when(j == n_tiles - 1)
    def _():
        pltpu.make_async_copy(buf.at[par, pl.ds(0, last), :], xs_hbm.at[pl.ds(0, last), :], sem.at[0]).wait()


def _dispatch(cnt, off, x_all, idx, n_slots):
    t_all = x_all.shape[0]
    n_tiles = pl.cdiv(t_all, MOE_TILE)
    row = lambda j, c, o: (j, 0)
    grid_spec = pltpu.PrefetchScalarGridSpec(
        num_scalar_prefetch=2,
        grid=(n_tiles,),
        in_specs=[pl.BlockSpec((MOE_TILE, D_MODEL), row), pl.BlockSpec((MOE_TILE, LANES), row)],
        out_specs=[pl.BlockSpec(memory_space=pl.ANY), pl.BlockSpec((MOE_TILE, LANES), row)],
        scratch_shapes=[pltpu.VMEM((2, TOP_K * MOE_TILE * SUB, LANES), F32), pltpu.SemaphoreType.DMA((1,))],
    )
    return pl.pallas_call(
        functools.partial(_dispatch_kernel, t_all=t_all, n_tiles=n_tiles),
        grid_spec=grid_spec,
        out_shape=[jax.ShapeDtypeStruct((n_slots * SUB, LANES), F32), jax.ShapeDtypeStruct((t_all, LANES), F32)],
        compiler_params=_params(40, "arbitrary"),
        name="moe_dispatch",
    )(cnt, off, x_all, idx)


def _expert_kernel(be_ref, nreal_ref, first_ref, par_ref, next_ref, xs_ref, wgu_hbm, bgu_ref, wdn_hbm, bdn_ref, ys_ref,
                   wgu_f32, wdn_f32, wgu_bf, wdn_bf, sem, *, layer):
    i = pl.program_id(0)
    ff = wdn_bf.shape[0]

    def copies(e, slot):
        return (pltpu.make_async_copy(wgu_hbm.at[layer, e], wgu_f32.at[slot], sem.at[0, slot]),
                pltpu.make_async_copy(wdn_hbm.at[layer, e], wdn_f32.at[slot], sem.at[1, slot]))

    @pl.when(i == 0)
    def _():
        for c in copies(be_ref[0], 0):
            c.start()

    @pl.when(i < nreal_ref[0])
    def _():
        @pl.when(first_ref[i] == 1)
        def _():
            slot = par_ref[i]
            for c in copies(be_ref[i], slot):
                c.wait()
            wgu_bf[...] = wgu_f32[slot].astype(BF16)
            wdn_bf[...] = wdn_f32[slot].astype(BF16)

            @pl.when(next_ref[i] >= 0)
            def _():
                for c in copies(next_ref[i], 1 - slot):
                    c.start()

        x = _tiles_to_rows(xs_ref, MOE_BLOCK).astype(BF16)
        h = jnp.dot(x, wgu_bf[...], preferred_element_type=F32) + bgu_ref[0]
        gate = jnp.minimum(h[:, :ff], SWIGLU_LIMIT)
        up = jnp.clip(h[:, ff:], -SWIGLU_LIMIT, SWIGLU_LIMIT)
        act = (up + 1.0) * gate * jax.nn.sigmoid(SWIGLU_ALPHA * gate)
        y = jnp.dot(act.astype(BF16), wdn_bf[...], preferred_element_type=F32) + bdn_ref[0]
        _rows_to_tiles(ys_ref, y)


def _experts(block_expert, n_real, run_first, run_parity, next_expert, xs, w_gu, b_gu, w_dn, b_dn, layer):
    n_blocks = xs.shape[0] // (MOE_BLOCK * SUB)
    ff = w_dn.shape[2]
    rows = lambda i, be, nr, *_: (jnp.minimum(i, nr[0] - 1), 0)
    by_expert = lambda i, be, *_: (layer, be[i], 0, 0)
    grid_spec = pltpu.PrefetchScalarGridSpec(
        num_scalar_prefetch=5,
        grid=(n_blocks,),
        in_specs=[
            pl.BlockSpec((MOE_BLOCK * SUB, LANES), rows),
            pl.BlockSpec(memory_space=pl.ANY),
            pl.BlockSpec((None, 1, 1, 2 * ff), by_expert),
            pl.BlockSpec(memory_space=pl.ANY),
            pl.BlockSpec((None, 1, 1, D_MODEL), by_expert),
        ],
        out_specs=pl.BlockSpec((MOE_BLOCK * SUB, LANES), rows),
        scratch_shapes=[
            pltpu.VMEM((2, D_MODEL, 2 * ff), F32),
            pltpu.VMEM((2, ff, D_MODEL), F32),
            pltpu.VMEM((D_MODEL, 2 * ff), BF16),
            pltpu.VMEM((ff, D_MODEL), BF16),
            pltpu.SemaphoreType.DMA((2, 2)),
        ],
    )
    return pl.pallas_call(
        functools.partial(_expert_kernel, layer=layer),
        grid_spec=grid_spec,
        out_shape=jax.ShapeDtypeStruct(xs.shape, F32),
        compiler_params=_params(56, "arbitrary"),
        name="experts",
    )(block_expert, n_real, run_first, run_parity, next_expert, xs, w_gu, b_gu, w_dn, b_dn)


def _combine_kernel(cnt_ref, off_ref, x_ref, slot_ref, gate_ref, g_ref, b_ref, ys_hbm, out_ref, buf, sem, *, t_all, n_tiles):
    j = pl.program_id(0)
    par = j % 2
    tile = MOE_TILE
    full = TOP_K * tile * SUB
    last = TOP_K * (t_all - (n_tiles - 1) * tile) * SUB

    @pl.when(j == 0)
    def _():
        _segment_copies(cnt_ref, off_ref, 0, ys_hbm, buf.at[0], sem.at[0], False)

    @pl.when(j + 1 < n_tiles)
    def _():
        _segment_copies(cnt_ref, off_ref, j + 1, ys_hbm, buf.at[1 - par], sem.at[1 - par], False)

    @pl.when(j < n_tiles - 1)
    def _():
        pltpu.make_async_copy(ys_hbm.at[pl.ds(0, full), :], buf.at[par], sem.at[par]).wait()

    @pl.when(j == n_tiles - 1)
    def _():
        pltpu.make_async_copy(ys_hbm.at[pl.ds(0, last), :], buf.at[par, pl.ds(0, last), :], sem.at[par]).wait()

    n_valid = TOP_K * jnp.minimum(tile, t_all - j * tile)
    ys = _tiles_to_rows(buf.at[par], TOP_K * tile)
    ys = jnp.where(lax.broadcasted_iota(jnp.int32, ys.shape, 0) < n_valid, ys, 0.0).astype(BF16)
    slots = slot_ref[...]
    gates = gate_ref[...]
    pos = lax.broadcasted_iota(jnp.int32, (tile, TOP_K * tile), 1).astype(F32)
    weights = jnp.zeros((tile, TOP_K * tile), F32)
    for k in range(TOP_K):
        weights = jnp.where(pos == slots[:, k:k + 1], gates[:, k:k + 1], weights)
    y = jnp.dot(weights.astype(BF16), ys, preferred_element_type=F32)
    out_ref[...] = _ln(DEEPNORM_ALPHA * x_ref[...] + y, g_ref[...], b_ref[...])


def _combine(cnt, off, x_all, slots, gates, ln_g, ln_b, ys):
    t_all = x_all.shape[0]
    n_tiles = pl.cdiv(t_all, MOE_TILE)
    row = lambda j, c, o: (j, 0)
    const2 = lambda j, c, o: (0, 0)
    grid_spec = pltpu.PrefetchScalarGridSpec(
        num_scalar_prefetch=2,
        grid=(n_tiles,),
        in_specs=[
            pl.BlockSpec((MOE_TILE, D_MODEL), row),
            pl.BlockSpec((MOE_TILE, LANES), row),
            pl.BlockSpec((MOE_TILE, LANES), row),
            pl.BlockSpec((1, D_MODEL), const2),
            pl.BlockSpec((1, D_MODEL), const2),
            pl.BlockSpec(memory_space=pl.ANY),
        ],
        out_specs=pl.BlockSpec((MOE_TILE, D_MODEL), row),
        scratch_shapes=[pltpu.VMEM((2, TOP_K * MOE_TILE * SUB, LANES), F32), pltpu.SemaphoreType.DMA((2,))],
    )
    return pl.pallas_call(
        functools.partial(_combine_kernel, t_all=t_all, n_tiles=n_tiles),
        grid_spec=grid_spec,
        out_shape=jax.ShapeDtypeStruct((t_all, D_MODEL), F32),
        compiler_params=_params(40, "arbitrary"),
        name="moe_combine",
    )(cnt, off, x_all, slots, gates, ln_g, ln_b, ys)


def _moe(x_all, w_r, b_r, w_gu, b_gu, w_dn, b_dn, ln_g, ln_b, layer):
    t_all = x_all.shape[0]
    idx, gates, tile_counts = _router(x_all, w_r, b_r)
    cnt = tile_counts[:, 0, :N_EXPERTS].astype(jnp.int32)
    counts = jnp.sum(cnt, axis=0)
    padded = (counts + MOE_BLOCK - 1) // MOE_BLOCK * MOE_BLOCK
    pad_end = jnp.cumsum(padded)
    pad_start = pad_end - padded
    off = pad_start[None, :] + jnp.cumsum(cnt, axis=0) - cnt
    n_blocks = -(-t_all * TOP_K // MOE_BLOCK) + N_EXPERTS
    block_start = jnp.arange(n_blocks, dtype=jnp.int32) * MOE_BLOCK
    block_expert = jnp.minimum(
        jnp.sum((pad_end[None, :] <= block_start[:, None]).astype(jnp.int32), axis=1), N_EXPERTS - 1)
    n_real = (pad_end[-1:] // MOE_BLOCK).astype(jnp.int32)
    block = jnp.arange(n_blocks, dtype=jnp.int32)
    prev_expert = jnp.concatenate([jnp.full((1,), -1, jnp.int32), block_expert[:-1]])
    run_first = ((block < n_real[0]) & (block_expert != prev_expert)).astype(jnp.int32)
    run_parity = (jnp.cumsum(run_first) - 1) % 2
    after_run = pad_end[block_expert] // MOE_BLOCK
    next_expert = jnp.where(after_run < n_real[0], block_expert[jnp.minimum(after_run, n_blocks - 1)], -1)
    cnt_flat, off_flat = cnt.reshape(-1), off.reshape(-1).astype(jnp.int32)
    xs, slots = _dispatch(cnt_flat, off_flat, x_all, idx, n_blocks * MOE_BLOCK)
    ys = _experts(block_expert, n_real, run_first, run_parity.astype(jnp.int32), next_expert.astype(jnp.int32),
                  xs, w_gu, b_gu, w_dn, b_dn, layer)
    return _combine(cnt_flat, off_flat, x_all, slots, gates, ln_g, ln_b, ys)


def _rope_tables(pos):
    half = HEAD_DIM // 2
    inv_freq = ROPE_THETA ** (-jnp.arange(half, dtype=F32) / half)
    ang = pos.astype(F32)[:, None] * inv_freq[None, :]
    cos = jnp.cos(ang)
    sin = jnp.sin(ang)
    cos = jnp.concatenate([cos, cos], axis=1)
    sin = jnp.concatenate([-sin, sin], axis=1)
    reps = LANES // HEAD_DIM
    return jnp.tile(cos, (1, reps)), jnp.tile(sin, (1, reps))


def _permute_qkv_weight(w_qkv):
    w3 = w_qkv.reshape(D_MODEL, 3, N_HEADS * HEAD_DIM)
    secs = []
    for part in range(3):
        for (h0, h1, _, _) in GROUPS:
            sec = w3[:, part, h0 * HEAD_DIM:h1 * HEAD_DIM]
            secs.append(jnp.pad(sec, ((0, 0), (0, SEC - sec.shape[1]))))
    return jnp.concatenate(secs, axis=1).astype(BF16)


def _permute_out_weight(w_o):
    secs = []
    for (h0, h1, _, _) in GROUPS:
        sec = w_o[h0 * HEAD_DIM:h1 * HEAD_DIM]
        secs.append(jnp.pad(sec, ((0, SEC - sec.shape[0]), (0, 0))))
    return jnp.stack(secs).astype(BF16)


def _sample_step_kernel(*refs, dec_s, aliased):
    y_ref, cos_ref, sin_ref = refs[:3]
    cache_refs = refs[3:3 + N_GROUPS]
    outs = refs[3 + N_GROUPS * (2 if aliased else 1):]
    o_ref, new_refs = outs[0], outs[1:]
    y = y_ref[0]
    cos = cos_ref[...]
    sin = sin_ref[...]
    lane = lax.broadcasted_iota(jnp.int32, (SUB, LANES), 1)
    first_half = (lane & (HEAD_DIM // 2)) == 0
    blocks_per_sec = SEC // LANES
    n_rot = 2 * N_GROUPS * blocks_per_sec
    blocks = []
    for c in range(3 * N_GROUPS * blocks_per_sec):
        blk = y[:, c * LANES:(c + 1) * LANES]
        if c < n_rot:
            swapped = jnp.where(first_half, pltpu.roll(blk, LANES - HEAD_DIM // 2, 1), pltpu.roll(blk, HEAD_DIM // 2, 1))
            blk = blk * cos + swapped * sin
        if c < n_rot // 2:
            blk = blk * (HEAD_DIM ** -0.5)
        blocks.append(blk)

    lo = lane < HEAD_DIM
    new_row = lax.broadcasted_iota(jnp.int32, (SUB, SUB), 1)
    qry_row = lax.broadcasted_iota(jnp.int32, (SUB, SUB), 0)
    pad_rows = jnp.zeros((LANES - SUB, LANES), F32)
    contract_lanes = (((1,), (1,)), ((), ()))
    o_blocks, glses = [], []
    for g in range(N_GROUPS):
        h0, h1, win, d = GROUPS[g]
        n_heads = h1 - h0
        cache_ref, new_ref = cache_refs[g], new_refs[g]
        lb = cache_ref.shape[-1]
        key = lax.broadcasted_iota(jnp.int32, (SUB, lb), 1)
        dist = lb + lax.broadcasted_iota(jnp.int32, (SUB, lb), 0) - key
        visible = ((dist & (d - 1)) == 0) & (dist <= win)
        dist_new = qry_row - new_row
        visible_new = (dist_new >= 0) & ((dist_new & (d - 1)) == 0) & (new_row < dec_s)
        tail_lane = lax.broadcasted_iota(jnp.int32, (HEAD_DIM, LANES), 1)
        lses = []
        for p in range(blocks_per_sec):
            heads = [h for h in (2 * p, 2 * p + 1) if h < n_heads]
            if not heads:
                o_blocks.append(jnp.zeros((SUB, LANES), F32))
                continue
            q2 = blocks[g * blocks_per_sec + p]
            k_new = blocks[(N_GROUPS + g) * blocks_per_sec + p]
            v_new = blocks[(2 * N_GROUPS + g) * blocks_per_sec + p]
            slabs = [[cache_ref[part, h] for h in heads] for part in range(2)]
            zero_slab = jnp.zeros((HEAD_DIM, lb), F32)
            kt2 = jnp.concatenate(slabs[0] + [zero_slab] * (2 - len(heads)), axis=0).astype(BF16)
            vt2 = jnp.concatenate(slabs[1] + [zero_slab] * (2 - len(heads)), axis=0).astype(BF16)
            q_both = jnp.concatenate([jnp.where(lo, q2, 0.0), jnp.where(lo, 0.0, q2)], axis=0).astype(BF16)
            s_both = jnp.dot(q_both, kt2, preferred_element_type=F32)
            s_new_both = lax.dot_general(q_both, k_new.astype(BF16), contract_lanes, preferred_element_type=F32)
            probs, probs_new = [], []
            for half in range(2):
                s = jnp.where(visible, s_both[half * SUB:(half + 1) * SUB], NEG_INF)
                s_new = jnp.where(visible_new, s_new_both[half * SUB:(half + 1) * SUB], NEG_INF)
                mx = jnp.maximum(jnp.max(s, axis=1, keepdims=True), jnp.max(s_new, axis=1, keepdims=True))
                pe = jnp.exp(s - mx)
                pe_new = jnp.exp(s_new - mx)
                l = jnp.sum(pe, axis=1, keepdims=True) + jnp.sum(pe_new, axis=1, keepdims=True)
                probs.append(pe / l)
                probs_new.append(pe_new / l)
                if half < len(heads):
                    lses.append(mx + jnp.log(l))
            pv = lax.dot_general(jnp.concatenate(probs, axis=0).astype(BF16), vt2, contract_lanes, preferred_element_type=F32)
            pv_new = jnp.dot(jnp.concatenate(probs_new, axis=0).astype(BF16), v_new.astype(BF16), preferred_element_type=F32)
            both = pv + pv_new
            o_blocks.append(jnp.where(lo, both[:SUB], both[SUB:]))
            for part, new in enumerate((k_new, v_new)):
                new_t = jnp.concatenate([new, pad_rows], axis=0).T
                for i, h in enumerate(heads):
                    shifted = pltpu.roll(slabs[part][i], lb - dec_s, 1)
                    tail = shifted[:, lb - LANES:]
                    feats = new_t[i * HEAD_DIM:(i + 1) * HEAD_DIM]
                    for jj in range(dec_s):
                        tail = jnp.where(tail_lane == LANES - dec_s + jj, feats[:, jj:jj + 1], tail)
                    if lb > LANES:
                        new_ref[part, h, :, :lb - LANES] = shifted[:, :lb - LANES]
                    new_ref[part, h, :, lb - LANES:] = tail
        lmax = functools.reduce(jnp.maximum, lses)
        lsum = functools.reduce(lambda a, b: a + b, [jnp.exp(l - lmax) for l in lses])
        glses.append(lmax + jnp.log(lsum) - math.log(n_heads))
    gmax = functools.reduce(jnp.maximum, glses)
    es = [jnp.exp(l - gmax) for l in glses]
    inv = float(N_GROUPS) / functools.reduce(lambda a, b: a + b, es)
    scaled = [o_blocks[g * blocks_per_sec + p] * (es[g] * inv) for g in range(N_GROUPS) for p in range(blocks_per_sec)]
    o_ref[0] = jnp.concatenate(scaled, axis=1)


def _sample_step(y_s, caches_t, new_caches_t, layer, cos_s, sin_s, dec_b, dec_s):
    n_cols = 3 * N_GROUPS * SEC
    y_pad = jnp.pad(y_s.reshape(dec_b, dec_s, n_cols), ((0, 0), (0, SUB - dec_s), (0, 0)))
    aliased = new_caches_t is not None
    cache_specs = []
    for (h0, h1, win, d), cache in zip(GROUPS, caches_t):
        assert cache.shape[-1] == win and dec_s <= LANES and win % LANES == 0
        cache_specs.append(pl.BlockSpec((None, None) + cache.shape[2:], lambda b: (layer, b, 0, 0, 0, 0)))
    in_specs = [
        pl.BlockSpec((1, SUB, n_cols), lambda b: (b, 0, 0)),
        pl.BlockSpec((SUB, LANES), lambda b: (0, 0)),
        pl.BlockSpec((SUB, LANES), lambda b: (0, 0)),
    ] + cache_specs
    args = [y_pad, cos_s, sin_s, *caches_t]
    aliases = {}
    if aliased:
        in_specs += [pl.BlockSpec(memory_space=pl.ANY)] * N_GROUPS
        aliases = {len(args) + g: 1 + g for g in range(N_GROUPS)}
        args += list(new_caches_t)
    outs = pl.pallas_call(
        functools.partial(_sample_step_kernel, dec_s=dec_s, aliased=aliased),
        grid=(dec_b,),
        in_specs=in_specs,
        out_specs=[pl.BlockSpec((1, SUB, N_GROUPS * SEC), lambda b: (b, 0, 0))] + cache_specs,
        out_shape=[jax.ShapeDtypeStruct((dec_b, SUB, N_GROUPS * SEC), F32)]
        + [jax.ShapeDtypeStruct(c.shape, F32) for c in caches_t],
        input_output_aliases=aliases,
        compiler_params=_params(52, "arbitrary"),
        name="sample_step",
    )(*args)
    return outs[0][:, :dec_s].reshape(dec_b * dec_s, N_GROUPS * SEC), outs[1:]


def _row2(a):
    return a.reshape(1, -1)


def _attn_layer(x_p, x_s, s_row0, caches_t, new_caches_t, layer, w_qkv, w_o, g1, b1, cos_p, sin_p, dims):
    batch, seq, dec_b, dec_s = dims
    t_prompt = batch * seq
    t_sample = dec_b * dec_s
    w_perm = _permute_qkv_weight(w_qkv)
    wo_perm = _permute_out_weight(w_o)
    outs = _qkv_prompt(x_p, w_perm, cos_p, sin_p, batch, seq)
    qkv_groups, windows = outs[:N_GROUPS], outs[N_GROUPS:]
    os_, ls_ = [], []
    for g in range(N_GROUPS):
        o_g, l_g = _attn_group(qkv_groups[g], g, batch, seq)
        os_.append(o_g)
        ls_.append(l_g)
    x_next = _attn_out_prompt(os_, ls_, wo_perm, x_p, g1, b1, batch, seq, t_prompt + t_sample)
    rows_p = []
    for g, (h0, h1, _, _) in enumerate(GROUPS):
        hg = h1 - h0
        halves = [windows[part * N_GROUPS + g][:, :, :hg * HEAD_DIM].reshape(batch, -1, hg, HEAD_DIM) for part in range(2)]
        rows_p.append(jnp.stack(halves, axis=2))
    y_s = _mm_rows(x_s, w_perm, s_row0, t_sample)
    cos_s, sin_s = _rope_tables(PAST_LEN + jnp.arange(SUB, dtype=jnp.int32))
    o_s, new_caches_t = _sample_step(y_s, caches_t, new_caches_t, layer, cos_s, sin_s, dec_b, dec_s)
    x_all = _proj_ln_rows(o_s, wo_perm.reshape(N_GROUPS * SEC, D_MODEL), x_s, s_row0, g1, b1, x_next, t_prompt)
    return x_all, rows_p, new_caches_t


def _sgu_layer(x_all, w_in, b_in, ln_g, ln_b, ws, bs, w_out, g1, b1, dims):
    batch, seq, dec_b, dec_s = dims
    t_prompt = batch * seq
    t_sample = dec_b * dec_s
    common = (w_in.astype(BF16), _row2(b_in), _row2(ln_g), _row2(ln_b))
    tail = (w_out.astype(BF16), g1, b1)
    bs_p = jnp.broadcast_to(bs[:, :, None], (SGU_GROUPS, SGU_CHUNK, SGU_CHUNK))
    x_next = _sgu(x_all, None, common + (ws, bs_p) + tail, 0, t_prompt, 512, SGU_CHUNK, False)[0]
    c = min(SGU_CHUNK, dec_s)
    reps = SGU_CHUNK // c
    ws_s = jnp.tile(ws[:, :c, :c], (1, reps, reps))
    bs_s = jnp.broadcast_to(jnp.tile(bs[:, :c], (1, reps))[:, :, None], (SGU_GROUPS, SGU_CHUNK, SGU_CHUNK))
    x_all, v_new = _sgu(x_all, x_next, common + (ws_s, bs_s) + tail, t_prompt, t_sample, t_sample, c, True)
    return x_all, v_new.reshape(dec_b, dec_s, D_MODEL)


def _moe_layer(x_all, w_router, b_router, w_gu, b_gu, w_dn, b_dn, ln_g, ln_b, layer):
    w_r = w_router.T
    b_r = jnp.broadcast_to(b_router[:, None], (N_EXPERTS, LANES))
    n_layers = w_gu.shape[0]
    return _moe(
        x_all, w_r, b_r,
        w_gu, b_gu.reshape(n_layers, N_EXPERTS, 1, -1),
        w_dn, b_dn.reshape(n_layers, N_EXPERTS, 1, -1),
        ln_g, ln_b, layer)


def kernel(x_prompt, x_sample, cache_kv_w128, cache_kv_w512, cache_kv_w2048, attn_w_qkv, attn_w_o, sgu_w_in, sgu_b_in, sgu_ln_g, sgu_ln_b, sgu_w_s, sgu_b_s, sgu_w_out, moe_w_router, moe_b_router, moe_w_gu, moe_b_gu, moe_w_down, moe_b_down, ln1_g, ln1_b, ln2_g, ln2_b):
    batch, seq, _ = x_prompt.shape
    dec_b, dec_s, _ = x_sample.shape
    t_prompt = batch * seq
    t_sample = dec_b * dec_s
    caches = (cache_kv_w128, cache_kv_w512, cache_kv_w2048)
    x_all = None
    cos_p, sin_p = _rope_tables(jnp.arange(seq, dtype=jnp.int32))
    row2 = _row2
    dims = (batch, seq, dec_b, dec_s)

    kv_prompt = [[] for _ in GROUPS]
    caches_t = [c.transpose(0, 1, 3, 4, 5, 2) for c in caches]
    new_caches_t = None
    v_rows = []
    for i in range(DEPTH):
        j = i // 2
        g1, b1 = row2(ln1_g[i]), row2(ln1_b[i])
        if i % 2 == 0:
            if x_all is None:
                sources = (x_prompt.reshape(t_prompt, D_MODEL), x_sample.reshape(t_sample, D_MODEL), 0)
            else:
                sources = (x_all, x_all, t_prompt)
            x_all, rows_p, new_caches_t = _attn_layer(
                *sources, caches_t, new_caches_t, j, attn_w_qkv[j], attn_w_o[j], g1, b1, cos_p, sin_p, dims)
            for g in range(N_GROUPS):
                kv_prompt[g].append(rows_p[g])
        else:
            x_all, v_new = _sgu_layer(
                x_all, sgu_w_in[j], sgu_b_in[j], sgu_ln_g[j], sgu_ln_b[j], sgu_w_s[j], sgu_b_s[j], sgu_w_out[j],
                g1, b1, dims)
            v_rows.append(v_new)
        x_all = _moe_layer(
            x_all, moe_w_router[i], moe_b_router[i], moe_w_gu, moe_b_gu, moe_w_down, moe_b_down,
            row2(ln2_g[i]), row2(ln2_b[i]), i)
    y_prompt = x_all[:t_prompt].reshape(batch, seq, D_MODEL)
    y_sample = x_all[t_prompt:].reshape(dec_b, dec_s, D_MODEL)
    kv_sample = [c.transpose(0, 1, 5, 2, 3, 4) for c in new_caches_t]
    return (
        y_prompt, y_sample,
        jnp.stack(kv_prompt[0]), jnp.stack(kv_prompt[1]), jnp.stack(kv_prompt[2]),
        kv_sample[0], kv_sample[1], kv_sample[2],
        jnp.stack(v_rows),
    )
```
